```python
import jax, jax.numpy as jnp
from jax import lax
import numpy as np

D_MODEL = 1024
BATCH = 8
SEQ = 4096
DEPTH = 2

FOURIER_GROUPS = 4
FOURIER_GROUP_DIM = 128
FOURIER_WIDTH = FOURIER_GROUPS * FOURIER_GROUP_DIM

DILATION_PATTERNS = ((128, 1), (512, 4), (2048, 16))
N_DIL_GROUPS = len(DILATION_PATTERNS)
HEADS_PER_GROUP = 4
HEAD_DIM = 64
N_ATTN_HEADS = N_DIL_GROUPS * HEADS_PER_GROUP
ATTN_WIDTH = N_ATTN_HEADS * HEAD_DIM
ATTN_OUT_WIDTH = HEADS_PER_GROUP * HEAD_DIM
Q_BLOCK = 128
ROPE_THETA = 10000.0

IN_COLS = FOURIER_WIDTH + 3 * ATTN_WIDTH + 2 * D_MODEL

FFN_DIM = 3584
N_EXPERTS = 8
TOP_K = 2
N_DENSE_LAYERS = (DEPTH + 1) // 2
N_MOE_LAYERS = DEPTH // 2

RMS_EPS = 1e-6
NEG_INF = -1e30

kernel_name = "hybrid_fourier_dilated_attn_moe_encoder"


def rms_norm(x, gain):
    xf = x.astype(jnp.float32)
    y = xf * lax.rsqrt(jnp.mean(xf * xf, axis=-1, keepdims=True) + RMS_EPS)
    return (y * gain.astype(jnp.float32)).astype(x.dtype)


def rope_tables(seq_len):
    half = HEAD_DIM // 2
    inv_freq = ROPE_THETA ** (-jnp.arange(half, dtype=jnp.float32) * 2.0 / HEAD_DIM)
    pos = jnp.arange(seq_len, dtype=jnp.float32)
    ang = pos[:, None] * inv_freq[None, :]
    return jnp.cos(ang), jnp.sin(ang)


def apply_rope(t, cos, sin):
    half = HEAD_DIM // 2
    tf = t.astype(jnp.float32)
    t1, t2 = tf[..., :half], tf[..., half:]
    c = cos[None, :, None, :]
    s = sin[None, :, None, :]
    return jnp.concatenate([t1 * c - t2 * s, t2 * c + t1 * s], axis=-1).astype(t.dtype)


def fourier_mix(u):
    b, s, _ = u.shape
    ug = u.reshape(b, s, FOURIER_GROUPS, FOURIER_GROUP_DIM).astype(jnp.float32)
    yf = jnp.fft.fft2(ug, axes=(1, 3), norm="ortho").real
    return yf.reshape(b, s, FOURIER_WIDTH).astype(u.dtype)


def dilated_window_attention(q, k, v, window, dilation):
    b, s, h, hd = q.shape
    half_span = window // (2 * dilation)
    offsets = dilation * jnp.arange(-half_span, half_span + 1)
    n_blk = s // Q_BLOCK
    q_blocks = q.reshape(b, n_blk, Q_BLOCK, h, hd).transpose(1, 0, 2, 3, 4)
    starts = jnp.arange(n_blk) * Q_BLOCK

    def one_block(args):
        q_blk, start = args
        pos = start + jnp.arange(Q_BLOCK)
        idx = pos[:, None] + offsets[None, :]
        valid = (idx >= 0) & (idx < s)
        idx_c = jnp.clip(idx, 0, s - 1)
        k_g = k[:, idx_c]
        v_g = v[:, idx_c]
        scores = jnp.einsum("bqhd,bqkhd->bhqk", q_blk, k_g).astype(jnp.float32)
        scores = jnp.where(valid[None, None], scores, NEG_INF)
        lse = jax.nn.logsumexp(scores, axis=-1)
        p = jnp.exp(scores - lse[..., None])
        out = jnp.einsum("bhqk,bqkhd->bqhd", p.astype(v.dtype), v_g)
        return out, lse.transpose(0, 2, 1)

    out, lse = lax.map(one_block, (q_blocks, starts))
    out = out.transpose(1, 0, 2, 3, 4).reshape(b, s, h, hd)
    lse = lse.transpose(1, 0, 2, 3).reshape(b, s, h)
    return out, lse


def swiglu(h, w1, w3, w2):
    return (jax.nn.silu(h @ w1) * (h @ w3)) @ w2


def moe_swiglu(h, w_router, b_router, w1, w3, w2):
    logits = (h @ w_router).astype(jnp.float32) + b_router.astype(jnp.float32)
    top_vals, top_idx = lax.top_k(logits, TOP_K)
    top_w = jax.nn.softmax(top_vals, axis=-1)
    combine = jnp.sum(jax.nn.one_hot(top_idx, N_EXPERTS, dtype=jnp.float32) * top_w[..., None], axis=-2)
    combine = combine.astype(h.dtype)
    out = jnp.zeros_like(h)
    for e in range(N_EXPERTS):
        out = out + combine[..., e:e + 1] * swiglu(h, w1[e], w3[e], w2[e])
    return out


def setup_inputs(seed: int = 0) -> dict:
    key = jax.random.key(seed)
    ks = jax.random.split(key, 20)
    f32 = jnp.float32

    def nrm(k, shape, fan_in):
        return jax.random.normal(k, shape, f32) * (fan_in ** -0.5)

    def gain(k, shape):
        return 1.0 + 0.05 * jax.random.normal(k, shape, f32)

    return {
        "x": jax.random.normal(ks[0], (BATCH, SEQ, D_MODEL), f32),
        "ln_mix": gain(ks[1], (DEPTH, D_MODEL)),
        "w_in": nrm(ks[2], (DEPTH, D_MODEL, IN_COLS), D_MODEL),
        "w_fourier": nrm(ks[3], (DEPTH, FOURIER_WIDTH, D_MODEL), FOURIER_WIDTH),
        "w_attn": nrm(ks[4], (DEPTH, ATTN_OUT_WIDTH, D_MODEL), ATTN_OUT_WIDTH),
        "w_out": nrm(ks[5], (DEPTH, D_MODEL, D_MODEL), D_MODEL),
        "ln_ffn": gain(ks[6], (DEPTH, D_MODEL)),
        "dense_w1": nrm(ks[7], (N_DENSE_LAYERS, D_MODEL, FFN_DIM), D_MODEL),
        "dense_w3": nrm(ks[8], (N_DENSE_LAYERS, D_MODEL, FFN_DIM), D_MODEL),
        "dense_w2": nrm(ks[9], (N_DENSE_LAYERS, FFN_DIM, D_MODEL), FFN_DIM),
        "router_w": nrm(ks[10], (N_MOE_LAYERS, D_MODEL, N_EXPERTS), D_MODEL),
        "router_b": 0.01 * jax.random.normal(ks[11], (N_MOE_LAYERS, N_EXPERTS), f32),
        "moe_w1": nrm(ks[12], (N_MOE_LAYERS, N_EXPERTS, D_MODEL, FFN_DIM), D_MODEL),
        "moe_w3": nrm(ks[13], (N_MOE_LAYERS, N_EXPERTS, D_MODEL, FFN_DIM), D_MODEL),
        "moe_w2": nrm(ks[14], (N_MOE_LAYERS, N_EXPERTS, FFN_DIM, D_MODEL), FFN_DIM),
        "ln_final": gain(ks[15], (D_MODEL,)),
    }


def reference(x, ln_mix, w_in, w_fourier, w_attn, w_out, ln_ffn, dense_w1, dense_w3, dense_w2,
              router_w, router_b, moe_w1, moe_w3, moe_w2, ln_final):
    b, s, _ = x.shape
    cos, sin = rope_tables(s)
    q_scale = HEAD_DIM ** -0.5
    split_points = np.cumsum([FOURIER_WIDTH, ATTN_WIDTH, ATTN_WIDTH, ATTN_WIDTH, D_MODEL]).tolist()

    for layer in range(DEPTH):
        h = rms_norm(x, ln_mix[layer])
        proj = h @ w_in[layer]
        u_f, q, k, v, gate_f, gate_a = jnp.split(proj, split_points, axis=-1)

        y_f = fourier_mix(u_f) @ w_fourier[layer]

        q = apply_rope(q.reshape(b, s, N_ATTN_HEADS, HEAD_DIM), cos, sin) * q_scale
        k = apply_rope(k.reshape(b, s, N_ATTN_HEADS, HEAD_DIM), cos, sin)
        v = v.reshape(b, s, N_ATTN_HEADS, HEAD_DIM)
        outs, lses = [], []
        for g, (window, dilation) in enumerate(DILATION_PATTERNS):
            hs = slice(g * HEADS_PER_GROUP, (g + 1) * HEADS_PER_GROUP)
            o_g, lse_g = dilated_window_attention(q[:, :, hs], k[:, :, hs], v[:, :, hs], window, dilation)
            outs.append(o_g)
            lses.append(lse_g)
        o_all = jnp.stack(outs, axis=0).astype(jnp.float32)
        alpha = jax.nn.softmax(jnp.stack(lses, axis=0), axis=0)
        o_att = jnp.einsum("gbsh,gbshd->bshd", alpha, o_all).astype(x.dtype)
        y_a = o_att.reshape(b, s, ATTN_OUT_WIDTH) @ w_attn[layer]

        g_f = jax.nn.sigmoid(gate_f.astype(jnp.float32)).astype(x.dtype)
        g_a = jax.nn.sigmoid(gate_a.astype(jnp.float32)).astype(x.dtype)
        x = x + (g_f * y_f + g_a * y_a) @ w_out[layer]

        h2 = rms_norm(x, ln_ffn[layer])
        if layer % 2 == 0:
            i = layer // 2
            x = x + swiglu(h2, dense_w1[i], dense_w3[i], dense_w2[i])
        else:
            i = layer // 2
            x = x + moe_swiglu(h2, router_w[i], router_b[i], moe_w1[i], moe_w3[i], moe_w2[i])

    return rms_norm(x, ln_final)
```

```python
import functools

import jax
import jax.numpy as jnp
import numpy as np
from jax import lax
from jax.experimental import pallas as pl
from jax.experimental.pallas import tpu as pltpu

D_MODEL = 1024
FOURIER_GROUPS = 4
FOURIER_GROUP_DIM = 128
FOURIER_WIDTH = FOURIER_GROUPS * FOURIER_GROUP_DIM
DILATIONS = (1, 4, 16)
HALF_SPAN = 64
HEADS_PER_GROUP = 4
HEAD_DIM = 64
GROUP_WIDTH = HEADS_PER_GROUP * HEAD_DIM
ATTN_WIDTH = len(DILATIONS) * GROUP_WIDTH
ROPE_THETA = 10000.0
FFN_DIM = 3584
N_EXPERTS = 8
RMS_EPS = 1e-6
NEG_INF = -1e30
LANES = 128

COL_UF = 0
COL_Q = FOURIER_WIDTH
COL_K = COL_Q + ATTN_WIDTH
COL_V = COL_K + ATTN_WIDTH
COL_GF = COL_V + ATTN_WIDTH
COL_GA = COL_GF + D_MODEL
IN_COLS = COL_GA + D_MODEL

VMEM_LIMIT = 56 * 1024 * 1024

BF16 = jnp.bfloat16
F32 = jnp.float32


def _params(semantics):
    return pltpu.CompilerParams(dimension_semantics=semantics, vmem_limit_bytes=VMEM_LIMIT)


def _rms(x, gain):
    ms = jnp.mean(x * x, axis=-1, keepdims=True)
    return x * lax.rsqrt(ms + RMS_EPS) * gain


def _in_proj_kernel(x_ref, g_ref, w_ref, cc_ref, sc_ref, cos_ref, sin_ref,
                    ab_ref, q0_ref, k0_ref, v0_ref, q1_ref, k1_ref, v1_ref, q2_ref, k2_ref, v2_ref,
                    gf_ref, ga_ref, scr_ref):
    tm = x_ref.shape[1]
    h = _rms(x_ref[0], g_ref[...]).astype(BF16)

    def proj(col, width):
        return jnp.dot(h, w_ref[:, col:col + width], preferred_element_type=F32)

    uf = proj(COL_UF, FOURIER_WIDTH).astype(BF16)
    ab_ref[0] = jnp.dot(uf, cc_ref[...], preferred_element_type=F32).astype(BF16)
    ab_ref[1] = jnp.dot(uf, sc_ref[...], preferred_element_type=F32).astype(BF16)

    cos = cos_ref[...]
    sin = sin_ref[...]

    def rope(t):
        t1, t2 = t[:, :LANES], t[:, LANES:]
        return jnp.concatenate([t1 * cos - t2 * sin, t2 * cos + t1 * sin], axis=-1)

    def put(ref, val, d):
        if d == 1:
            ref[0, 0] = val.astype(BF16)
            return
        for c in range(GROUP_WIDTH // LANES):
            scr_ref[c] = val[:, c * LANES:(c + 1) * LANES]
        for r in range(d):
            for c in range(GROUP_WIDTH // LANES):
                ref[0, r, :, c * LANES:(c + 1) * LANES] = scr_ref[c, pl.ds(r, tm // d, stride=d), :].astype(BF16)

    outs = ((q0_ref, k0_ref, v0_ref), (q1_ref, k1_ref, v1_ref), (q2_ref, k2_ref, v2_ref))
    for g, d in enumerate(DILATIONS):
        qr, kr, vr = outs[g]
        put(qr, rope(proj(COL_Q + g * GROUP_WIDTH, GROUP_WIDTH)) * (HEAD_DIM ** -0.5), d)
        put(kr, rope(proj(COL_K + g * GROUP_WIDTH, GROUP_WIDTH)), d)
        put(vr, proj(COL_V + g * GROUP_WIDTH, GROUP_WIDTH), d)

    gf_ref[0] = jax.nn.sigmoid(proj(COL_GF, D_MODEL)).astype(BF16)
    ga_ref[0] = jax.nn.sigmoid(proj(COL_GA, D_MODEL)).astype(BF16)


def _in_proj(x, gain, w, cc, sc, cos_t, sin_t, tm):
    b, s, _ = x.shape
    grid = (b, s // tm)
    qkv_shapes, qkv_specs = [], []
    for d in DILATIONS:
        for _ in range(3):
            qkv_shapes.append(jax.ShapeDtypeStruct((b, d, s // d, GROUP_WIDTH), BF16))
            qkv_specs.append(pl.BlockSpec((1, d, tm // d, GROUP_WIDTH), lambda bi, si: (bi, 0, si, 0)))
    out_shape = ([jax.ShapeDtypeStruct((2, s, b * FOURIER_WIDTH), BF16)] + qkv_shapes +
                 [jax.ShapeDtypeStruct((b, s, D_MODEL), BF16)] * 2)
    out_specs = ([pl.BlockSpec((2, tm, FOURIER_WIDTH), lambda bi, si: (0, si, bi))] + qkv_specs +
                 [pl.BlockSpec((1, tm, D_MODEL), lambda bi, si: (bi, si, 0))] * 2)
    in_specs = [
        pl.BlockSpec((1, tm, D_MODEL), lambda bi, si: (bi, si, 0)),
        pl.BlockSpec((1, D_MODEL), lambda bi, si: (0, 0)),
        pl.BlockSpec((D_MODEL, IN_COLS), lambda bi, si: (0, 0)),
        pl.BlockSpec((FOURIER_WIDTH, FOURIER_WIDTH), lambda bi, si: (0, 0)),
        pl.BlockSpec((FOURIER_WIDTH, FOURIER_WIDTH), lambda bi, si: (0, 0)),
        pl.BlockSpec((tm, LANES), lambda bi, si: (si, 0)),
        pl.BlockSpec((tm, LANES), lambda bi, si: (si, 0)),
    ]
    return pl.pallas_call(
        _in_proj_kernel, grid=grid, in_specs=in_specs, out_specs=out_specs, out_shape=out_shape,
        scratch_shapes=[pltpu.VMEM((GROUP_WIDTH // LANES, tm, LANES), F32)],
        compiler_params=_params(("parallel", "parallel")), name="in_proj",
    )(x, gain, w, cc, sc, cos_t, sin_t)


def _matmul_kernel(a_ref, b_ref, o_ref, acc_ref):
    @pl.when(pl.program_id(2) == 0)
    def _():
        acc_ref[...] = jnp.zeros_like(acc_ref)

    acc_ref[...] += jnp.dot(a_ref[...], b_ref[...], preferred_element_type=F32)

    @pl.when(pl.program_id(2) == pl.num_programs(2) - 1)
    def _():
        o_ref[...] = acc_ref[...].astype(o_ref.dtype)


def _matmul(a, b, bm, bn, bk, out_dtype):
    m, k = a.shape
    _, n = b.shape
    bm, bn, bk = min(bm, m), min(bn, n), min(bk, k)
    return pl.pallas_call(
        _matmul_kernel, grid=(m // bm, n // bn, k // bk),
        in_specs=[pl.BlockSpec((bm, bk), lambda i, j, kk: (i, kk)),
                  pl.BlockSpec((bk, bn), lambda i, j, kk: (kk, j))],
        out_specs=pl.BlockSpec((bm, bn), lambda i, j, kk: (i, j)),
        out_shape=jax.ShapeDtypeStruct((m, n), out_dtype),
        scratch_shapes=[pltpu.VMEM((bm, bn), F32)],
        compiler_params=_params(("parallel", "parallel", "arbitrary")), name="seq_dft",
    )(a, b)


def _attn_kernel(q_ref, k_ref, v_ref, o_ref, lse_ref, *, tq, tk, seq_len):
    m0 = pl.program_id(2) * tq
    start = pl.multiple_of(jnp.clip(m0 - HALF_SPAN, 0, seq_len - tk), 16)
    q = q_ref[0, 0]
    k = k_ref[0, 0, pl.ds(start, tk), :]
    v = v_ref[0, 0, pl.ds(start, tk), :]
    rel = (start - m0) + lax.broadcasted_iota(jnp.int32, (tq, tk), 1) - lax.broadcasted_iota(jnp.int32, (tq, tk), 0)
    valid = jnp.abs(rel) <= HALF_SPAN
    lane = lax.broadcasted_iota(jnp.int32, (1, GROUP_WIDTH), 1)
    qk_head = (lane % LANES) // (HEAD_DIM // 2)
    v_head = lane // HEAD_DIM
    o_acc = jnp.zeros((tq, GROUP_WIDTH), F32)
    lse_acc = jnp.zeros((tq, GROUP_WIDTH), F32)
    for hd in range(HEADS_PER_GROUP):
        qh = jnp.where(qk_head == hd, q, jnp.zeros_like(q))
        s = lax.dot_general(qh, k, (((1,), (1,)), ((), ())), preferred_element_type=F32)
        s = jnp.where(valid, s, NEG_INF)
        m = jnp.max(s, axis=-1, keepdims=True)
        p = jnp.exp(s - m)
        l = jnp.sum(p, axis=-1, keepdims=True)
        oh = jnp.dot(p.astype(BF16), v, preferred_element_type=F32)
        o_acc = jnp.where(v_head == hd, oh / l, o_acc)
        lse_acc = jnp.where(v_head == hd, m + jnp.log(l), lse_acc)
    o_ref[0, 0] = o_acc
    lse_ref[0, 0] = lse_acc


def _attention(q, k, v):
    b, d, seq_len, _ = q.shape
    tq = min(256, seq_len)
    tk = min(tq + 2 * HALF_SPAN, seq_len)
    kern = functools.partial(_attn_kernel, tq=tq, tk=tk, seq_len=seq_len)
    blk_q = pl.BlockSpec((1, 1, tq, GROUP_WIDTH), lambda bi, ri, i: (bi, ri, i, 0))
    blk_kv = pl.BlockSpec((1, 1, seq_len, GROUP_WIDTH), lambda bi, ri, i: (bi, ri, 0, 0))
    shape = jax.ShapeDtypeStruct((b, d, seq_len, GROUP_WIDTH), F32)
    return pl.pallas_call(
        kern, grid=(b, d, seq_len // tq), in_specs=[blk_q, blk_kv, blk_kv],
        out_specs=[blk_q, blk_q], out_shape=[shape, shape],
        compiler_params=_params(("parallel", "parallel", "parallel")), name=f"attn_d{d}",
    )(q, k, v)


def _mix_out_kernel(*refs, routed):
    (x_ref, yf_ref, o0_ref, l0_ref, o1_ref, l1_ref, o2_ref, l2_ref, gf_ref, ga_ref,
     wf_ref, wa_ref, wo_ref, g2_ref) = refs[:14]
    if routed:
        rw_ref, rb_ref, xn_ref, h2_ref, cw_ref, o_scr, l_scr = refs[14:]
    else:
        xn_ref, h2_ref, o_scr, l_scr = refs[14:]
    tm = x_ref.shape[1]

    def gathered(ref, scr, d):
        if d == 1:
            return ref[0, 0]
        n_half = GROUP_WIDTH // LANES
        for r in range(d):
            for c in range(n_half):
                scr[c, pl.ds(r, tm // d, stride=d), :] = ref[0, r, :, c * LANES:(c + 1) * LANES]
        return jnp.concatenate([scr[c] for c in range(n_half)], axis=-1)

    o_refs, l_refs = (o0_ref, o1_ref, o2_ref), (l0_ref, l1_ref, l2_ref)
    os_, ls_ = [], []
    for g, d in enumerate(DILATIONS):
        os_.append(gathered(o_refs[g], o_scr.at[g], d))
        ls_.append(gathered(l_refs[g], l_scr.at[g], d))
    mx = jnp.maximum(jnp.maximum(ls_[0], ls_[1]), ls_[2])
    es = [jnp.exp(l - mx) for l in ls_]
    den = es[0] + es[1] + es[2]
    o_att = (es[0] * os_[0] + es[1] * os_[1] + es[2] * os_[2]) / den

    y_a = jnp.dot(o_att.astype(BF16), wa_ref[...], preferred_element_type=F32)
    y_f = jnp.dot(yf_ref[...], wf_ref[...], preferred_element_type=F32)
    z = gf_ref[0].astype(F32) * y_f + ga_ref[0].astype(F32) * y_a
    xn = x_ref[0] + jnp.dot(z.astype(BF16), wo_ref[...], preferred_element_type=F32)
    xn_ref[0] = xn
    h2 = _rms(xn, g2_ref[...])
    h2_ref[0] = h2.astype(BF16)

    if routed:
        logits = jnp.dot(h2, rw_ref[...], preferred_element_type=F32,
                         precision=lax.Precision.HIGHEST) + rb_ref[...]
        lane = lax.broadcasted_iota(jnp.int32, logits.shape, 1)
        big = jnp.int32(LANES)
        m1 = jnp.max(logits, axis=-1, keepdims=True)
        i1 = jnp.min(jnp.where(logits == m1, lane, big), axis=-1, keepdims=True)
        rest = jnp.where(lane == i1, NEG_INF * 2, logits)
        m2 = jnp.max(rest, axis=-1, keepdims=True)
        i2 = jnp.min(jnp.where(rest == m2, lane, big), axis=-1, keepdims=True)
        e2 = jnp.exp(m2 - m1)
        w1 = 1.0 / (1.0 + e2)
        w2 = e2 / (1.0 + e2)
        cw_ref[...] = jnp.where(lane == i1, w1, jnp.where(lane == i2, w2, 0.0))


def _mix_out(x, yf, attn, gf, ga, wf, wa, wo, g2, router, tm):
    b, s, _ = x.shape
    routed = router is not None
    row = pl.BlockSpec((1, tm, D_MODEL), lambda bi, si: (bi, si, 0))
    full = lambda shape: pl.BlockSpec(shape, lambda bi, si: (0,) * len(shape))
    in_specs = [row, pl.BlockSpec((tm, FOURIER_WIDTH), lambda bi, si: (si, bi))]
    args = [x, yf]
    for d, (o, l) in zip(DILATIONS, attn):
        spec = pl.BlockSpec((1, d, tm // d, GROUP_WIDTH), lambda bi, si: (bi, 0, si, 0))
        in_specs += [spec, spec]
        args += [o, l]
    in_specs += [row, row, full((FOURIER_WIDTH, D_MODEL)), full((GROUP_WIDTH, D_MODEL)),
                 full((D_MODEL, D_MODEL)), full((1, D_MODEL))]
    args += [gf, ga, wf, wa, wo, g2]
    out_shape = [jax.ShapeDtypeStruct((b, s, D_MODEL), F32), jax.ShapeDtypeStruct((b, s, D_MODEL), BF16)]
    out_specs = [row, row]
    if routed:
        in_specs += [full((D_MODEL, LANES)), full((1, LANES))]
        args += list(router)
        out_shape.append(jax.ShapeDtypeStruct((b * s, LANES), F32))
        out_specs.append(pl.BlockSpec((tm, LANES), lambda bi, si: (bi * (s // tm) + si, 0)))
    return pl.pallas_call(
        functools.partial(_mix_out_kernel, routed=routed), grid=(b, s // tm),
        in_specs=in_specs, out_specs=out_specs, out_shape=out_shape,
        scratch_shapes=[pltpu.VMEM((len(DILATIONS), GROUP_WIDTH // LANES, tm, LANES), F32)] * 2,
        compiler_params=_params(("parallel", "parallel")), name="mix_out",
    )(*args)


def _ffn_kernel(*refs, routed, final_norm):
    if routed:
        h_ref, x_ref, cw_ref, w1_ref, w3_ref, w2_ref, gfin_ref, o_ref = refs
    else:
        h_ref, x_ref, w1_ref, w3_ref, w2_ref, gfin_ref, o_ref = refs
    e, f = pl.program_id(1), pl.program_id(2)

    @pl.when((e == 0) & (f == 0))
    def _():
        o_ref[...] = x_ref[...]

    h = h_ref[...]
    a = jnp.dot(h, w1_ref[0], preferred_element_type=F32)
    g = jnp.dot(h, w3_ref[0], preferred_element_type=F32)
    act = (a * jax.nn.sigmoid(a) * g).astype(BF16)
    y = jnp.dot(act, w2_ref[0], preferred_element_type=F32)
    if routed:
        lane = lax.broadcasted_iota(jnp.int32, cw_ref.shape, 1)
        y = y * jnp.sum(jnp.where(lane == e, cw_ref[...], 0.0), axis=-1, keepdims=True)
    o_ref[...] += y

    if final_norm:
        @pl.when((e == pl.num_programs(1) - 1) & (f == pl.num_programs(2) - 1))
        def _():
            o_ref[...] = _rms(o_ref[...], gfin_ref[...])


def _ffn(h, x, cw, w1, w3, w2, gfin, final_norm, tm, tf):
    n = h.shape[0]
    n_exp = w1.shape[0]
    routed = cw is not None
    row = pl.BlockSpec((tm, D_MODEL), lambda i, e, f: (i, 0))
    in_specs = [row, row]
    args = [h, x]
    if routed:
        in_specs.append(pl.BlockSpec((tm, LANES), lambda i, e, f: (i, 0)))
        args.append(cw)
    in_specs += [pl.BlockSpec((1, D_MODEL, tf), lambda i, e, f: (e, 0, f)),
                 pl.BlockSpec((1, D_MODEL, tf), lambda i, e, f: (e, 0, f)),
                 pl.BlockSpec((1, tf, D_MODEL), lambda i, e, f: (e, f, 0)),
                 pl.BlockSpec((1, D_MODEL), lambda i, e, f: (0, 0))]
    args += [w1, w3, w2, gfin]
    return pl.pallas_call(
        functools.partial(_ffn_kernel, routed=routed, final_norm=final_norm),
        grid=(n // tm, n_exp, FFN_DIM // tf), in_specs=in_specs, out_specs=row,
        out_shape=jax.ShapeDtypeStruct((n, D_MODEL), F32),
        compiler_params=_params(("parallel", "arbitrary", "arbitrary")), name="ffn",
    )(*args)


def _qk_column_order():
    half = HEAD_DIM // 2
    order = []
    for g in range(len(DILATIONS)):
        base = g * GROUP_WIDTH
        for part in range(2):
            for hd in range(HEADS_PER_GROUP):
                order.extend(range(base + hd * HEAD_DIM + part * half, base + hd * HEAD_DIM + (part + 1) * half))
    return np.asarray(order, np.int32)


def _dft_tables(n):
    j = jnp.arange(n, dtype=jnp.int32)
    phase = (j[:, None] * j[None, :]) % n
    ang = phase.astype(F32) * (2.0 * np.pi / n)
    return jnp.cos(ang), jnp.sin(ang)


def _tables(s):
    half = HEAD_DIM // 2
    inv_freq = ROPE_THETA ** (-jnp.arange(half, dtype=F32) * 2.0 / HEAD_DIM)
    ang = jnp.arange(s, dtype=F32)[:, None] * inv_freq[None, :]
    cos_t = jnp.tile(jnp.cos(ang), (1, LANES // half))
    sin_t = jnp.tile(jnp.sin(ang), (1, LANES // half))
    cg, sg = _dft_tables(FOURIER_GROUP_DIM)
    eye = jnp.eye(FOURIER_GROUPS, dtype=F32)
    scale_c = FOURIER_GROUP_DIM ** -0.5
    cc = (jnp.kron(eye, cg) * scale_c).astype(BF16)
    sc = (jnp.kron(eye, sg) * scale_c).astype(BF16)
    cs, ss = _dft_tables(s)
    scale_s = s ** -0.5
    seq = jnp.concatenate([cs * scale_s, ss * (-scale_s)], axis=1).astype(BF16)
    return cos_t, sin_t, cc, sc, seq


def kernel(x, ln_mix, w_in, w_fourier, w_attn, w_out, ln_ffn, dense_w1, dense_w3, dense_w2,
           router_w, router_b, moe_w1, moe_w3, moe_w2, ln_final):
    b, s, _ = x.shape
    depth = w_in.shape[0]
    n = b * s
    tm = 512
    cos_t, sin_t, cc, sc, seq_tab = _tables(s)
    order = _qk_column_order()
    gfin = ln_final.reshape(1, D_MODEL)

    for layer in range(depth):
        w = w_in[layer]
        wq = w[:, COL_Q:COL_K][:, order]
        wk = w[:, COL_K:COL_V][:, order]
        w_l = jnp.concatenate([w[:, :COL_Q], wq, wk, w[:, COL_V:]], axis=1).astype(BF16)

        outs = _in_proj(x, ln_mix[layer].reshape(1, D_MODEL), w_l, cc, sc, cos_t, sin_t, tm)
        ab, qkv, gf, ga = outs[0], outs[1:10], outs[10], outs[11]
        yf = _matmul(seq_tab, ab.reshape(2 * s, b * FOURIER_WIDTH), 1024, 1024, 2048, BF16)
        attn = [_attention(qkv[3 * g], qkv[3 * g + 1], qkv[3 * g + 2]) for g in range(len(DILATIONS))]

        last = layer == depth - 1
        i = layer // 2
        if layer % 2 == 0:
            router = None
        else:
            rw = jnp.zeros((D_MODEL, LANES), F32).at[:, :N_EXPERTS].set(router_w[i])
            rb = jnp.full((1, LANES), NEG_INF, F32).at[0, :N_EXPERTS].set(router_b[i])
            router = (rw, rb)
        res = _mix_out(x, yf, attn, gf, ga, w_fourier[layer].astype(BF16), w_attn[layer].astype(BF16),
                       w_out[layer].astype(BF16), ln_ffn[layer].reshape(1, D_MODEL), router, tm)
        xn, h2 = res[0].reshape(n, D_MODEL), res[1].reshape(n, D_MODEL)
        if layer % 2 == 0:
            y = _ffn(h2, xn, None, dense_w1[i][None].astype(BF16), dense_w3[i][None].astype(BF16),
                     dense_w2[i][None].astype(BF16), gfin, last, 1024, 512)
        else:
            y = _ffn(h2, xn, res[2], moe_w1[i].astype(BF16), moe_w3[i].astype(BF16),
                     moe_w2[i].astype(BF16), gfin, last, 1024, 512)
        x = y.reshape(b, s, D_MODEL)
    return x
```

```python
import functools

import jax
import jax.numpy as jnp
import numpy as np
from jax import lax
from jax.experimental import pallas as pl
from jax.experimental.pallas import tpu as pltpu

D_MODEL = 1024
FOURIER_GROUPS = 4
FOURIER_GROUP_DIM = 128
FOURIER_WIDTH = FOURIER_GROUPS * FOURIER_GROUP_DIM
DILATIONS = (1, 4, 16)
HALF_SPAN = 64
HEADS_PER_GROUP = 4
HEAD_DIM = 64
GROUP_WIDTH = HEADS_PER_GROUP * HEAD_DIM
ATTN_WIDTH = len(DILATIONS) * GROUP_WIDTH
ROPE_THETA = 10000.0
FFN_DIM = 3584
N_EXPERTS = 8
RMS_EPS = 1e-6
NEG_INF = -1e30
LANES = 128

COL_UF = 0
COL_Q = FOURIER_WIDTH
COL_K = COL_Q + ATTN_WIDTH
COL_V = COL_K + ATTN_WIDTH
COL_GF = COL_V + ATTN_WIDTH
COL_GA = COL_GF + D_MODEL
IN_COLS = COL_GA + D_MODEL

VMEM_LIMIT = 56 * 1024 * 1024

BF16 = jnp.bfloat16
F32 = jnp.float32


def _params(semantics):
    return pltpu.CompilerParams(dimension_semantics=semantics, vmem_limit_bytes=VMEM_LIMIT)


def _rms(x, gain):
    ms = jnp.mean(x * x, axis=-1, keepdims=True)
    return x * lax.rsqrt(ms + RMS_EPS) * gain


def _in_proj_kernel(x_ref, g_ref, w_ref, cc_ref, sc_ref, cos_ref, sin_ref,
                    ab_ref, q0_ref, k0_ref, v0_ref, q1_ref, k1_ref, v1_ref, q2_ref, k2_ref, v2_ref,
                    gf_ref, ga_ref, scr_ref):
    tm = x_ref.shape[1]
    h = _rms(x_ref[0], g_ref[...]).astype(BF16)

    def proj(col, width):
        return jnp.dot(h, w_ref[:, col:col + width], preferred_element_type=F32)

    uf = proj(COL_UF, FOURIER_WIDTH).astype(BF16)
    ab_ref[0] = jnp.dot(uf, cc_ref[...], preferred_element_type=F32).astype(BF16)
    ab_ref[1] = jnp.dot(uf, sc_ref[...], preferred_element_type=F32).astype(BF16)

    cos = cos_ref[...]
    sin = sin_ref[...]

    def rope(t):
        t1, t2 = t[:, :LANES], t[:, LANES:]
        return jnp.concatenate([t1 * cos - t2 * sin, t2 * cos + t1 * sin], axis=-1)

    def put(ref, val, d):
        if d == 1:
            ref[0, 0] = val.astype(BF16)
            return
        for c in range(GROUP_WIDTH // LANES):
            scr_ref[c] = val[:, c * LANES:(c + 1) * LANES]
        for r in range(d):
            for c in range(GROUP_WIDTH // LANES):
                ref[0, r, :, c * LANES:(c + 1) * LANES] = scr_ref[c, pl.ds(r, tm // d, stride=d), :].astype(BF16)

    outs = ((q0_ref, k0_ref, v0_ref), (q1_ref, k1_ref, v1_ref), (q2_ref, k2_ref, v2_ref))
    for g, d in enumerate(DILATIONS):
        qr, kr, vr = outs[g]
        put(qr, rope(proj(COL_Q + g * GROUP_WIDTH, GROUP_WIDTH)) * (HEAD_DIM ** -0.5), d)
        put(kr, rope(proj(COL_K + g * GROUP_WIDTH, GROUP_WIDTH)), d)
        put(vr, proj(COL_V + g * GROUP_WIDTH, GROUP_WIDTH), d)

    gf_ref[0] = jax.nn.sigmoid(proj(COL_GF, D_MODEL)).astype(BF16)
    ga_ref[0] = jax.nn.sigmoid(proj(COL_GA, D_MODEL)).astype(BF16)


def _in_proj(x, gain, w, cc, sc, cos_t, sin_t, tm):
    b, s, _ = x.shape
    grid = (b, s // tm)
    qkv_shapes, qkv_specs = [], []
    for d in DILATIONS:
        for _ in range(3):
            qkv_shapes.append(jax.ShapeDtypeStruct((b, d, s // d, GROUP_WIDTH), BF16))
            qkv_specs.append(pl.BlockSpec((1, d, tm // d, GROUP_WIDTH), lambda bi, si: (bi, 0, si, 0)))
    out_shape = ([jax.ShapeDtypeStruct((2, s, b * FOURIER_WIDTH), BF16)] + qkv_shapes +
                 [jax.ShapeDtypeStruct((b, s, D_MODEL), BF16)] * 2)
    out_specs = ([pl.BlockSpec((2, tm, FOURIER_WIDTH), lambda bi, si: (0, si, bi))] + qkv_specs +
                 [pl.BlockSpec((1, tm, D_MODEL), lambda bi, si: (bi, si, 0))] * 2)
    in_specs = [
        pl.BlockSpec((1, tm, D_MODEL), lambda bi, si: (bi, si, 0)),
        pl.BlockSpec((1, D_MODEL), lambda bi, si: (0, 0)),
        pl.BlockSpec((D_MODEL, IN_COLS), lambda bi, si: (0, 0)),
        pl.BlockSpec((FOURIER_WIDTH, FOURIER_WIDTH), lambda bi, si: (0, 0)),
        pl.BlockSpec((FOURIER_WIDTH, FOURIER_WIDTH), lambda bi, si: (0, 0)),
        pl.BlockSpec((tm, LANES), lambda bi, si: (si, 0)),
        pl.BlockSpec((tm, LANES), lambda bi, si: (si, 0)),
    ]
    return pl.pallas_call(
        _in_proj_kernel, grid=grid, in_specs=in_specs, out_specs=out_specs, out_shape=out_shape,
        scratch_shapes=[pltpu.VMEM((GROUP_WIDTH // LANES, tm, LANES), F32)],
        compiler_params=_params(("parallel", "parallel")), name="in_proj",
    )(x, gain, w, cc, sc, cos_t, sin_t)


def _matmul_kernel(a_ref, b_ref, o_ref, acc_ref):
    @pl.when(pl.program_id(2) == 0)
    def _():
        acc_ref[...] = jnp.zeros_like(acc_ref)

    acc_ref[...] += jnp.dot(a_ref[...], b_ref[...], preferred_element_type=F32)

    @pl.when(pl.program_id(2) == pl.num_programs(2) - 1)
    def _():
        o_ref[...] = acc_ref[...].astype(o_ref.dtype)


def _matmul(a, b, bm, bn, bk, out_dtype):
    m, k = a.shape
    _, n = b.shape
    bm, bn, bk = min(bm, m), min(bn, n), min(bk, k)
    return pl.pallas_call(
        _matmul_kernel, grid=(m // bm, n // bn, k // bk),
        in_specs=[pl.BlockSpec((bm, bk), lambda i, j, kk: (i, kk)),
                  pl.BlockSpec((bk, bn), lambda i, j, kk: (kk, j))],
        out_specs=pl.BlockSpec((bm, bn), lambda i, j, kk: (i, j)),
        out_shape=jax.ShapeDtypeStruct((m, n), out_dtype),
        scratch_shapes=[pltpu.VMEM((bm, bn), F32)],
        compiler_params=_params(("parallel", "parallel", "arbitrary")), name="seq_dft",
    )(a, b)


def _attn_kernel(q_ref, k_ref, v_ref, o_ref, lse_ref, *, tq, tk, seq_len):
    m0 = pl.program_id(2) * tq
    start = pl.multiple_of(jnp.clip(m0 - HALF_SPAN, 0, seq_len - tk), 16)
    q = q_ref[0, 0]
    k = k_ref[0, 0, pl.ds(start, tk), :]
    v = v_ref[0, 0, pl.ds(start, tk), :]
    rel = (start - m0) + lax.broadcasted_iota(jnp.int32, (tq, tk), 1) - lax.broadcasted_iota(jnp.int32, (tq, tk), 0)
    valid = jnp.abs(rel) <= HALF_SPAN
    lane = lax.broadcasted_iota(jnp.int32, (1, GROUP_WIDTH), 1)
    qk_head = (lane % LANES) // (HEAD_DIM // 2)
    v_head = lane // HEAD_DIM
    o_acc = jnp.zeros((tq, GROUP_WIDTH), F32)
    lse_acc = jnp.zeros((tq, GROUP_WIDTH), F32)
    for hd in range(HEADS_PER_GROUP):
        qh = jnp.where(qk_head == hd, q, jnp.zeros_like(q))
        s = lax.dot_general(qh, k, (((1,), (1,)), ((), ())), preferred_element_type=F32)
        s = jnp.where(valid, s, NEG_INF)
        m = jnp.max(s, axis=-1, keepdims=True)
        p = jnp.exp(s - m)
        l = jnp.sum(p, axis=-1, keepdims=True)
        oh = jnp.dot(p.astype(BF16), v, preferred_element_type=F32)
        o_acc = jnp.where(v_head == hd, oh / l, o_acc)
        lse_acc = jnp.where(v_head == hd, m + jnp.log(l), lse_acc)
    o_ref[0, 0] = o_acc
    lse_ref[0, 0] = lse_acc


def _attention(q, k, v):
    b, d, seq_len, _ = q.shape
    tq = min(256, seq_len)
    tk = min(tq + 2 * HALF_SPAN, seq_len)
    kern = functools.partial(_attn_kernel, tq=tq, tk=tk, seq_len=seq_len)
    blk_q = pl.BlockSpec((1, 1, tq, GROUP_WIDTH), lambda bi, ri, i: (bi, ri, i, 0))
    blk_kv = pl.BlockSpec((1, 1, seq_len, GROUP_WIDTH), lambda bi, ri, i: (bi, ri, 0, 0))
    shape = jax.ShapeDtypeStruct((b, d, seq_len, GROUP_WIDTH), F32)
    return pl.pallas_call(
        kern, grid=(b, d, seq_len // tq), in_specs=[blk_q, blk_kv, blk_kv],
        out_specs=[blk_q, blk_q], out_shape=[shape, shape],
        compiler_params=_params(("parallel", "parallel", "parallel")), name=f"attn_d{d}",
    )(q, k, v)


def _mix_out_kernel(*refs, routed):
    (x_ref, yf_ref, o0_ref, l0_ref, o1_ref, l1_ref, o2_ref, l2_ref, gf_ref, ga_ref,
     wf_ref, wa_ref, wo_ref, g2_ref) = refs[:14]
    if routed:
        rw_ref, rb_ref, xn_ref, h2_ref, cw_ref, sel_ref, sel_t_ref, o_scr, l_scr = refs[14:]
    else:
        xn_ref, h2_ref, o_scr, l_scr = refs[14:]
    tm = x_ref.shape[1]

    def gathered(ref, scr, d):
        if d == 1:
            return ref[0, 0]
        n_half = GROUP_WIDTH // LANES
        for r in range(d):
            for c in range(n_half):
                scr[c, pl.ds(r, tm // d, stride=d), :] = ref[0, r, :, c * LANES:(c + 1) * LANES]
        return jnp.concatenate([scr[c] for c in range(n_half)], axis=-1)

    o_refs, l_refs = (o0_ref, o1_ref, o2_ref), (l0_ref, l1_ref, l2_ref)
    os_, ls_ = [], []
    for g, d in enumerate(DILATIONS):
        os_.append(gathered(o_refs[g], o_scr.at[g], d))
        ls_.append(gathered(l_refs[g], l_scr.at[g], d))
    mx = jnp.maximum(jnp.maximum(ls_[0], ls_[1]), ls_[2])
    es = [jnp.exp(l - mx) for l in ls_]
    den = es[0] + es[1] + es[2]
    o_att = (es[0] * os_[0] + es[1] * os_[1] + es[2] * os_[2]) / den

    y_a = jnp.dot(o_att.astype(BF16), wa_ref[...], preferred_element_type=F32)
    y_f = jnp.dot(yf_ref[...], wf_ref[...], preferred_element_type=F32)
    z = gf_ref[0].astype(F32) * y_f + ga_ref[0].astype(F32) * y_a
    xn = x_ref[0] + jnp.dot(z.astype(BF16), wo_ref[...], preferred_element_type=F32)
    xn_ref[0] = xn
    h2 = _rms(xn, g2_ref[...])
    h2_hi = h2.astype(BF16)
    h2_ref[0] = h2_hi

    if routed:
        h2_lo = (h2 - h2_hi.astype(F32)).astype(BF16)
        logits = (jnp.dot(h2_hi, rw_ref[0], preferred_element_type=F32)
                  + jnp.dot(h2_hi, rw_ref[1], preferred_element_type=F32)
                  + jnp.dot(h2_lo, rw_ref[0], preferred_element_type=F32)) + rb_ref[...]
        lane = lax.broadcasted_iota(jnp.int32, logits.shape, 1)
        big = jnp.int32(LANES)
        m1 = jnp.max(logits, axis=-1, keepdims=True)
        i1 = jnp.min(jnp.where(logits == m1, lane, big), axis=-1, keepdims=True)
        rest = jnp.where(lane == i1, NEG_INF * 2, logits)
        m2 = jnp.max(rest, axis=-1, keepdims=True)
        i2 = jnp.min(jnp.where(rest == m2, lane, big), axis=-1, keepdims=True)
        e2 = jnp.exp(m2 - m1)
        w1 = 1.0 / (1.0 + e2)
        w2 = e2 / (1.0 + e2)
        cw_ref[...] = jnp.where(lane == i1, w1, jnp.where(lane == i2, w2, 0.0))
        sel = jnp.where((lane == i1) | (lane == i2), 1.0, 0.0)
        sel_ref[...] = sel
        sel_t_ref[...] = sel.T[:N_EXPERTS, :]


def _mix_out(x, yf, attn, gf, ga, wf, wa, wo, g2, router, tm):
    b, s, _ = x.shape
    routed = router is not None
    row = pl.BlockSpec((1, tm, D_MODEL), lambda bi, si: (bi, si, 0))
    full = lambda shape: pl.BlockSpec(shape, lambda bi, si: (0,) * len(shape))
    in_specs = [row, pl.BlockSpec((tm, FOURIER_WIDTH), lambda bi, si: (si, bi))]
    args = [x, yf]
    for d, (o, l) in zip(DILATIONS, attn):
        spec = pl.BlockSpec((1, d, tm // d, GROUP_WIDTH), lambda bi, si: (bi, 0, si, 0))
        in_specs += [spec, spec]
        args += [o, l]
    in_specs += [row, row, full((FOURIER_WIDTH, D_MODEL)), full((GROUP_WIDTH, D_MODEL)),
                 full((D_MODEL, D_MODEL)), full((1, D_MODEL))]
    args += [gf, ga, wf, wa, wo, g2]
    out_shape = [jax.ShapeDtypeStruct((b, s, D_MODEL), F32), jax.ShapeDtypeStruct((b, s, D_MODEL), BF16)]
    out_specs = [row, row]
    if routed:
        in_specs += [full((2, D_MODEL, LANES)), full((1, LANES))]
        args += list(router)
        tok = pl.BlockSpec((tm, LANES), lambda bi, si: (bi * (s // tm) + si, 0))
        out_shape += [jax.ShapeDtypeStruct((b * s, LANES), F32)] * 2 + [jax.ShapeDtypeStruct((N_EXPERTS, b * s), F32)]
        out_specs += [tok, tok, pl.BlockSpec((N_EXPERTS, tm), lambda bi, si: (0, bi * (s // tm) + si))]
    return pl.pallas_call(
        functools.partial(_mix_out_kernel, routed=routed), grid=(b, s // tm),
        in_specs=in_specs, out_specs=out_specs, out_shape=out_shape,
        scratch_shapes=[pltpu.VMEM((len(DILATIONS), GROUP_WIDTH // LANES, tm, LANES), F32)] * 2,
        compiler_params=_params(("parallel", "parallel")), name="mix_out",
    )(*args)


def _swiglu(h, w1, w3, w2):
    a = jnp.dot(h, w1, preferred_element_type=F32)
    g = jnp.dot(h, w3, preferred_element_type=F32)
    act = (a * jax.nn.sigmoid(a) * g).astype(BF16)
    return jnp.dot(act, w2, preferred_element_type=F32)


def _ffn_kernel(h_ref, x_ref, w1_ref, w3_ref, w2_ref, gfin_ref, o_ref, *, final_norm):
    f = pl.program_id(1)

    @pl.when(f == 0)
    def _():
        o_ref[...] = x_ref[...]

    o_ref[...] += _swiglu(h_ref[...], w1_ref[...], w3_ref[...], w2_ref[...])

    if final_norm:
        @pl.when(f == pl.num_programs(1) - 1)
        def _():
            o_ref[...] = _rms(o_ref[...], gfin_ref[...])


def _ffn(h, x, w1, w3, w2, gfin, final_norm, tm, tf):
    n = h.shape[0]
    row = pl.BlockSpec((tm, D_MODEL), lambda i, f: (i, 0))
    in_specs = [row, row,
                pl.BlockSpec((D_MODEL, tf), lambda i, f: (0, f)),
                pl.BlockSpec((D_MODEL, tf), lambda i, f: (0, f)),
                pl.BlockSpec((tf, D_MODEL), lambda i, f: (f, 0)),
                pl.BlockSpec((1, D_MODEL), lambda i, f: (0, 0))]
    return pl.pallas_call(
        functools.partial(_ffn_kernel, final_norm=final_norm),
        grid=(n // tm, FFN_DIM // tf), in_specs=in_specs, out_specs=row,
        out_shape=jax.ShapeDtypeStruct((n, D_MODEL), F32),
        compiler_params=_params(("parallel", "arbitrary")), name="ffn",
    )(h, x, w1, w3, w2, gfin)


MOE_SUB = 256
MOE_WIN = 128
MOE_ROWS = 128
ROW_ALIGN = 16


def _moe_kernel(h_ref, sel_t_ref, sel_ref, cw_ref, w1_ref, w3_ref, w2_ref, triu_ref, tril_ref, o_ref,
                pos_t_ref, pos_ref, hs_ref, y_ref, pre_ref):
    tb = h_ref.shape[0]
    n_sub = tb // MOE_SUB
    blk, e, f = pl.program_id(0), pl.program_id(1), pl.program_id(2)
    n_f = pl.num_programs(2)

    @pl.when((blk == 0) & (e == 0) & (f == 0))
    def _():
        hs_ref[...] = jnp.zeros_like(hs_ref)
        y_ref[...] = jnp.zeros_like(y_ref)

    @pl.when((e == 0) & (f == 0))
    def _():
        o_ref[...] = jnp.zeros_like(o_ref)
        off_t = jnp.zeros((N_EXPERTS, 1), F32)
        off = jnp.zeros((1, LANES), F32)
        for s in range(n_sub):
            sl = slice(s * MOE_SUB, (s + 1) * MOE_SUB)
            sel_t = sel_t_ref[:, sl]
            within_t = jnp.dot(sel_t.astype(BF16), triu_ref[...], preferred_element_type=F32)
            pos_t_ref[:, sl] = jnp.where(sel_t > 0.0, within_t + off_t, -1.0)
            off_t = off_t + jnp.sum(sel_t, axis=1, keepdims=True)
            sel = sel_ref[sl, :]
            within = jnp.dot(tril_ref[...], sel.astype(BF16), preferred_element_type=F32)
            pos_ref[sl, :] = jnp.where(sel > 0.0, within + off, -1.0)
            off = off + jnp.sum(sel, axis=0, keepdims=True)

    def window(s):
        p0, p1 = pre_ref[s], pre_ref[s + 1]
        base = (p0 // ROW_ALIGN) * ROW_ALIGN
        return p0, base, (p1 - base + MOE_WIN - 1) // MOE_WIN

    @pl.when(f == 0)
    def _():
        run = jnp.int32(0)
        pre_ref[0] = run
        for s in range(n_sub):
            cnt = jnp.sum(sel_t_ref[pl.ds(e, 1), s * MOE_SUB:(s + 1) * MOE_SUB])
            run = run + cnt.astype(jnp.int32)
            pre_ref[s + 1] = run
        row = lax.broadcasted_iota(jnp.int32, (MOE_WIN, MOE_SUB), 0)
        for s in range(n_sub):
            p0, base, n_win = window(s)
            sl = slice(s * MOE_SUB, (s + 1) * MOE_SUB)
            pos_i = pos_t_ref[pl.ds(e, 1), sl].astype(jnp.int32)
            h_s = h_ref[sl, :]

            def gather(j, carry, p0=p0, base=base, pos_i=pos_i, h_s=h_s):
                r0 = pl.multiple_of(base + j * MOE_WIN, ROW_ALIGN)
                one_hot = jnp.where(row == pos_i - r0, 1.0, 0.0).astype(BF16)
                rows = jnp.dot(one_hot, h_s, preferred_element_type=F32).astype(BF16)
                keep = lax.broadcasted_iota(jnp.int32, (MOE_WIN, 1), 0) < p0 - r0
                hs_ref[pl.ds(r0, MOE_WIN), :] = jnp.where(keep, hs_ref[pl.ds(r0, MOE_WIN), :], rows)
                return carry

            lax.fori_loop(0, n_win, gather, 0)

    n_chunks = (pre_ref[n_sub] + MOE_ROWS - 1) // MOE_ROWS

    def expert_rows(r0, n_rows):
        y = _swiglu(hs_ref[pl.ds(r0, n_rows), :], w1_ref[0], w3_ref[0], w2_ref[0])

        @pl.when(f == 0)
        def _():
            y_ref[pl.ds(r0, n_rows), :] = y

        @pl.when(f > 0)
        def _():
            y_ref[pl.ds(r0, n_rows), :] += y

    def chunk_pair(c, carry):
        expert_rows(pl.multiple_of(c * (2 * MOE_ROWS), 2 * MOE_ROWS), 2 * MOE_ROWS)
        return carry

    lax.fori_loop(0, n_chunks // 2, chunk_pair, 0)

    @pl.when(n_chunks % 2 == 1)
    def _():
        expert_rows(pl.multiple_of((n_chunks - 1) * MOE_ROWS, MOE_ROWS), MOE_ROWS)

    @pl.when(f == n_f - 1)
    def _():
        lane = lax.broadcasted_iota(jnp.int32, (MOE_SUB, LANES), 1)
        for s in range(n_sub):
            _, base, n_win = window(s)
            sl = slice(s * MOE_SUB, (s + 1) * MOE_SUB)
            mine = lane == e
            pos_i = jnp.sum(jnp.where(mine, pos_ref[sl, :], 0.0), axis=1, keepdims=True).astype(jnp.int32)
            weight = jnp.sum(jnp.where(mine, cw_ref[sl, :], 0.0), axis=1, keepdims=True)

            def scatter(j, carry, base=base, sl=sl, pos_i=pos_i, weight=weight):
                r0 = pl.multiple_of(base + j * MOE_WIN, ROW_ALIGN)
                one_hot = jnp.where(lane == pos_i - r0, 1.0, 0.0).astype(BF16)
                rows = y_ref[pl.ds(r0, MOE_WIN), :].astype(BF16)
                back = jnp.dot(one_hot, rows, preferred_element_type=F32)
                o_ref[sl, :] = (o_ref[sl, :].astype(F32) + back * weight).astype(BF16)
                return carry

            lax.fori_loop(0, n_win, scatter, 0)


def _moe(h, sel_t, sel, cw, w1, w3, w2, tb, tf):
    n = h.shape[0]
    tb = min(tb, n)
    n_exp = w1.shape[0]
    triu = jnp.triu(jnp.ones((MOE_SUB, MOE_SUB), F32), k=1).astype(BF16)
    pad = 2 * MOE_WIN
    tok = pl.BlockSpec((tb, LANES), lambda i, e, f: (i, 0))
    tri = pl.BlockSpec((MOE_SUB, MOE_SUB), lambda i, e, f: (0, 0))
    in_specs = [pl.BlockSpec((tb, D_MODEL), lambda i, e, f: (i, 0)),
                pl.BlockSpec((N_EXPERTS, tb), lambda i, e, f: (0, i)),
                tok, tok,
                pl.BlockSpec((1, D_MODEL, tf), lambda i, e, f: (e, 0, f)),
                pl.BlockSpec((1, D_MODEL, tf), lambda i, e, f: (e, 0, f)),
                pl.BlockSpec((1, tf, D_MODEL), lambda i, e, f: (e, f, 0)),
                tri, tri]
    return pl.pallas_call(
        _moe_kernel, grid=(n // tb, n_exp, FFN_DIM // tf), in_specs=in_specs,
        out_specs=pl.BlockSpec((tb, D_MODEL), lambda i, e, f: (i, 0)),
        out_shape=jax.ShapeDtypeStruct((n, D_MODEL), BF16),
        scratch_shapes=[pltpu.VMEM((N_EXPERTS, tb), F32),
                        pltpu.VMEM((tb, LANES), F32),
                        pltpu.VMEM((tb + pad, D_MODEL), BF16),
                        pltpu.VMEM((tb + pad, D_MODEL), F32),
                        pltpu.SMEM((tb // MOE_SUB + 1,), jnp.int32)],
        compiler_params=_params(("arbitrary", "arbitrary", "arbitrary")), name="moe",
    )(h, sel_t, sel, cw, w1, w3, w2, triu, triu.T)


def _residual_kernel(x_ref, y_ref, g_ref, o_ref, *, final_norm):
    v = x_ref[...] + y_ref[...].astype(F32)
    o_ref[...] = _rms(v, g_ref[...]) if final_norm else v


def _residual(x, y, gfin, final_norm, tm):
    n = x.shape[0]
    row = pl.BlockSpec((tm, D_MODEL), lambda i: (i, 0))
    return pl.pallas_call(
        functools.partial(_residual_kernel, final_norm=final_norm), grid=(n // tm,),
        in_specs=[row, row, pl.BlockSpec((1, D_MODEL), lambda i: (0, 0))], out_specs=row,
        out_shape=jax.ShapeDtypeStruct((n, D_MODEL), F32),
        compiler_params=_params(("parallel",)), name="residual",
    )(x, y, gfin)


def _qk_column_order():
    half = HEAD_DIM // 2
    order = []
    for g in range(len(DILATIONS)):
        base = g * GROUP_WIDTH
        for part in range(2):
            for hd in range(HEADS_PER_GROUP):
                order.extend(range(base + hd * HEAD_DIM + part * half, base + hd * HEAD_DIM + (part + 1) * half))
    return np.asarray(order, np.int32)


def _dft_tables(n):
    j = jnp.arange(n, dtype=jnp.int32)
    phase = (j[:, None] * j[None, :]) % n
    ang = phase.astype(F32) * (2.0 * np.pi / n)
    return jnp.cos(ang), jnp.sin(ang)


def _tables(s):
    half = HEAD_DIM // 2
    inv_freq = ROPE_THETA ** (-jnp.arange(half, dtype=F32) * 2.0 / HEAD_DIM)
    ang = jnp.arange(s, dtype=F32)[:, None] * inv_freq[None, :]
    cos_t = jnp.tile(jnp.cos(ang), (1, LANES // half))
    sin_t = jnp.tile(jnp.sin(ang), (1, LANES // half))
    cg, sg = _dft_tables(FOURIER_GROUP_DIM)
    eye = jnp.eye(FOURIER_GROUPS, dtype=F32)
    scale_c = FOURIER_GROUP_DIM ** -0.5
    cc = (jnp.kron(eye, cg) * scale_c).astype(BF16)
    sc = (jnp.kron(eye, sg) * scale_c).astype(BF16)
    cs, ss = _dft_tables(s)
    scale_s = s ** -0.5
    seq = jnp.concatenate([cs * scale_s, ss * (-scale_s)], axis=1).astype(BF16)
    return cos_t, sin_t, cc, sc, seq


def kernel(x, ln_mix, w_in, w_fourier, w_attn, w_out, ln_ffn, dense_w1, dense_w3, dense_w2,
           router_w, router_b, moe_w1, moe_w3, moe_w2, ln_final):
    b, s, _ = x.shape
    depth = w_in.shape[0]
    n = b * s
    tm = 512
    cos_t, sin_t, cc, sc, seq_tab = _tables(s)
    order = _qk_column_order()
    gfin = ln_final.reshape(1, D_MODEL)

    for layer in range(depth):
        w = w_in[layer]
        wq = w[:, COL_Q:COL_K][:, order]
        wk = w[:, COL_K:COL_V][:, order]
        w_l = jnp.concatenate([w[:, :COL_Q], wq, wk, w[:, COL_V:]], axis=1).astype(BF16)

        outs = _in_proj(x, ln_mix[layer].reshape(1, D_MODEL), w_l, cc, sc, cos_t, sin_t, tm)
        ab, qkv, gf, ga = outs[0], outs[1:10], outs[10], outs[11]
        yf = _matmul(seq_tab, ab.reshape(2 * s, b * FOURIER_WIDTH), 1024, 1024, 2048, BF16)
        attn = [_attention(qkv[3 * g], qkv[3 * g + 1], qkv[3 * g + 2]) for g in range(len(DILATIONS))]

        last = layer == depth - 1
        i = layer // 2
        if layer % 2 == 0:
            router = None
        else:
            rw = jnp.zeros((D_MODEL, LANES), F32).at[:, :N_EXPERTS].set(router_w[i])
            rw_hi = rw.astype(BF16)
            rw_lo = (rw - rw_hi.astype(F32)).astype(BF16)
            rb = jnp.full((1, LANES), NEG_INF, F32).at[0, :N_EXPERTS].set(router_b[i])
            router = (jnp.stack([rw_hi, rw_lo]), rb)
        res = _mix_out(x, yf, attn, gf, ga, w_fourier[layer].astype(BF16), w_attn[layer].astype(BF16),
                       w_out[layer].astype(BF16), ln_ffn[layer].reshape(1, D_MODEL), router, tm)
        xn, h2 = res[0].reshape(n, D_MODEL), res[1].reshape(n, D_MODEL)
        if layer % 2 == 0:
            y = _ffn(h2, xn, dense_w1[i].astype(BF16), dense_w3[i].astype(BF16),
                     dense_w2[i].astype(BF16), gfin, last, 1024, 512)
        else:
            y = _moe(h2, res[4], res[3], res[2], moe_w1[i].astype(BF16), moe_w3[i].astype(BF16),
                     moe_w2[i].astype(BF16), 2048, 896)
            y = _residual(xn, y, gfin, last, 1024)
        x = y.reshape(b, s, D_MODEL)
    return x
```

```python
import functools

import jax
import jax.numpy as jnp
import numpy as np
from jax import lax
from jax.experimental import pallas as pl
from jax.experimental.pallas import tpu as pltpu

D_MODEL = 1024
FOURIER_GROUPS = 4
FOURIER_GROUP_DIM = 128
FOURIER_WIDTH = FOURIER_GROUPS * FOURIER_GROUP_DIM
DILATIONS = (1, 4, 16)
HALF_SPAN = 64
HEADS_PER_GROUP = 4
HEAD_DIM = 64
GROUP_WIDTH = HEADS_PER_GROUP * HEAD_DIM
ATTN_WIDTH = len(DILATIONS) * GROUP_WIDTH
ROPE_THETA = 10000.0
FFN_DIM = 3584
N_EXPERTS = 8
RMS_EPS = 1e-6
NEG_INF = -1e30
LOG2E = float(np.log2(np.e))
LN2 = float(np.log(2.0))
LANES = 128

COL_UF = 0
COL_Q = FOURIER_WIDTH
COL_K = COL_Q + ATTN_WIDTH
COL_V = COL_K + ATTN_WIDTH
COL_GF = COL_V + ATTN_WIDTH
COL_GA = COL_GF + D_MODEL
IN_COLS = COL_GA + D_MODEL

VMEM_LIMIT = 56 * 1024 * 1024

BF16 = jnp.bfloat16
F32 = jnp.float32


def _params(semantics):
    return pltpu.CompilerParams(dimension_semantics=semantics, vmem_limit_bytes=VMEM_LIMIT)


def _rms(x, gain):
    ms = jnp.mean(x * x, axis=-1, keepdims=True)
    return x * lax.rsqrt(ms + RMS_EPS) * gain


def _in_proj_kernel(x_ref, g_ref, w_ref, cc_ref, sc_ref, cos_ref, sin_ref,
                    ab_ref, q0_ref, k0_ref, v0_ref, q1_ref, k1_ref, v1_ref, q2_ref, k2_ref, v2_ref,
                    gf_ref, ga_ref, scr_ref):
    tm = x_ref.shape[1]
    h = _rms(x_ref[0], g_ref[...]).astype(BF16)

    def proj(col, width):
        return jnp.dot(h, w_ref[:, col:col + width], preferred_element_type=F32)

    uf = proj(COL_UF, FOURIER_WIDTH).astype(BF16)
    ab_ref[0] = jnp.dot(uf, cc_ref[...], preferred_element_type=F32).astype(BF16)
    ab_ref[1] = jnp.dot(uf, sc_ref[...], preferred_element_type=F32).astype(BF16)

    cos = cos_ref[...]
    sin = sin_ref[...]

    def rope(t):
        t1, t2 = t[:, :LANES], t[:, LANES:]
        return jnp.concatenate([t1 * cos - t2 * sin, t2 * cos + t1 * sin], axis=-1)

    def put(ref, val, d):
        if d == 1:
            ref[0, 0] = val.astype(BF16)
            return
        for c in range(GROUP_WIDTH // LANES):
            scr_ref[c] = val[:, c * LANES:(c + 1) * LANES]
        for r in range(d):
            for c in range(GROUP_WIDTH // LANES):
                ref[0, r, :, c * LANES:(c + 1) * LANES] = scr_ref[c, pl.ds(r, tm // d, stride=d), :].astype(BF16)

    outs = ((q0_ref, k0_ref, v0_ref), (q1_ref, k1_ref, v1_ref), (q2_ref, k2_ref, v2_ref))
    for g, d in enumerate(DILATIONS):
        qr, kr, vr = outs[g]
        put(qr, rope(proj(COL_Q + g * GROUP_WIDTH, GROUP_WIDTH)) * (HEAD_DIM ** -0.5 * LOG2E), d)
        put(kr, rope(proj(COL_K + g * GROUP_WIDTH, GROUP_WIDTH)), d)
        put(vr, proj(COL_V + g * GROUP_WIDTH, GROUP_WIDTH), d)

    gf_ref[0] = jax.nn.sigmoid(proj(COL_GF, D_MODEL)).astype(BF16)
    ga_ref[0] = jax.nn.sigmoid(proj(COL_GA, D_MODEL)).astype(BF16)


def _in_proj(x, gain, w, cc, sc, cos_t, sin_t, tm):
    b, s, _ = x.shape
    grid = (b, s // tm)
    qkv_shapes, qkv_specs = [], []
    for d in DILATIONS:
        for _ in range(3):
            qkv_shapes.append(jax.ShapeDtypeStruct((b, d, s // d, GROUP_WIDTH), BF16))
            qkv_specs.append(pl.BlockSpec((1, d, tm // d, GROUP_WIDTH), lambda bi, si: (bi, 0, si, 0)))
    out_shape = ([jax.ShapeDtypeStruct((2, s, b * FOURIER_WIDTH), BF16)] + qkv_shapes +
                 [jax.ShapeDtypeStruct((b, s, D_MODEL), BF16)] * 2)
    out_specs = ([pl.BlockSpec((2, tm, FOURIER_WIDTH), lambda bi, si: (0, si, bi))] + qkv_specs +
                 [pl.BlockSpec((1, tm, D_MODEL), lambda bi, si: (bi, si, 0))] * 2)
    in_specs = [
        pl.BlockSpec((1, tm, D_MODEL), lambda bi, si: (bi, si, 0)),
        pl.BlockSpec((1, D_MODEL), lambda bi, si: (0, 0)),
        pl.BlockSpec((D_MODEL, IN_COLS), lambda bi, si: (0, 0)),
        pl.BlockSpec((FOURIER_WIDTH, FOURIER_WIDTH), lambda bi, si: (0, 0)),
        pl.BlockSpec((FOURIER_WIDTH, FOURIER_WIDTH), lambda bi, si: (0, 0)),
        pl.BlockSpec((tm, LANES), lambda bi, si: (si, 0)),
        pl.BlockSpec((tm, LANES), lambda bi, si: (si, 0)),
    ]
    return pl.pallas_call(
        _in_proj_kernel, grid=grid, in_specs=in_specs, out_specs=out_specs, out_shape=out_shape,
        scratch_shapes=[pltpu.VMEM((GROUP_WIDTH // LANES, tm, LANES), F32)],
        compiler_params=_params(("parallel", "parallel")), name="in_proj",
    )(x, gain, w, cc, sc, cos_t, sin_t)


def _matmul_kernel(a_ref, b_ref, o_ref, acc_ref):
    @pl.when(pl.program_id(2) == 0)
    def _():
        acc_ref[...] = jnp.zeros_like(acc_ref)

    acc_ref[...] += jnp.dot(a_ref[...], b_ref[...], preferred_element_type=F32)

    @pl.when(pl.program_id(2) == pl.num_programs(2) - 1)
    def _():
        o_ref[...] = acc_ref[...].astype(o_ref.dtype)


def _matmul(a, b, bm, bn, bk, out_dtype):
    m, k = a.shape
    _, n = b.shape
    bm, bn, bk = min(bm, m), min(bn, n), min(bk, k)
    return pl.pallas_call(
        _matmul_kernel, grid=(m // bm, n // bn, k // bk),
        in_specs=[pl.BlockSpec((bm, bk), lambda i, j, kk: (i, kk)),
                  pl.BlockSpec((bk, bn), lambda i, j, kk: (kk, j))],
        out_specs=pl.BlockSpec((bm, bn), lambda i, j, kk: (i, j)),
        out_shape=jax.ShapeDtypeStruct((m, n), out_dtype),
        scratch_shapes=[pltpu.VMEM((bm, bn), F32)],
        compiler_params=_params(("parallel", "parallel", "arbitrary")), name="seq_dft",
    )(a, b)


ATTN_UNROLL = 4


def _attn_kernel(q_ref, k_ref, v_ref, o_ref, lse_ref, *, qb, tk, seq_len):
    lane = lax.broadcasted_iota(jnp.int32, (1, GROUP_WIDTH), 1)
    qk_head = (lane % LANES) // (HEAD_DIM // 2)
    v_head = lane // HEAD_DIM
    col_minus_row = (lax.broadcasted_iota(jnp.int32, (qb, tk), 1)
                     - lax.broadcasted_iota(jnp.int32, (qb, tk), 0))
    n_res = q_ref.shape[1]
    n_sub = seq_len // qb

    def query_block(res, sub):
        m0 = pl.multiple_of(sub * qb, qb)
        start = pl.multiple_of(jnp.clip(m0 - HALF_SPAN, 0, seq_len - tk), 16)
        q = q_ref[0, res, pl.ds(m0, qb), :]
        k = k_ref[0, res, pl.ds(start, tk), :]
        v = v_ref[0, res, pl.ds(start, tk), :]
        valid = jnp.abs(col_minus_row + (start - m0)) <= HALF_SPAN
        q_heads = jnp.concatenate(
            [jnp.where(qk_head == hd, q, jnp.zeros_like(q)) for hd in range(HEADS_PER_GROUP)], axis=0)
        s = lax.dot_general(q_heads, k, (((1,), (1,)), ((), ())), preferred_element_type=F32)
        s = jnp.where(valid[None], s.reshape(HEADS_PER_GROUP, qb, tk), NEG_INF)
        m = jnp.max(s, axis=-1, keepdims=True)
        p = jnp.exp2(s - m)
        l = jnp.sum(p, axis=-1, keepdims=True)
        pv = jnp.dot(p.astype(BF16).reshape(HEADS_PER_GROUP * qb, tk), v, preferred_element_type=F32)
        pv = pv.reshape(HEADS_PER_GROUP, qb, GROUP_WIDTH) / l
        lse = (m + jnp.log2(l)) * LN2
        o_acc = pv[0]
        lse_acc = jnp.broadcast_to(lse[0], (qb, GROUP_WIDTH))
        for hd in range(1, HEADS_PER_GROUP):
            o_acc = jnp.where(v_head == hd, pv[hd], o_acc)
            lse_acc = jnp.where(v_head == hd, lse[hd], lse_acc)
        o_ref[0, res, pl.ds(m0, qb), :] = o_acc
        lse_ref[0, res, pl.ds(m0, qb), :] = lse_acc

    def body(it, carry):
        for j in range(ATTN_UNROLL):
            if n_sub % ATTN_UNROLL == 0:
                per_res = n_sub // ATTN_UNROLL
                query_block(it // per_res, (it % per_res) * ATTN_UNROLL + j)
            else:
                query_block(it * (ATTN_UNROLL // n_sub) + j // n_sub, j % n_sub)
        return carry

    lax.fori_loop(0, n_res * n_sub // ATTN_UNROLL, body, 0)


def _attention(q, k, v):
    b, d, seq_len, _ = q.shape
    qb = min(128, seq_len)
    tk = min(qb + 2 * HALF_SPAN, seq_len)
    n_sub = seq_len // qb
    n_res = min(d, max(1, 8 // n_sub))
    assert (n_res * n_sub) % ATTN_UNROLL == 0 and (n_sub % ATTN_UNROLL == 0 or ATTN_UNROLL % n_sub == 0)
    kern = functools.partial(_attn_kernel, qb=qb, tk=tk, seq_len=seq_len)
    blk = pl.BlockSpec((1, n_res, seq_len, GROUP_WIDTH), lambda bi, ri: (bi, ri, 0, 0))
    shape = jax.ShapeDtypeStruct((b, d, seq_len, GROUP_WIDTH), F32)
    return pl.pallas_call(
        kern, grid=(b, d // n_res), in_specs=[blk, blk, blk],
        out_specs=[blk, blk], out_shape=[shape, shape],
        compiler_params=_params(("parallel", "parallel")), name=f"attn_d{d}",
    )(q, k, v)


def _mix_out_kernel(*refs, routed):
    (x_ref, yf_ref, o0_ref, l0_ref, o1_ref, l1_ref, o2_ref, l2_ref, gf_ref, ga_ref,
     wf_ref, wa_ref, wo_ref, g2_ref) = refs[:14]
    if routed:
        rw_ref, rb_ref, xn_ref, h2_ref, cw_ref, sel_ref, sel_t_ref, o_scr, l_scr = refs[14:]
    else:
        xn_ref, h2_ref, o_scr, l_scr = refs[14:]
    tm = x_ref.shape[1]

    def gathered(ref, scr, d):
        if d == 1:
            return ref[0, 0]
        n_half = GROUP_WIDTH // LANES
        for r in range(d):
            for c in range(n_half):
                scr[c, pl.ds(r, tm // d, stride=d), :] = ref[0, r, :, c * LANES:(c + 1) * LANES]
        return jnp.concatenate([scr[c] for c in range(n_half)], axis=-1)

    o_refs, l_refs = (o0_ref, o1_ref, o2_ref), (l0_ref, l1_ref, l2_ref)
    os_, ls_ = [], []
    for g, d in enumerate(DILATIONS):
        os_.append(gathered(o_refs[g], o_scr.at[g], d))
        ls_.append(gathered(l_refs[g], l_scr.at[g], d))
    mx = jnp.maximum(jnp.maximum(ls_[0], ls_[1]), ls_[2])
    es = [jnp.exp(l - mx) for l in ls_]
    den = es[0] + es[1] + es[2]
    o_att = (es[0] * os_[0] + es[1] * os_[1] + es[2] * os_[2]) / den

    y_a = jnp.dot(o_att.astype(BF16), wa_ref[...], preferred_element_type=F32)
    y_f = jnp.dot(yf_ref[...], wf_ref[...], preferred_element_type=F32)
    z = gf_ref[0].astype(F32) * y_f + ga_ref[0].astype(F32) * y_a
    xn = x_ref[0] + jnp.dot(z.astype(BF16), wo_ref[...], preferred_element_type=F32)
    xn_ref[0] = xn
    h2 = _rms(xn, g2_ref[...])
    h2_hi = h2.astype(BF16)
    h2_ref[0] = h2_hi

    if routed:
        h2_lo = (h2 - h2_hi.astype(F32)).astype(BF16)
        logits = (jnp.dot(h2_hi, rw_ref[0], preferred_element_type=F32)
                  + jnp.dot(h2_hi, rw_ref[1], preferred_element_type=F32)
                  + jnp.dot(h2_lo, rw_ref[0], preferred_element_type=F32)) + rb_ref[...]
        lane = lax.broadcasted_iota(jnp.int32, logits.shape, 1)
        big = jnp.int32(LANES)
        m1 = jnp.max(logits, axis=-1, keepdims=True)
        i1 = jnp.min(jnp.where(logits == m1, lane, big), axis=-1, keepdims=True)
        rest = jnp.where(lane == i1, NEG_INF * 2, logits)
        m2 = jnp.max(rest, axis=-1, keepdims=True)
        i2 = jnp.min(jnp.where(rest == m2, lane, big), axis=-1, keepdims=True)
        e2 = jnp.exp(m2 - m1)
        w1 = 1.0 / (1.0 + e2)
        w2 = e2 / (1.0 + e2)
        cw_ref[...] = jnp.where(lane == i1, w1, jnp.where(lane == i2, w2, 0.0))
        sel = jnp.where((lane == i1) | (lane == i2), 1.0, 0.0)
        sel_ref[...] = sel
        sel_t_ref[...] = sel.T[:N_EXPERTS, :]


def _mix_out(x, yf, attn, gf, ga, wf, wa, wo, g2, router, tm):
    b, s, _ = x.shape
    routed = router is not None
    row = pl.BlockSpec((1, tm, D_MODEL), lambda bi, si: (bi, si, 0))
    full = lambda shape: pl.BlockSpec(shape, lambda bi, si: (0,) * len(shape))
    in_specs = [row, pl.BlockSpec((tm, FOURIER_WIDTH), lambda bi, si: (si, bi))]
    args = [x, yf]
    for d, (o, l) in zip(DILATIONS, attn):
        spec = pl.BlockSpec((1, d, tm // d, GROUP_WIDTH), lambda bi, si: (bi, 0, si, 0))
        in_specs += [spec, spec]
        args += [o, l]
    in_specs += [row, row, full((FOURIER_WIDTH, D_MODEL)), full((GROUP_WIDTH, D_MODEL)),
                 full((D_MODEL, D_MODEL)), full((1, D_MODEL))]
    args += [gf, ga, wf, wa, wo, g2]
    out_shape = [jax.ShapeDtypeStruct((b, s, D_MODEL), F32), jax.ShapeDtypeStruct((b, s, D_MODEL), BF16)]
    out_specs = [row, row]
    if routed:
        in_specs += [full((2, D_MODEL, LANES)), full((1, LANES))]
        args += list(router)
        tok = pl.BlockSpec((tm, LANES), lambda bi, si: (bi * (s // tm) + si, 0))
        out_shape += [jax.ShapeDtypeStruct((b * s, LANES), F32)] * 2 + [jax.ShapeDtypeStruct((N_EXPERTS, b * s), F32)]
        out_specs += [tok, tok, pl.BlockSpec((N_EXPERTS, tm), lambda bi, si: (0, bi * (s // tm) + si))]
    return pl.pallas_call(
        functools.partial(_mix_out_kernel, routed=routed), grid=(b, s // tm),
        in_specs=in_specs, out_specs=out_specs, out_shape=out_shape,
        scratch_shapes=[pltpu.VMEM((len(DILATIONS), GROUP_WIDTH // LANES, tm, LANES), F32)] * 2,
        compiler_params=_params(("parallel", "parallel")), name="mix_out",
    )(*args)


def _swiglu(h, w1, w3, w2):
    a = jnp.dot(h, w1, preferred_element_type=F32)
    g = jnp.dot(h, w3, preferred_element_type=F32)
    act = (a * jax.nn.sigmoid(a) * g).astype(BF16)
    return jnp.dot(act, w2, preferred_element_type=F32)


def _ffn_kernel(h_ref, x_ref, w1_ref, w3_ref, w2_ref, gfin_ref, o_ref, *, final_norm):
    f = pl.program_id(1)

    @pl.when(f == 0)
    def _():
        o_ref[...] = x_ref[...]

    o_ref[...] += _swiglu(h_ref[...], w1_ref[...], w3_ref[...], w2_ref[...])

    if final_norm:
        @pl.when(f == pl.num_programs(1) - 1)
        def _():
            o_ref[...] = _rms(o_ref[...], gfin_ref[...])


def _ffn(h, x, w1, w3, w2, gfin, final_norm, tm, tf):
    n = h.shape[0]
    row = pl.BlockSpec((tm, D_MODEL), lambda i, f: (i, 0))
    in_specs = [row, row,
                pl.BlockSpec((D_MODEL, tf), lambda i, f: (0, f)),
                pl.BlockSpec((D_MODEL, tf), lambda i, f: (0, f)),
                pl.BlockSpec((tf, D_MODEL), lambda i, f: (f, 0)),
                pl.BlockSpec((1, D_MODEL), lambda i, f: (0, 0))]
    return pl.pallas_call(
        functools.partial(_ffn_kernel, final_norm=final_norm),
        grid=(n // tm, FFN_DIM // tf), in_specs=in_specs, out_specs=row,
        out_shape=jax.ShapeDtypeStruct((n, D_MODEL), F32),
        compiler_params=_params(("parallel", "arbitrary")), name="ffn",
    )(h, x, w1, w3, w2, gfin)


MOE_SUB = 256
MOE_WIN = 128
MOE_ROWS = 128
ROW_ALIGN = 16


def _moe_kernel(h_ref, sel_t_ref, sel_ref, cw_ref, w1_ref, w3_ref, w2_ref, triu_ref, tril_ref, o_ref,
                pos_t_ref, pos_ref, hs_ref, y_ref, pre_ref):
    tb = h_ref.shape[0]
    n_sub = tb // MOE_SUB
    blk, e, f = pl.program_id(0), pl.program_id(1), pl.program_id(2)
    n_f = pl.num_programs(2)

    @pl.when((blk == 0) & (e == 0) & (f == 0))
    def _():
        hs_ref[...] = jnp.zeros_like(hs_ref)
        y_ref[...] = jnp.zeros_like(y_ref)

    @pl.when((e == 0) & (f == 0))
    def _():
        o_ref[...] = jnp.zeros_like(o_ref)
        off_t = jnp.zeros((N_EXPERTS, 1), F32)
        off = jnp.zeros((1, LANES), F32)
        for s in range(n_sub):
            sl = slice(s * MOE_SUB, (s + 1) * MOE_SUB)
            sel_t = sel_t_ref[:, sl]
            within_t = jnp.dot(sel_t.astype(BF16), triu_ref[...], preferred_element_type=F32)
            pos_t_ref[:, sl] = jnp.where(sel_t > 0.0, within_t + off_t, -1.0)
            off_t = off_t + jnp.sum(sel_t, axis=1, keepdims=True)
            sel = sel_ref[sl, :]
            within = jnp.dot(tril_ref[...], sel.astype(BF16), preferred_element_type=F32)
            pos_ref[sl, :] = jnp.where(sel > 0.0, within + off, -1.0)
            off = off + jnp.sum(sel, axis=0, keepdims=True)

    def window(s):
        p0, p1 = pre_ref[s], pre_ref[s + 1]
        base = (p0 // ROW_ALIGN) * ROW_ALIGN
        return p0, base, (p1 - base + MOE_WIN - 1) // MOE_WIN

    @pl.when(f == 0)
    def _():
        run = jnp.int32(0)
        pre_ref[0] = run
        for s in range(n_sub):
            cnt = jnp.sum(sel_t_ref[pl.ds(e, 1), s * MOE_SUB:(s + 1) * MOE_SUB])
            run = run + cnt.astype(jnp.int32)
            pre_ref[s + 1] = run
        row = lax.broadcasted_iota(jnp.int32, (MOE_WIN, MOE_SUB), 0)
        row1 = lax.broadcasted_iota(jnp.int32, (MOE_WIN, 1), 0)

        def gather(s, j):
            p0, p1 = pre_ref[s], pre_ref[s + 1]
            _, base, _ = window(s)
            sl = slice(s * MOE_SUB, (s + 1) * MOE_SUB)
            r0 = pl.multiple_of(base + j * MOE_WIN, ROW_ALIGN)
            pos_i = pos_t_ref[pl.ds(e, 1), sl].astype(jnp.int32)
            one_hot = jnp.where(row == pos_i - r0, 1.0, 0.0).astype(BF16)
            rows = jnp.dot(one_hot, h_ref[sl, :], preferred_element_type=F32).astype(BF16)
            mine = (row1 >= p0 - r0) & (row1 < p1 - r0)
            hs_ref[pl.ds(r0, MOE_WIN), :] = jnp.where(mine, rows, hs_ref[pl.ds(r0, MOE_WIN), :])

        for s in range(n_sub):
            gather(s, 0)
        for s in range(n_sub):
            lax.fori_loop(1, window(s)[2], lambda j, c, s=s: (gather(s, j), c)[1], 0)

    n_chunks = (pre_ref[n_sub] + MOE_ROWS - 1) // MOE_ROWS

    def expert_rows(r0, n_rows):
        y = _swiglu(hs_ref[pl.ds(r0, n_rows), :], w1_ref[0], w3_ref[0], w2_ref[0])

        @pl.when(f == 0)
        def _():
            y_ref[pl.ds(r0, n_rows), :] = y

        @pl.when(f > 0)
        def _():
            y_ref[pl.ds(r0, n_rows), :] += y

    def chunk_pair(c, carry):
        expert_rows(pl.multiple_of(c * (2 * MOE_ROWS), 2 * MOE_ROWS), 2 * MOE_ROWS)
        return carry

    lax.fori_loop(0, n_chunks // 2, chunk_pair, 0)

    @pl.when(n_chunks % 2 == 1)
    def _():
        expert_rows(pl.multiple_of((n_chunks - 1) * MOE_ROWS, MOE_ROWS), MOE_ROWS)

    @pl.when(f == n_f - 1)
    def _():
        lane = lax.broadcasted_iota(jnp.int32, (MOE_SUB, LANES), 1)

        def scatter(s, j):
            _, base, _ = window(s)
            sl = slice(s * MOE_SUB, (s + 1) * MOE_SUB)
            r0 = pl.multiple_of(base + j * MOE_WIN, ROW_ALIGN)
            mine = lane == e
            pos_i = jnp.sum(jnp.where(mine, pos_ref[sl, :], 0.0), axis=1, keepdims=True).astype(jnp.int32)
            weight = jnp.sum(jnp.where(mine, cw_ref[sl, :], 0.0), axis=1, keepdims=True)
            one_hot = jnp.where(lane == pos_i - r0, 1.0, 0.0).astype(BF16)
            rows = y_ref[pl.ds(r0, MOE_WIN), :].astype(BF16)
            back = jnp.dot(one_hot, rows, preferred_element_type=F32)
            o_ref[sl, :] = (o_ref[sl, :].astype(F32) + back * weight).astype(BF16)

        for s in range(n_sub):
            scatter(s, 0)
        for s in range(n_sub):
            lax.fori_loop(1, window(s)[2], lambda j, c, s=s: (scatter(s, j), c)[1], 0)


def _moe(h, sel_t, sel, cw, w1, w3, w2, tb, tf):
    n = h.shape[0]
    tb = min(tb, n)
    n_exp = w1.shape[0]
    triu = jnp.triu(jnp.ones((MOE_SUB, MOE_SUB), F32), k=1).astype(BF16)
    pad = 2 * MOE_WIN
    once = pl.Buffered(1)
    tok = pl.BlockSpec((tb, LANES), lambda i, e, f: (i, 0), pipeline_mode=once)
    tri = pl.BlockSpec((MOE_SUB, MOE_SUB), lambda i, e, f: (0, 0), pipeline_mode=once)
    in_specs = [pl.BlockSpec((tb, D_MODEL), lambda i, e, f: (i, 0), pipeline_mode=once),
                pl.BlockSpec((N_EXPERTS, tb), lambda i, e, f: (0, i), pipeline_mode=once),
                tok, tok,
                pl.BlockSpec((1, D_MODEL, tf), lambda i, e, f: (e, 0, f)),
                pl.BlockSpec((1, D_MODEL, tf), lambda i, e, f: (e, 0, f)),
                pl.BlockSpec((1, tf, D_MODEL), lambda i, e, f: (e, f, 0)),
                tri, tri]
    return pl.pallas_call(
        _moe_kernel, grid=(n // tb, n_exp, FFN_DIM // tf), in_specs=in_specs,
        out_specs=pl.BlockSpec((tb, D_MODEL), lambda i, e, f: (i, 0), pipeline_mode=once),
        out_shape=jax.ShapeDtypeStruct((n, D_MODEL), BF16),
        scratch_shapes=[pltpu.VMEM((N_EXPERTS, tb), F32),
                        pltpu.VMEM((tb, LANES), F32),
                        pltpu.VMEM((tb + pad, D_MODEL), BF16),
                        pltpu.VMEM((tb + pad, D_MODEL), F32),
                        pltpu.SMEM((tb // MOE_SUB + 1,), jnp.int32)],
        compiler_params=_params(("arbitrary", "arbitrary", "arbitrary")), name="moe",
    )(h, sel_t, sel, cw, w1, w3, w2, triu, triu.T)


def _residual_kernel(x_ref, y_ref, g_ref, o_ref, *, final_norm):
    v = x_ref[...] + y_ref[...].astype(F32)
    o_ref[...] = _rms(v, g_ref[...]) if final_norm else v


def _residual(x, y, gfin, final_norm, tm):
    n = x.shape[0]
    row = pl.BlockSpec((tm, D_MODEL), lambda i: (i, 0))
    return pl.pallas_call(
        functools.partial(_residual_kernel, final_norm=final_norm), grid=(n // tm,),
        in_specs=[row, row, pl.BlockSpec((1, D_MODEL), lambda i: (0, 0))], out_specs=row,
        out_shape=jax.ShapeDtypeStruct((n, D_MODEL), F32),
        compiler_params=_params(("parallel",)), name="residual",
    )(x, y, gfin)


def _qk_column_order():
    half = HEAD_DIM // 2
    order = []
    for g in range(len(DILATIONS)):
        base = g * GROUP_WIDTH
        for part in range(2):
            for hd in range(HEADS_PER_GROUP):
                order.extend(range(base + hd * HEAD_DIM + part * half, base + hd * HEAD_DIM + (part + 1) * half))
    return np.asarray(order, np.int32)


def _dft_tables(n):
    j = jnp.arange(n, dtype=jnp.int32)
    phase = (j[:, None] * j[None, :]) % n
    ang = phase.astype(F32) * (2.0 * np.pi / n)
    return jnp.cos(ang), jnp.sin(ang)


def _tables(s):
    half = HEAD_DIM // 2
    inv_freq = ROPE_THETA ** (-jnp.arange(half, dtype=F32) * 2.0 / HEAD_DIM)
    ang = jnp.arange(s, dtype=F32)[:, None] * inv_freq[None, :]
    cos_t = jnp.tile(jnp.cos(ang), (1, LANES // half))
    sin_t = jnp.tile(jnp.sin(ang), (1, LANES // half))
    cg, sg = _dft_tables(FOURIER_GROUP_DIM)
    eye = jnp.eye(FOURIER_GROUPS, dtype=F32)
    scale_c = FOURIER_GROUP_DIM ** -0.5
    cc = (jnp.kron(eye, cg) * scale_c).astype(BF16)
    sc = (jnp.kron(eye, sg) * scale_c).astype(BF16)
    r = 64
    col = jnp.arange(s, dtype=jnp.int32)[None, :]
    ang_a = ((jnp.arange(s // r, dtype=jnp.int32)[:, None] * r * col) % s).astype(F32) * (2.0 * np.pi / s)
    ang_b = ((jnp.arange(r, dtype=jnp.int32)[:, None] * col) % s).astype(F32) * (2.0 * np.pi / s)
    ca, sa = jnp.cos(ang_a)[:, None, :], jnp.sin(ang_a)[:, None, :]
    cb, sb = jnp.cos(ang_b)[None, :, :], jnp.sin(ang_b)[None, :, :]
    scale_s = s ** -0.5
    cs = ((ca * cb - sa * sb) * scale_s).reshape(s, s)
    ss = ((sa * cb + ca * sb) * (-scale_s)).reshape(s, s)
    seq = jnp.concatenate([cs, ss], axis=1).astype(BF16)
    return cos_t, sin_t, cc, sc, seq


def kernel(x, ln_mix, w_in, w_fourier, w_attn, w_out, ln_ffn, dense_w1, dense_w3, dense_w2,
           router_w, router_b, moe_w1, moe_w3, moe_w2, ln_final):
    b, s, _ = x.shape
    depth = w_in.shape[0]
    n = b * s
    tm = 512
    cos_t, sin_t, cc, sc, seq_tab = _tables(s)
    order = _qk_column_order()
    gfin = ln_final.reshape(1, D_MODEL)

    for layer in range(depth):
        w = w_in[layer]
        wq = w[:, COL_Q:COL_K][:, order]
        wk = w[:, COL_K:COL_V][:, order]
        w_l = jnp.concatenate([w[:, :COL_Q], wq, wk, w[:, COL_V:]], axis=1).astype(BF16)

        outs = _in_proj(x, ln_mix[layer].reshape(1, D_MODEL), w_l, cc, sc, cos_t, sin_t, tm)
        ab, qkv, gf, ga = outs[0], outs[1:10], outs[10], outs[11]
        yf = _matmul(seq_tab, ab.reshape(2 * s, b * FOURIER_WIDTH), 1024, 1024, 2048, BF16)
        attn = [_attention(qkv[3 * g], qkv[3 * g + 1], qkv[3 * g + 2]) for g in range(len(DILATIONS))]

        last = layer == depth - 1
        i = layer // 2
        if layer % 2 == 0:
            router = None
        else:
            rw = jnp.zeros((D_MODEL, LANES), F32).at[:, :N_EXPERTS].set(router_w[i])
            rw_hi = rw.astype(BF16)
            rw_lo = (rw - rw_hi.astype(F32)).astype(BF16)
            rb = jnp.full((1, LANES), NEG_INF, F32).at[0, :N_EXPERTS].set(router_b[i])
            router = (jnp.stack([rw_hi, rw_lo]), rb)
        res = _mix_out(x, yf, attn, gf, ga, w_fourier[layer].astype(BF16), w_attn[layer].astype(BF16),
                       w_out[layer].astype(BF16), ln_ffn[layer].reshape(1, D_MODEL), router, tm)
        xn, h2 = res[0].reshape(n, D_MODEL), res[1].reshape(n, D_MODEL)
        if layer % 2 == 0:
            y = _ffn(h2, xn, dense_w1[i].astype(BF16), dense_w3[i].astype(BF16),
                     dense_w2[i].astype(BF16), gfin, last, 1024, 896)
        else:
            y = _moe(h2, res[4], res[3], res[2], moe_w1[i].astype(BF16), moe_w3[i].astype(BF16),
                     moe_w2[i].astype(BF16), 2048, 1792)
            y = _residual(xn, y, gfin, last, 1024)
        x = y.reshape(b, s, D_MODEL)
    return x
```

```python
import functools

import jax
import jax.numpy as jnp
import numpy as np
from jax import lax
from jax.experimental import pallas as pl
from jax.experimental.pallas import tpu as pltpu

D_MODEL = 1024
FOURIER_GROUPS = 4
FOURIER_GROUP_DIM = 128
FOURIER_WIDTH = FOURIER_GROUPS * FOURIER_GROUP_DIM
DILATIONS = (1, 4, 16)
HALF_SPAN = 64
HEADS_PER_GROUP = 4
HEAD_DIM = 64
GROUP_WIDTH = HEADS_PER_GROUP * HEAD_DIM
ATTN_WIDTH = len(DILATIONS) * GROUP_WIDTH
ROPE_THETA = 10000.0
FFN_DIM = 3584
N_EXPERTS = 8
RMS_EPS = 1e-6
NEG_INF = -1e30
LOG2E = float(np.log2(np.e))
LN2 = float(np.log(2.0))
LANES = 128

COL_UF = 0
COL_Q = FOURIER_WIDTH
COL_K = COL_Q + ATTN_WIDTH
COL_V = COL_K + ATTN_WIDTH
COL_GF = COL_V + ATTN_WIDTH
COL_GA = COL_GF + D_MODEL
IN_COLS = COL_GA + D_MODEL

VMEM_LIMIT = 56 * 1024 * 1024

BF16 = jnp.bfloat16
F32 = jnp.float32


def _params(semantics):
    return pltpu.CompilerParams(dimension_semantics=semantics, vmem_limit_bytes=VMEM_LIMIT)


def _rms(x, gain):
    ms = jnp.mean(x * x, axis=-1, keepdims=True)
    return x * lax.rsqrt(ms + RMS_EPS) * gain


def _in_proj_kernel(x_ref, g_ref, w_ref, cc_ref, sc_ref, cos_ref, sin_ref,
                    ab_ref, q0_ref, k0_ref, v0_ref, q1_ref, k1_ref, v1_ref, q2_ref, k2_ref, v2_ref,
                    gf_ref, ga_ref, scr_ref):
    tm = x_ref.shape[1]
    h = _rms(x_ref[0], g_ref[...]).astype(BF16)

    def proj(col, width):
        return jnp.dot(h, w_ref[:, col:col + width], preferred_element_type=F32)

    uf = proj(COL_UF, FOURIER_WIDTH).astype(BF16)
    ab_ref[0] = jnp.dot(uf, cc_ref[...], preferred_element_type=F32).astype(BF16)
    ab_ref[1] = jnp.dot(uf, sc_ref[...], preferred_element_type=F32).astype(BF16)

    cos = cos_ref[...]
    sin = sin_ref[...]

    def rope(t):
        t1, t2 = t[:, :LANES], t[:, LANES:]
        return jnp.concatenate([t1 * cos - t2 * sin, t2 * cos + t1 * sin], axis=-1)

    def put(ref, val, d):
        if d == 1:
            ref[0, 0] = val.astype(BF16)
            return
        for c in range(GROUP_WIDTH // LANES):
            scr_ref[c] = val[:, c * LANES:(c + 1) * LANES]
        for r in range(d):
            for c in range(GROUP_WIDTH // LANES):
                ref[0, r, :, c * LANES:(c + 1) * LANES] = scr_ref[c, pl.ds(r, tm // d, stride=d), :].astype(BF16)

    outs = ((q0_ref, k0_ref, v0_ref), (q1_ref, k1_ref, v1_ref), (q2_ref, k2_ref, v2_ref))
    for g, d in enumerate(DILATIONS):
        qr, kr, vr = outs[g]
        put(qr, rope(proj(COL_Q + g * GROUP_WIDTH, GROUP_WIDTH)) * (HEAD_DIM ** -0.5 * LOG2E), d)
        put(kr, rope(proj(COL_K + g * GROUP_WIDTH, GROUP_WIDTH)), d)
        put(vr, proj(COL_V + g * GROUP_WIDTH, GROUP_WIDTH), d)

    gf_ref[0] = jax.nn.sigmoid(proj(COL_GF, D_MODEL)).astype(BF16)
    ga_ref[0] = jax.nn.sigmoid(proj(COL_GA, D_MODEL)).astype(BF16)


def _in_proj(x, gain, w, cc, sc, cos_t, sin_t, tm):
    b, s, _ = x.shape
    grid = (b, s // tm)
    qkv_shapes, qkv_specs = [], []
    for d in DILATIONS:
        for _ in range(3):
            qkv_shapes.append(jax.ShapeDtypeStruct((b, d, s // d, GROUP_WIDTH), BF16))
            qkv_specs.append(pl.BlockSpec((1, d, tm // d, GROUP_WIDTH), lambda bi, si: (bi, 0, si, 0)))
    out_shape = ([jax.ShapeDtypeStruct((2, s, b * FOURIER_WIDTH), BF16)] + qkv_shapes +
                 [jax.ShapeDtypeStruct((b, s, D_MODEL), BF16)] * 2)
    out_specs = ([pl.BlockSpec((2, tm, FOURIER_WIDTH), lambda bi, si: (0, si, bi))] + qkv_specs +
                 [pl.BlockSpec((1, tm, D_MODEL), lambda bi, si: (bi, si, 0))] * 2)
    in_specs = [
        pl.BlockSpec((1, tm, D_MODEL), lambda bi, si: (bi, si, 0)),
        pl.BlockSpec((1, D_MODEL), lambda bi, si: (0, 0)),
        pl.BlockSpec((D_MODEL, IN_COLS), lambda bi, si: (0, 0)),
        pl.BlockSpec((FOURIER_WIDTH, FOURIER_WIDTH), lambda bi, si: (0, 0)),
        pl.BlockSpec((FOURIER_WIDTH, FOURIER_WIDTH), lambda bi, si: (0, 0)),
        pl.BlockSpec((tm, LANES), lambda bi, si: (si, 0)),
        pl.BlockSpec((tm, LANES), lambda bi, si: (si, 0)),
    ]
    return pl.pallas_call(
        _in_proj_kernel, grid=grid, in_specs=in_specs, out_specs=out_specs, out_shape=out_shape,
        scratch_shapes=[pltpu.VMEM((GROUP_WIDTH // LANES, tm, LANES), F32)],
        compiler_params=_params(("parallel", "parallel")), name="in_proj",
    )(x, gain, w, cc, sc, cos_t, sin_t)


SEQ_RADIX = 4


def _seq_dft_kernel(ab_ref, tab_ref, tw_ref, o_ref, u_ref):
    q_len = tab_ref.shape[0]
    for c in range(o_ref.shape[2] // LANES):
        cols = slice(c * LANES, (c + 1) * LANES)
        re = [ab_ref[0, j, :, cols].astype(F32) for j in range(SEQ_RADIX)]
        im = [-ab_ref[1, j, :, cols].astype(F32) for j in range(SEQ_RADIX)]
        p_re, p_im, q_re, q_im = re[0] + re[2], im[0] + im[2], re[0] - re[2], im[0] - im[2]
        s_re, s_im, d_re, d_im = re[1] + re[3], im[1] + im[3], re[1] - re[3], im[1] - im[3]
        terms = ((p_re + s_re, p_im + s_im), (q_re + d_im, q_im - d_re),
                 (p_re - s_re, p_im - s_im), (q_re - d_im, q_im + d_re))
        for r, (t_re, t_im) in enumerate(terms):
            if r > 0:
                cos, sin = tw_ref[r - 1, 0], tw_ref[r - 1, 1]
                t_re, t_im = t_re * cos + t_im * sin, t_im * cos - t_re * sin
            u_ref[r, :q_len, cols] = t_re.astype(BF16)
            u_ref[r, q_len:, cols] = t_im.astype(BF16)
    for r in range(SEQ_RADIX):
        o_ref[r] = jnp.dot(tab_ref[...], u_ref[r], preferred_element_type=F32).astype(o_ref.dtype)


def _seq_dft(ab, tab, tw, bn):
    _, s, n = ab.shape
    q_len = s // SEQ_RADIX
    bn = min(bn, n)
    return pl.pallas_call(
        _seq_dft_kernel, grid=(n // bn,),
        in_specs=[pl.BlockSpec((2, SEQ_RADIX, q_len, bn), lambda j: (0, 0, 0, j)),
                  pl.BlockSpec((q_len, 2 * q_len), lambda j: (0, 0), pipeline_mode=pl.Buffered(1)),
                  pl.BlockSpec((SEQ_RADIX - 1, 2, q_len, LANES), lambda j: (0, 0, 0, 0),
                               pipeline_mode=pl.Buffered(1))],
        out_specs=pl.BlockSpec((SEQ_RADIX, q_len, bn), lambda j: (0, 0, j)),
        out_shape=jax.ShapeDtypeStruct((SEQ_RADIX, q_len, n), BF16),
        scratch_shapes=[pltpu.VMEM((SEQ_RADIX, 2 * q_len, bn), BF16)],
        compiler_params=_params(("parallel",)), name="seq_dft",
    )(ab.reshape(2, SEQ_RADIX, q_len, n), tab, tw)


ATTN_UNROLL = 4


def _attn_kernel(q_ref, k_ref, v_ref, o_ref, lse_ref, *, qb, tk, seq_len):
    lane = lax.broadcasted_iota(jnp.int32, (1, GROUP_WIDTH), 1)
    qk_head = (lane % LANES) // (HEAD_DIM // 2)
    v_head = lane // HEAD_DIM
    col_minus_row = (lax.broadcasted_iota(jnp.int32, (qb, tk), 1)
                     - lax.broadcasted_iota(jnp.int32, (qb, tk), 0))
    n_res = q_ref.shape[1]
    n_sub = seq_len // qb

    def query_block(res, sub):
        m0 = pl.multiple_of(sub * qb, qb)
        start = pl.multiple_of(jnp.clip(m0 - HALF_SPAN, 0, seq_len - tk), 16)
        q = q_ref[0, res, pl.ds(m0, qb), :]
        k = k_ref[0, res, pl.ds(start, tk), :]
        v = v_ref[0, res, pl.ds(start, tk), :]
        valid = jnp.abs(col_minus_row + (start - m0)) <= HALF_SPAN
        q_heads = jnp.concatenate(
            [jnp.where(qk_head == hd, q, jnp.zeros_like(q)) for hd in range(HEADS_PER_GROUP)], axis=0)
        s = lax.dot_general(q_heads, k, (((1,), (1,)), ((), ())), preferred_element_type=F32)
        s = jnp.where(valid[None], s.reshape(HEADS_PER_GROUP, qb, tk), NEG_INF)
        m = jnp.max(s, axis=-1, keepdims=True)
        p = jnp.exp2(s - m)
        l = jnp.sum(p, axis=-1, keepdims=True)
        pv = jnp.dot(p.astype(BF16).reshape(HEADS_PER_GROUP * qb, tk), v, preferred_element_type=F32)
        pv = pv.reshape(HEADS_PER_GROUP, qb, GROUP_WIDTH) / l
        lse = (m + jnp.log2(l)) * LN2
        o_acc = pv[0]
        lse_acc = jnp.broadcast_to(lse[0], (qb, GROUP_WIDTH))
        for hd in range(1, HEADS_PER_GROUP):
            o_acc = jnp.where(v_head == hd, pv[hd], o_acc)
            lse_acc = jnp.where(v_head == hd, lse[hd], lse_acc)
        o_ref[0, res, pl.ds(m0, qb), :] = o_acc
        lse_ref[0, res, pl.ds(m0, qb), :] = lse_acc

    def body(it, carry):
        for j in range(ATTN_UNROLL):
            if n_sub % ATTN_UNROLL == 0:
                per_res = n_sub // ATTN_UNROLL
                query_block(it // per_res, (it % per_res) * ATTN_UNROLL + j)
            else:
                query_block(it * (ATTN_UNROLL // n_sub) + j // n_sub, j % n_sub)
        return carry

    lax.fori_loop(0, n_res * n_sub // ATTN_UNROLL, body, 0)


def _attention(q, k, v):
    b, d, seq_len, _ = q.shape
    qb = min(128, seq_len)
    tk = min(qb + 2 * HALF_SPAN, seq_len)
    n_sub = seq_len // qb
    n_res = min(d, max(1, 8 // n_sub))
    assert (n_res * n_sub) % ATTN_UNROLL == 0 and (n_sub % ATTN_UNROLL == 0 or ATTN_UNROLL % n_sub == 0)
    kern = functools.partial(_attn_kernel, qb=qb, tk=tk, seq_len=seq_len)
    blk = pl.BlockSpec((1, n_res, seq_len, GROUP_WIDTH), lambda bi, ri: (bi, ri, 0, 0))
    shape = jax.ShapeDtypeStruct((b, d, seq_len, GROUP_WIDTH), F32)
    return pl.pallas_call(
        kern, grid=(b, d // n_res), in_specs=[blk, blk, blk],
        out_specs=[blk, blk], out_shape=[shape, shape],
        compiler_params=_params(("parallel", "parallel")), name=f"attn_d{d}",
    )(q, k, v)


def _mix_out_kernel(*refs, routed):
    (x_ref, yf_ref, o0_ref, l0_ref, o1_ref, l1_ref, o2_ref, l2_ref, gf_ref, ga_ref,
     wf_ref, wa_ref, wo_ref, g2_ref) = refs[:14]
    if routed:
        rw_ref, rb_ref, xn_ref, h2_ref, cw_ref, sel_ref, sel_t_ref, o_scr, l_scr, f_scr = refs[14:]
    else:
        xn_ref, h2_ref, o_scr, l_scr, f_scr = refs[14:]
    tm = x_ref.shape[1]

    def gathered(ref, scr, d):
        if d == 1:
            return ref[0]
        n_chunk = ref.shape[-1] // LANES
        for r in range(d):
            for c in range(n_chunk):
                scr[c, pl.ds(r, tm // d, stride=d), :] = ref[r, :, c * LANES:(c + 1) * LANES].astype(F32)
        return jnp.concatenate([scr[c] for c in range(n_chunk)], axis=-1)

    o_refs, l_refs = (o0_ref, o1_ref, o2_ref), (l0_ref, l1_ref, l2_ref)
    os_, ls_ = [], []
    for g, d in enumerate(DILATIONS):
        os_.append(gathered(o_refs[g].at[0], o_scr.at[g], d))
        ls_.append(gathered(l_refs[g].at[0], l_scr.at[g], d))
    mx = jnp.maximum(jnp.maximum(ls_[0], ls_[1]), ls_[2])
    es = [jnp.exp(l - mx) for l in ls_]
    den = es[0] + es[1] + es[2]
    o_att = (es[0] * os_[0] + es[1] * os_[1] + es[2] * os_[2]) / den

    y_a = jnp.dot(o_att.astype(BF16), wa_ref[...], preferred_element_type=F32)
    y_mix = gathered(yf_ref, f_scr, SEQ_RADIX).astype(BF16)
    y_f = jnp.dot(y_mix, wf_ref[...], preferred_element_type=F32)
    z = gf_ref[0].astype(F32) * y_f + ga_ref[0].astype(F32) * y_a
    xn = x_ref[0] + jnp.dot(z.astype(BF16), wo_ref[...], preferred_element_type=F32)
    xn_ref[0] = xn
    h2 = _rms(xn, g2_ref[...])
    h2_hi = h2.astype(BF16)
    h2_ref[0] = h2_hi

    if routed:
        h2_lo = (h2 - h2_hi.astype(F32)).astype(BF16)
        logits = (jnp.dot(h2_hi, rw_ref[0], preferred_element_type=F32)
                  + jnp.dot(h2_hi, rw_ref[1], preferred_element_type=F32)
                  + jnp.dot(h2_lo, rw_ref[0], preferred_element_type=F32)) + rb_ref[...]
        lane = lax.broadcasted_iota(jnp.int32, logits.shape, 1)
        big = jnp.int32(LANES)
        m1 = jnp.max(logits, axis=-1, keepdims=True)
        i1 = jnp.min(jnp.where(logits == m1, lane, big), axis=-1, keepdims=True)
        rest = jnp.where(lane == i1, NEG_INF * 2, logits)
        m2 = jnp.max(rest, axis=-1, keepdims=True)
        i2 = jnp.min(jnp.where(rest == m2, lane, big), axis=-1, keepdims=True)
        e2 = jnp.exp(m2 - m1)
        w1 = 1.0 / (1.0 + e2)
        w2 = e2 / (1.0 + e2)
        cw_ref[...] = jnp.where(lane == i1, w1, jnp.where(lane == i2, w2, 0.0))
        sel = jnp.where((lane == i1) | (lane == i2), 1.0, 0.0)
        sel_ref[...] = sel
        sel_t_ref[...] = sel.T[:N_EXPERTS, :]


def _mix_out(x, yf, attn, gf, ga, wf, wa, wo, g2, router, tm):
    b, s, _ = x.shape
    routed = router is not None
    row = pl.BlockSpec((1, tm, D_MODEL), lambda bi, si: (bi, si, 0))
    full = lambda shape: pl.BlockSpec(shape, lambda bi, si: (0,) * len(shape))
    in_specs = [row, pl.BlockSpec((SEQ_RADIX, tm // SEQ_RADIX, FOURIER_WIDTH), lambda bi, si: (0, si, bi))]
    args = [x, yf]
    for d, (o, l) in zip(DILATIONS, attn):
        spec = pl.BlockSpec((1, d, tm // d, GROUP_WIDTH), lambda bi, si: (bi, 0, si, 0))
        in_specs += [spec, spec]
        args += [o, l]
    in_specs += [row, row, full((FOURIER_WIDTH, D_MODEL)), full((GROUP_WIDTH, D_MODEL)),
                 full((D_MODEL, D_MODEL)), full((1, D_MODEL))]
    args += [gf, ga, wf, wa, wo, g2]
    out_shape = [jax.ShapeDtypeStruct((b, s, D_MODEL), F32), jax.ShapeDtypeStruct((b, s, D_MODEL), BF16)]
    out_specs = [row, row]
    if routed:
        in_specs += [full((2, D_MODEL, LANES)), full((1, LANES))]
        args += list(router)
        tok = pl.BlockSpec((tm, LANES), lambda bi, si: (bi * (s // tm) + si, 0))
        out_shape += [jax.ShapeDtypeStruct((b * s, LANES), F32)] * 2 + [jax.ShapeDtypeStruct((N_EXPERTS, b * s), F32)]
        out_specs += [tok, tok, pl.BlockSpec((N_EXPERTS, tm), lambda bi, si: (0, bi * (s // tm) + si))]
    return pl.pallas_call(
        functools.partial(_mix_out_kernel, routed=routed), grid=(b, s // tm),
        in_specs=in_specs, out_specs=out_specs, out_shape=out_shape,
        scratch_shapes=[pltpu.VMEM((len(DILATIONS), GROUP_WIDTH // LANES, tm, LANES), F32)] * 2
        + [pltpu.VMEM((FOURIER_WIDTH // LANES, tm, LANES), F32)],
        compiler_params=_params(("parallel", "parallel")), name="mix_out",
    )(*args)


def _swiglu(h, w1, w3, w2):
    a = jnp.dot(h, w1, preferred_element_type=F32)
    g = jnp.dot(h, w3, preferred_element_type=F32)
    act = (a * jax.nn.sigmoid(a) * g).astype(BF16)
    return jnp.dot(act, w2, preferred_element_type=F32)


def _ffn_kernel(h_ref, x_ref, w1_ref, w3_ref, w2_ref, gfin_ref, o_ref, *, final_norm):
    f = pl.program_id(1)

    @pl.when(f == 0)
    def _():
        o_ref[...] = x_ref[...]

    o_ref[...] += _swiglu(h_ref[...], w1_ref[...], w3_ref[...], w2_ref[...])

    if final_norm:
        @pl.when(f == pl.num_programs(1) - 1)
        def _():
            o_ref[...] = _rms(o_ref[...], gfin_ref[...])


def _ffn(h, x, w1, w3, w2, gfin, final_norm, tm, tf):
    n = h.shape[0]
    row = pl.BlockSpec((tm, D_MODEL), lambda i, f: (i, 0))
    in_specs = [row, row,
                pl.BlockSpec((D_MODEL, tf), lambda i, f: (0, f)),
                pl.BlockSpec((D_MODEL, tf), lambda i, f: (0, f)),
                pl.BlockSpec((tf, D_MODEL), lambda i, f: (f, 0)),
                pl.BlockSpec((1, D_MODEL), lambda i, f: (0, 0))]
    return pl.pallas_call(
        functools.partial(_ffn_kernel, final_norm=final_norm),
        grid=(n // tm, FFN_DIM // tf), in_specs=in_specs, out_specs=row,
        out_shape=jax.ShapeDtypeStruct((n, D_MODEL), F32),
        compiler_params=_params(("parallel", "arbitrary")), name="ffn",
    )(h, x, w1, w3, w2, gfin)


MOE_SUB = 256
MOE_WIN = 128
MOE_ROWS = 64
ROW_ALIGN = 16


def _moe_kernel(h_ref, sel_t_ref, sel_ref, cw_ref, w1_ref, w3_ref, w2_ref, triu_ref, tril_ref, o_ref,
                pos_t_ref, pos_ref, hs_ref, y_ref, pre_ref):
    tb = h_ref.shape[0]
    n_sub = tb // MOE_SUB
    blk, e, f = pl.program_id(0), pl.program_id(1), pl.program_id(2)
    n_f = pl.num_programs(2)

    @pl.when((blk == 0) & (e == 0) & (f == 0))
    def _():
        hs_ref[...] = jnp.zeros_like(hs_ref)
        y_ref[...] = jnp.zeros_like(y_ref)

    @pl.when((e == 0) & (f == 0))
    def _():
        o_ref[...] = jnp.zeros_like(o_ref)
        off_t = jnp.zeros((N_EXPERTS, 1), F32)
        off = jnp.zeros((1, LANES), F32)
        for s in range(n_sub):
            sl = slice(s * MOE_SUB, (s + 1) * MOE_SUB)
            sel_t = sel_t_ref[:, sl]
            within_t = jnp.dot(sel_t.astype(BF16), triu_ref[...], preferred_element_type=F32)
            pos_t_ref[:, sl] = jnp.where(sel_t > 0.0, within_t + off_t, -1.0)
            off_t = off_t + jnp.sum(sel_t, axis=1, keepdims=True)
            sel = sel_ref[sl, :]
            within = jnp.dot(tril_ref[...], sel.astype(BF16), preferred_element_type=F32)
            pos_ref[sl, :] = jnp.where(sel > 0.0, within + off, -1.0)
            off = off + jnp.sum(sel, axis=0, keepdims=True)

    def window(s):
        p0, p1 = pre_ref[s], pre_ref[s + 1]
        base = (p0 // ROW_ALIGN) * ROW_ALIGN
        return p0, base, (p1 - base + MOE_WIN - 1) // MOE_WIN

    @pl.when(f == 0)
    def _():
        run = jnp.int32(0)
        pre_ref[0] = run
        for s in range(n_sub):
            cnt = jnp.sum(sel_t_ref[pl.ds(e, 1), s * MOE_SUB:(s + 1) * MOE_SUB])
            run = run + cnt.astype(jnp.int32)
            pre_ref[s + 1] = run
        row = lax.broadcasted_iota(jnp.int32, (MOE_WIN, MOE_SUB), 0)
        row1 = lax.broadcasted_iota(jnp.int32, (MOE_WIN, 1), 0)

        def gather(s, j):
            p0, p1 = pre_ref[s], pre_ref[s + 1]
            _, base, _ = window(s)
            sl = slice(s * MOE_SUB, (s + 1) * MOE_SUB)
            r0 = pl.multiple_of(base + j * MOE_WIN, ROW_ALIGN)
            pos_i = pos_t_ref[pl.ds(e, 1), sl].astype(jnp.int32)
            one_hot = jnp.where(row == pos_i - r0, 1.0, 0.0).astype(BF16)
            rows = jnp.dot(one_hot, h_ref[sl, :], preferred_element_type=F32).astype(BF16)
            mine = (row1 >= p0 - r0) & (row1 < p1 - r0)
            hs_ref[pl.ds(r0, MOE_WIN), :] = jnp.where(mine, rows, hs_ref[pl.ds(r0, MOE_WIN), :])

        for s in range(n_sub):
            gather(s, 0)
        for s in range(n_sub):
            lax.fori_loop(1, window(s)[2], lambda j, c, s=s: (gather(s, j), c)[1], 0)

    n_units = (pre_ref[n_sub] + MOE_ROWS - 1) // MOE_ROWS
    n_full = n_units // 4
    rem = n_units % 4

    def expert_rows(r0, n_rows):
        y = _swiglu(hs_ref[pl.ds(r0, n_rows), :], w1_ref[0], w3_ref[0], w2_ref[0])

        @pl.when(f == 0)
        def _():
            y_ref[pl.ds(r0, n_rows), :] = y

        @pl.when(f > 0)
        def _():
            y_ref[pl.ds(r0, n_rows), :] += y

    def full_chunk(c, carry):
        expert_rows(pl.multiple_of(c * (4 * MOE_ROWS), 4 * MOE_ROWS), 4 * MOE_ROWS)
        return carry

    lax.fori_loop(0, n_full, full_chunk, 0)

    @pl.when(rem >= 2)
    def _():
        expert_rows(pl.multiple_of(n_full * (4 * MOE_ROWS), 2 * MOE_ROWS), 2 * MOE_ROWS)

    @pl.when(rem % 2 == 1)
    def _():
        expert_rows(pl.multiple_of((n_units - 1) * MOE_ROWS, MOE_ROWS), MOE_ROWS)

    @pl.when(f == n_f - 1)
    def _():
        lane = lax.broadcasted_iota(jnp.int32, (MOE_SUB, LANES), 1)

        def scatter(s, j):
            _, base, _ = window(s)
            sl = slice(s * MOE_SUB, (s + 1) * MOE_SUB)
            r0 = pl.multiple_of(base + j * MOE_WIN, ROW_ALIGN)
            mine = lane == e
            pos_i = jnp.sum(jnp.where(mine, pos_ref[sl, :], 0.0), axis=1, keepdims=True).astype(jnp.int32)
            weight = jnp.sum(jnp.where(mine, cw_ref[sl, :], 0.0), axis=1, keepdims=True)
            one_hot = jnp.where(lane == pos_i - r0, 1.0, 0.0).astype(BF16)
            rows = y_ref[pl.ds(r0, MOE_WIN), :].astype(BF16)
            back = jnp.dot(one_hot, rows, preferred_element_type=F32)
            o_ref[sl, :] = (o_ref[sl, :].astype(F32) + back * weight).astype(BF16)

        for s in range(n_sub):
            scatter(s, 0)
        for s in range(n_sub):
            lax.fori_loop(1, window(s)[2], lambda j, c, s=s: (scatter(s, j), c)[1], 0)


def _moe(h, sel_t, sel, cw, w1, w3, w2, tb, tf):
    n = h.shape[0]
    tb = min(tb, n)
    n_exp = w1.shape[0]
    triu = jnp.triu(jnp.ones((MOE_SUB, MOE_SUB), F32), k=1).astype(BF16)
    pad = 2 * MOE_WIN
    once = pl.Buffered(1)
    tok = pl.BlockSpec((tb, LANES), lambda i, e, f: (i, 0), pipeline_mode=once)
    tri = pl.BlockSpec((MOE_SUB, MOE_SUB), lambda i, e, f: (0, 0), pipeline_mode=once)
    in_specs = [pl.BlockSpec((tb, D_MODEL), lambda i, e, f: (i, 0), pipeline_mode=once),
                pl.BlockSpec((N_EXPERTS, tb), lambda i, e, f: (0, i), pipeline_mode=once),
                tok, tok,
                pl.BlockSpec((1, D_MODEL, tf), lambda i, e, f: (e, 0, f)),
                pl.BlockSpec((1, D_MODEL, tf), lambda i, e, f: (e, 0, f)),
                pl.BlockSpec((1, tf, D_MODEL), lambda i, e, f: (e, f, 0)),
                tri, tri]
    return pl.pallas_call(
        _moe_kernel, grid=(n // tb, n_exp, FFN_DIM // tf), in_specs=in_specs,
        out_specs=pl.BlockSpec((tb, D_MODEL), lambda i, e, f: (i, 0), pipeline_mode=once),
        out_shape=jax.ShapeDtypeStruct((n, D_MODEL), BF16),
        scratch_shapes=[pltpu.VMEM((N_EXPERTS, tb), F32),
                        pltpu.VMEM((tb, LANES), F32),
                        pltpu.VMEM((tb + pad, D_MODEL), BF16),
                        pltpu.VMEM((tb + pad, D_MODEL), F32),
                        pltpu.SMEM((tb // MOE_SUB + 1,), jnp.int32)],
        compiler_params=_params(("arbitrary", "arbitrary", "arbitrary")), name="moe",
    )(h, sel_t, sel, cw, w1, w3, w2, triu, triu.T)


def _residual_kernel(x_ref, y_ref, g_ref, o_ref, *, final_norm):
    v = x_ref[...] + y_ref[...].astype(F32)
    o_ref[...] = _rms(v, g_ref[...]) if final_norm else v


def _residual(x, y, gfin, final_norm, tm):
    n = x.shape[0]
    row = pl.BlockSpec((tm, D_MODEL), lambda i: (i, 0))
    return pl.pallas_call(
        functools.partial(_residual_kernel, final_norm=final_norm), grid=(n // tm,),
        in_specs=[row, row, pl.BlockSpec((1, D_MODEL), lambda i: (0, 0))], out_specs=row,
        out_shape=jax.ShapeDtypeStruct((n, D_MODEL), F32),
        compiler_params=_params(("parallel",)), name="residual",
    )(x, y, gfin)


def _qk_column_order():
    half = HEAD_DIM // 2
    order = []
    for g in range(len(DILATIONS)):
        base = g * GROUP_WIDTH
        for part in range(2):
            for hd in range(HEADS_PER_GROUP):
                order.extend(range(base + hd * HEAD_DIM + part * half, base + hd * HEAD_DIM + (part + 1) * half))
    return np.asarray(order, np.int32)


def _dft_tables(n):
    j = jnp.arange(n, dtype=jnp.int32)
    phase = (j[:, None] * j[None, :]) % n
    ang = phase.astype(F32) * (2.0 * np.pi / n)
    return jnp.cos(ang), jnp.sin(ang)


def _tables(s):
    half = HEAD_DIM // 2
    inv_freq = ROPE_THETA ** (-jnp.arange(half, dtype=F32) * 2.0 / HEAD_DIM)
    ang = jnp.arange(s, dtype=F32)[:, None] * inv_freq[None, :]
    cos_t = jnp.tile(jnp.cos(ang), (1, LANES // half))
    sin_t = jnp.tile(jnp.sin(ang), (1, LANES // half))
    cg, sg = _dft_tables(FOURIER_GROUP_DIM)
    eye = jnp.eye(FOURIER_GROUPS, dtype=F32)
    scale_c = FOURIER_GROUP_DIM ** -0.5
    cc = (jnp.kron(eye, cg) * scale_c).astype(BF16)
    sc = (jnp.kron(eye, sg) * scale_c).astype(BF16)
    q_len = s // SEQ_RADIX
    cq, sq = _dft_tables(q_len)
    seq_tab = (jnp.concatenate([cq, sq], axis=1) * s ** -0.5).astype(BF16)
    pos = jnp.arange(q_len, dtype=jnp.int32)[None, :]
    ang = ((jnp.arange(1, SEQ_RADIX, dtype=jnp.int32)[:, None] * pos) % s).astype(F32) * (2.0 * np.pi / s)
    seq_tw = jnp.broadcast_to(jnp.stack([jnp.cos(ang), jnp.sin(ang)], axis=1)[..., None],
                              (SEQ_RADIX - 1, 2, q_len, LANES))
    return cos_t, sin_t, cc, sc, seq_tab, seq_tw


def kernel(x, ln_mix, w_in, w_fourier, w_attn, w_out, ln_ffn, dense_w1, dense_w3, dense_w2,
           router_w, router_b, moe_w1, moe_w3, moe_w2, ln_final):
    b, s, _ = x.shape
    depth = w_in.shape[0]
    n = b * s
    tm = 512
    cos_t, sin_t, cc, sc, seq_tab, seq_tw = _tables(s)
    order = _qk_column_order()
    gfin = ln_final.reshape(1, D_MODEL)

    for layer in range(depth):
        w = w_in[layer]
        wq = w[:, COL_Q:COL_K][:, order]
        wk = w[:, COL_K:COL_V][:, order]
        w_l = jnp.concatenate([w[:, :COL_Q], wq, wk, w[:, COL_V:]], axis=1).astype(BF16)

        outs = _in_proj(x, ln_mix[layer].reshape(1, D_MODEL), w_l, cc, sc, cos_t, sin_t, min(1024, s))
        ab, qkv, gf, ga = outs[0], outs[1:10], outs[10], outs[11]
        yf = _seq_dft(ab, seq_tab, seq_tw, 256)
        attn = [_attention(qkv[3 * g], qkv[3 * g + 1], qkv[3 * g + 2]) for g in range(len(DILATIONS))]

        last = layer == depth - 1
        i = layer // 2
        if layer % 2 == 0:
            router = None
        else:
            rw = jnp.zeros((D_MODEL, LANES), F32).at[:, :N_EXPERTS].set(router_w[i])
            rw_hi = rw.astype(BF16)
            rw_lo = (rw - rw_hi.astype(F32)).astype(BF16)
            rb = jnp.full((1, LANES), NEG_INF, F32).at[0, :N_EXPERTS].set(router_b[i])
            router = (jnp.stack([rw_hi, rw_lo]), rb)
        res = _mix_out(x, yf, attn, gf, ga, w_fourier[layer].astype(BF16), w_attn[layer].astype(BF16),
                       w_out[layer].astype(BF16), ln_ffn[layer].reshape(1, D_MODEL), router, tm)
        xn, h2 = res[0].reshape(n, D_MODEL), res[1].reshape(n, D_MODEL)
        if layer % 2 == 0:
            y = _ffn(h2, xn, dense_w1[i].astype(BF16), dense_w3[i].astype(BF16),
                     dense_w2[i].astype(BF16), gfin, last, 1024, 512)
        else:
            y = _moe(h2, res[4], res[3], res[2], moe_w1[i].astype(BF16), moe_w3[i].astype(BF16),
                     moe_w2[i].astype(BF16), 2048, 1792)
            y = _residual(xn, y, gfin, last, 1024)
        x = y.reshape(b, s, D_MODEL)
    return x
```

```python
import functools

import jax
import jax.numpy as jnp
import numpy as np
from jax import lax
from jax.experimental import pallas as pl
from jax.experimental.pallas import tpu as pltpu

D_MODEL = 1024
FOURIER_GROUPS = 4
FOURIER_GROUP_DIM = 128
FOURIER_WIDTH = FOURIER_GROUPS * FOURIER_GROUP_DIM
DILATIONS = (1, 4, 16)
HALF_SPAN = 64
HEADS_PER_GROUP = 4
HEAD_DIM = 64
GROUP_WIDTH = HEADS_PER_GROUP * HEAD_DIM
ATTN_WIDTH = len(DILATIONS) * GROUP_WIDTH
ROPE_THETA = 10000.0
FFN_DIM = 3584
N_EXPERTS = 8
RMS_EPS = 1e-6
NEG_INF = -1e30
LOG2E = float(np.log2(np.e))
LN2 = float(np.log(2.0))
LANES = 128

COL_UF = 0
COL_Q = FOURIER_WIDTH
COL_K = COL_Q + ATTN_WIDTH
COL_V = COL_K + ATTN_WIDTH
COL_GF = COL_V + ATTN_WIDTH
COL_GA = COL_GF + D_MODEL
IN_COLS = COL_GA + D_MODEL

VMEM_LIMIT = 56 * 1024 * 1024

BF16 = jnp.bfloat16
F32 = jnp.float32


def _params(semantics):
    return pltpu.CompilerParams(dimension_semantics=semantics, vmem_limit_bytes=VMEM_LIMIT)


def _rms(x, gain):
    ms = jnp.mean(x * x, axis=-1, keepdims=True)
    return x * lax.rsqrt(ms + RMS_EPS) * gain


def _in_proj_kernel(x_ref, g_ref, w_ref, cc_ref, sc_ref, cos_ref, sin_ref,
                    ab_ref, q0_ref, k0_ref, v0_ref, q1_ref, k1_ref, v1_ref, q2_ref, k2_ref, v2_ref,
                    scr_ref):
    tm = x_ref.shape[1]
    h = _rms(x_ref[0], g_ref[...]).astype(BF16)

    def proj(col, width):
        return jnp.dot(h, w_ref[:, col:col + width], preferred_element_type=F32)

    uf = proj(COL_UF, FOURIER_WIDTH).astype(BF16)
    ab_ref[0] = jnp.dot(uf, cc_ref[...], preferred_element_type=F32).astype(BF16)
    ab_ref[1] = jnp.dot(uf, sc_ref[...], preferred_element_type=F32).astype(BF16)

    cos = cos_ref[...]
    sin = sin_ref[...]

    def rope(t):
        t1, t2 = t[:, :LANES], t[:, LANES:]
        return jnp.concatenate([t1 * cos - t2 * sin, t2 * cos + t1 * sin], axis=-1)

    def put(ref, val, d):
        if d == 1:
            ref[0, 0] = val.astype(BF16)
            return
        for c in range(GROUP_WIDTH // LANES):
            scr_ref[c] = val[:, c * LANES:(c + 1) * LANES]
        for r in range(d):
            for c in range(GROUP_WIDTH // LANES):
                ref[0, r, :, c * LANES:(c + 1) * LANES] = scr_ref[c, pl.ds(r, tm // d, stride=d), :].astype(BF16)

    outs = ((q0_ref, k0_ref, v0_ref), (q1_ref, k1_ref, v1_ref), (q2_ref, k2_ref, v2_ref))
    for g, d in enumerate(DILATIONS):
        qr, kr, vr = outs[g]
        put(qr, rope(proj(COL_Q + g * GROUP_WIDTH, GROUP_WIDTH)) * (HEAD_DIM ** -0.5 * LOG2E), d)
        put(kr, rope(proj(COL_K + g * GROUP_WIDTH, GROUP_WIDTH)), d)
        put(vr, proj(COL_V + g * GROUP_WIDTH, GROUP_WIDTH), d)


def _in_proj(x, gain, w, cc, sc, cos_t, sin_t, tm):
    b, s, _ = x.shape
    grid = (b, s // tm)
    qkv_shapes, qkv_specs = [], []
    for d in DILATIONS:
        for _ in range(3):
            qkv_shapes.append(jax.ShapeDtypeStruct((b, d, s // d, GROUP_WIDTH), BF16))
            qkv_specs.append(pl.BlockSpec((1, d, tm // d, GROUP_WIDTH), lambda bi, si: (bi, 0, si, 0)))
    out_shape = [jax.ShapeDtypeStruct((2, s, b * FOURIER_WIDTH), BF16)] + qkv_shapes
    out_specs = [pl.BlockSpec((2, tm, FOURIER_WIDTH), lambda bi, si: (0, si, bi))] + qkv_specs
    in_specs = [
        pl.BlockSpec((1, tm, D_MODEL), lambda bi, si: (bi, si, 0)),
        pl.BlockSpec((1, D_MODEL), lambda bi, si: (0, 0)),
        pl.BlockSpec((D_MODEL, COL_GF), lambda bi, si: (0, 0)),
        pl.BlockSpec((FOURIER_WIDTH, FOURIER_WIDTH), lambda bi, si: (0, 0)),
        pl.BlockSpec((FOURIER_WIDTH, FOURIER_WIDTH), lambda bi, si: (0, 0)),
        pl.BlockSpec((tm, LANES), lambda bi, si: (si, 0)),
        pl.BlockSpec((tm, LANES), lambda bi, si: (si, 0)),
    ]
    return pl.pallas_call(
        _in_proj_kernel, grid=grid, in_specs=in_specs, out_specs=out_specs, out_shape=out_shape,
        scratch_shapes=[pltpu.VMEM((GROUP_WIDTH // LANES, tm, LANES), F32)],
        compiler_params=_params(("parallel", "parallel")), name="in_proj",
    )(x, gain, w, cc, sc, cos_t, sin_t)


SEQ_RADIX = 4


def _seq_dft_kernel(ab_ref, tab_ref, tw_ref, o_ref, u_ref):
    q_len = tab_ref.shape[0]
    for c in range(o_ref.shape[2] // LANES):
        cols = slice(c * LANES, (c + 1) * LANES)
        re = [ab_ref[0, j, :, cols].astype(F32) for j in range(SEQ_RADIX)]
        im = [-ab_ref[1, j, :, cols].astype(F32) for j in range(SEQ_RADIX)]
        p_re, p_im, q_re, q_im = re[0] + re[2], im[0] + im[2], re[0] - re[2], im[0] - im[2]
        s_re, s_im, d_re, d_im = re[1] + re[3], im[1] + im[3], re[1] - re[3], im[1] - im[3]
        terms = ((p_re + s_re, p_im + s_im), (q_re + d_im, q_im - d_re),
                 (p_re - s_re, p_im - s_im), (q_re - d_im, q_im + d_re))
        for r, (t_re, t_im) in enumerate(terms):
            if r > 0:
                cos, sin = tw_ref[r - 1, 0], tw_ref[r - 1, 1]
                t_re, t_im = t_re * cos + t_im * sin, t_im * cos - t_re * sin
            u_ref[r, :q_len, cols] = t_re.astype(BF16)
            u_ref[r, q_len:, cols] = t_im.astype(BF16)
    for r in range(SEQ_RADIX):
        o_ref[r] = jnp.dot(tab_ref[...], u_ref[r], preferred_element_type=F32).astype(o_ref.dtype)


def _seq_dft(ab, tab, tw, bn):
    _, s, n = ab.shape
    q_len = s // SEQ_RADIX
    bn = min(bn, n)
    return pl.pallas_call(
        _seq_dft_kernel, grid=(n // bn,),
        in_specs=[pl.BlockSpec((2, SEQ_RADIX, q_len, bn), lambda j: (0, 0, 0, j)),
                  pl.BlockSpec((q_len, 2 * q_len), lambda j: (0, 0), pipeline_mode=pl.Buffered(1)),
                  pl.BlockSpec((SEQ_RADIX - 1, 2, q_len, LANES), lambda j: (0, 0, 0, 0),
                               pipeline_mode=pl.Buffered(1))],
        out_specs=pl.BlockSpec((SEQ_RADIX, q_len, bn), lambda j: (0, 0, j)),
        out_shape=jax.ShapeDtypeStruct((SEQ_RADIX, q_len, n), BF16),
        scratch_shapes=[pltpu.VMEM((SEQ_RADIX, 2 * q_len, bn), BF16)],
        compiler_params=_params(("parallel",)), name="seq_dft",
    )(ab.reshape(2, SEQ_RADIX, q_len, n), tab, tw)


ATTN_UNROLL = 4


def _attn_kernel(q_ref, k_ref, v_ref, o_ref, lse_ref, *, qb, tk, seq_len):
    lane = lax.broadcasted_iota(jnp.int32, (1, GROUP_WIDTH), 1)
    qk_head = (lane % LANES) // (HEAD_DIM // 2)
    v_head = lane // HEAD_DIM
    col_minus_row = (lax.broadcasted_iota(jnp.int32, (qb, tk), 1)
                     - lax.broadcasted_iota(jnp.int32, (qb, tk), 0))
    n_res = q_ref.shape[1]
    n_sub = seq_len // qb

    def query_block(res, sub):
        m0 = pl.multiple_of(sub * qb, qb)
        start = pl.multiple_of(jnp.clip(m0 - HALF_SPAN, 0, seq_len - tk), 16)
        q = q_ref[0, res, pl.ds(m0, qb), :]
        k = k_ref[0, res, pl.ds(start, tk), :]
        v = v_ref[0, res, pl.ds(start, tk), :]
        valid = jnp.abs(col_minus_row + (start - m0)) <= HALF_SPAN
        q_heads = jnp.concatenate(
            [jnp.where(qk_head == hd, q, jnp.zeros_like(q)) for hd in range(HEADS_PER_GROUP)], axis=0)
        s = lax.dot_general(q_heads, k, (((1,), (1,)), ((), ())), preferred_element_type=F32)
        s = jnp.where(valid[None], s.reshape(HEADS_PER_GROUP, qb, tk), NEG_INF)
        m = jnp.max(s, axis=-1, keepdims=True)
        p = jnp.exp2(s - m)
        l = jnp.sum(p, axis=-1, keepdims=True)
        pv = jnp.dot(p.astype(BF16).reshape(HEADS_PER_GROUP * qb, tk), v, preferred_element_type=F32)
        pv = pv.reshape(HEADS_PER_GROUP, qb, GROUP_WIDTH) / l
        lse = (m + jnp.log2(l)) * LN2
        o_acc = pv[0]
        lse_acc = jnp.broadcast_to(lse[0], (qb, GROUP_WIDTH))
        for hd in range(1, HEADS_PER_GROUP):
            o_acc = jnp.where(v_head == hd, pv[hd], o_acc)
            lse_acc = jnp.where(v_head == hd, lse[hd], lse_acc)
        o_ref[0, res, pl.ds(m0, qb), :] = o_acc.astype(o_ref.dtype)
        lse_ref[0, res, pl.ds(m0, qb), :] = lse_acc

    def body(it, carry):
        for j in range(ATTN_UNROLL):
            if n_sub % ATTN_UNROLL == 0:
                per_res = n_sub // ATTN_UNROLL
                query_block(it // per_res, (it % per_res) * ATTN_UNROLL + j)
            else:
                query_block(it * (ATTN_UNROLL // n_sub) + j // n_sub, j % n_sub)
        return carry

    lax.fori_loop(0, n_res * n_sub // ATTN_UNROLL, body, 0)


def _attention(q, k, v):
    b, d, seq_len, _ = q.shape
    qb = min(128, seq_len)
    tk = min(qb + 2 * HALF_SPAN, seq_len)
    n_sub = seq_len // qb
    n_res = min(d, max(1, 8 // n_sub))
    assert (n_res * n_sub) % ATTN_UNROLL == 0 and (n_sub % ATTN_UNROLL == 0 or ATTN_UNROLL % n_sub == 0)
    kern = functools.partial(_attn_kernel, qb=qb, tk=tk, seq_len=seq_len)
    blk = pl.BlockSpec((1, n_res, seq_len, GROUP_WIDTH), lambda bi, ri: (bi, ri, 0, 0))
    shape = (b, d, seq_len, GROUP_WIDTH)
    return pl.pallas_call(
        kern, grid=(b, d // n_res), in_specs=[blk, blk, blk], out_specs=[blk, blk],
        out_shape=[jax.ShapeDtypeStruct(shape, BF16), jax.ShapeDtypeStruct(shape, F32)],
        compiler_params=_params(("parallel", "parallel")), name=f"attn_d{d}",
    )(q, k, v)


def _mix_out_kernel(*refs, routed):
    (x_ref, yf_ref, o0_ref, l0_ref, o1_ref, l1_ref, o2_ref, l2_ref, g1_ref, wg_ref,
     wf_ref, wa_ref, wo_ref, g2_ref) = refs[:14]
    if routed:
        rw_ref, rb_ref, xn_ref, h2_ref, cw_ref, sel_ref, sel_t_ref, o_scr, l_scr, f_scr = refs[14:]
    else:
        xn_ref, h2_ref, o_scr, l_scr, f_scr = refs[14:]
    tm = x_ref.shape[1]

    def gathered(ref, scr, d):
        if d == 1:
            return ref[0]
        n_chunk = ref.shape[-1] // LANES
        for r in range(d):
            for c in range(n_chunk):
                scr[c, pl.ds(r, tm // d, stride=d), :] = ref[r, :, c * LANES:(c + 1) * LANES].astype(F32)
        return jnp.concatenate([scr[c] for c in range(n_chunk)], axis=-1)

    o_refs, l_refs = (o0_ref, o1_ref, o2_ref), (l0_ref, l1_ref, l2_ref)
    os_, ls_ = [], []
    for g, d in enumerate(DILATIONS):
        os_.append(gathered(o_refs[g].at[0], o_scr.at[g], d))
        ls_.append(gathered(l_refs[g].at[0], l_scr.at[g], d))
    mx = jnp.maximum(jnp.maximum(ls_[0], ls_[1]), ls_[2])
    es = [jnp.exp(l - mx) for l in ls_]
    den = es[0] + es[1] + es[2]
    o_att = (es[0] * os_[0] + es[1] * os_[1] + es[2] * os_[2]) / den

    y_a = jnp.dot(o_att.astype(BF16), wa_ref[...], preferred_element_type=F32)
    y_mix = gathered(yf_ref, f_scr, SEQ_RADIX).astype(BF16)
    y_f = jnp.dot(y_mix, wf_ref[...], preferred_element_type=F32)
    h1 = _rms(x_ref[0], g1_ref[...]).astype(BF16)
    gate_f = jax.nn.sigmoid(jnp.dot(h1, wg_ref[:, :D_MODEL], preferred_element_type=F32))
    gate_a = jax.nn.sigmoid(jnp.dot(h1, wg_ref[:, D_MODEL:], preferred_element_type=F32))
    z = gate_f * y_f + gate_a * y_a
    xn = x_ref[0] +jnp.dot(z.astype(BF16), wo_ref[...], preferred_element_type=F32)
    xn_ref[0] = xn
    h2 = _rms(xn, g2_ref[...])
    h2_hi = h2.astype(BF16)
    h2_ref[0] = h2_hi

    if routed:
        h2_lo = (h2 - h2_hi.astype(F32)).astype(BF16)
        logits = (jnp.dot(h2_hi, rw_ref[0], preferred_element_type=F32)
                  + jnp.dot(h2_hi, rw_ref[1], preferred_element_type=F32)
                  + jnp.dot(h2_lo, rw_ref[0], preferred_element_type=F32)) + rb_ref[...]
        lane = lax.broadcasted_iota(jnp.int32, logits.shape, 1)
        big = jnp.int32(LANES)
        m1 = jnp.max(logits, axis=-1, keepdims=True)
        i1 = jnp.min(jnp.where(logits == m1, lane, big), axis=-1, keepdims=True)
        rest = jnp.where(lane == i1, NEG_INF * 2, logits)
        m2 = jnp.max(rest, axis=-1, keepdims=True)
        i2 = jnp.min(jnp.where(rest == m2, lane, big), axis=-1, keepdims=True)
        e2 = jnp.exp(m2 - m1)
        w1 = 1.0 / (1.0 + e2)
        w2 = e2 / (1.0 + e2)
        cw_ref[...] = jnp.where(lane == i1, w1, jnp.where(lane == i2, w2, 0.0))
        sel = jnp.where((lane == i1) | (lane == i2), 1.0, 0.0)
        sel_ref[...] = sel
        sel_t_ref[...] = sel.T[:N_EXPERTS, :]


def _mix_out(x, yf, attn, g1, wg, wf, wa, wo, g2, router, tm):
    b, s, _ = x.shape
    routed = router is not None
    row = pl.BlockSpec((1, tm, D_MODEL), lambda bi, si: (bi, si, 0))
    full = lambda shape: pl.BlockSpec(shape, lambda bi, si: (0,) * len(shape), pipeline_mode=pl.Buffered(1))
    in_specs = [row, pl.BlockSpec((SEQ_RADIX, tm // SEQ_RADIX, FOURIER_WIDTH), lambda bi, si: (0, si, bi))]
    args = [x, yf]
    for d, (o, l) in zip(DILATIONS, attn):
        spec = pl.BlockSpec((1, d, tm // d, GROUP_WIDTH), lambda bi, si: (bi, 0, si, 0))
        in_specs += [spec, spec]
        args += [o, l]
    in_specs += [full((1, D_MODEL)), full((D_MODEL, 2 * D_MODEL)), full((FOURIER_WIDTH, D_MODEL)),
                 full((GROUP_WIDTH, D_MODEL)), full((D_MODEL, D_MODEL)), full((1, D_MODEL))]
    args += [g1, wg, wf, wa, wo, g2]
    out_shape = [jax.ShapeDtypeStruct((b, s, D_MODEL), F32), jax.ShapeDtypeStruct((b, s, D_MODEL), BF16)]
    out_specs = [row, row]
    if routed:
        in_specs += [full((2, D_MODEL, LANES)), full((1, LANES))]
        args += list(router)
        tok = pl.BlockSpec((tm, LANES), lambda bi, si: (bi * (s // tm) + si, 0))
        out_shape += [jax.ShapeDtypeStruct((b * s, LANES), F32)] * 2 + [jax.ShapeDtypeStruct((N_EXPERTS, b * s), F32)]
        out_specs += [tok, tok, pl.BlockSpec((N_EXPERTS, tm), lambda bi, si: (0, bi * (s // tm) + si))]
    return pl.pallas_call(
        functools.partial(_mix_out_kernel, routed=routed), grid=(b, s // tm),
        in_specs=in_specs, out_specs=out_specs, out_shape=out_shape,
        scratch_shapes=[pltpu.VMEM((len(DILATIONS), GROUP_WIDTH // LANES, tm, LANES), F32)] * 2
        + [pltpu.VMEM((FOURIER_WIDTH // LANES, tm, LANES), F32)],
        compiler_params=_params(("parallel", "parallel")), name="mix_out",
    )(*args)


def _swiglu(h, w1, w3, w2):
    a = jnp.dot(h, w1, preferred_element_type=F32)
    g = jnp.dot(h, w3, preferred_element_type=F32)
    act = (a * jax.nn.sigmoid(a) * g).astype(BF16)
    return jnp.dot(act, w2, preferred_element_type=F32)


def _ffn_kernel(h_ref, x_ref, w1_ref, w3_ref, w2_ref, gfin_ref, o_ref, *, final_norm):
    f = pl.program_id(1)

    @pl.when(f == 0)
    def _():
        o_ref[...] = x_ref[...]

    o_ref[...] += _swiglu(h_ref[...], w1_ref[...], w3_ref[...], w2_ref[...])

    if final_norm:
        @pl.when(f == pl.num_programs(1) - 1)
        def _():
            o_ref[...] = _rms(o_ref[...], gfin_ref[...])


def _ffn(h, x, w1, w3, w2, gfin, final_norm, tm, tf):
    n = h.shape[0]
    row = pl.BlockSpec((tm, D_MODEL), lambda i, f: (i, 0))
    in_specs = [row, row,
                pl.BlockSpec((D_MODEL, tf), lambda i, f: (0, f)),
                pl.BlockSpec((D_MODEL, tf), lambda i, f: (0, f)),
                pl.BlockSpec((tf, D_MODEL), lambda i, f: (f, 0)),
                pl.BlockSpec((1, D_MODEL), lambda i, f: (0, 0))]
    return pl.pallas_call(
        functools.partial(_ffn_kernel, final_norm=final_norm),
        grid=(n // tm, FFN_DIM // tf), in_specs=in_specs, out_specs=row,
        out_shape=jax.ShapeDtypeStruct((n, D_MODEL), F32),
        compiler_params=_params(("parallel", "arbitrary")), name="ffn",
    )(h, x, w1, w3, w2, gfin)


MOE_SUB = 256
MOE_WIN = 128
MOE_ROWS = 128
ROW_ALIGN = 16


def _moe_kernel(h_ref, sel_t_ref, sel_ref, cw_ref, w1_ref, w3_ref, w2_ref, triu_ref, tril_ref, o_ref,
                pos_t_ref, pos_ref, hs_ref, y_ref, pre_ref):
    tb = h_ref.shape[0]
    n_sub = tb // MOE_SUB
    blk, e, f = pl.program_id(0), pl.program_id(1), pl.program_id(2)
    n_f = pl.num_programs(2)

    @pl.when((blk == 0) & (e == 0) & (f == 0))
    def _():
        hs_ref[...] = jnp.zeros_like(hs_ref)
        y_ref[...] = jnp.zeros_like(y_ref)

    @pl.when((e == 0) & (f == 0))
    def _():
        o_ref[...] = jnp.zeros_like(o_ref)
        off_t = jnp.zeros((N_EXPERTS, 1), F32)
        off = jnp.zeros((1, LANES), F32)
        for s in range(n_sub):
            sl = slice(s * MOE_SUB, (s + 1) * MOE_SUB)
            sel_t = sel_t_ref[:, sl]
            within_t = jnp.dot(sel_t.astype(BF16), triu_ref[...], preferred_element_type=F32)
            pos_t_ref[:, sl] = jnp.where(sel_t > 0.0, within_t + off_t, -1.0)
            off_t = off_t + jnp.sum(sel_t, axis=1, keepdims=True)
            sel = sel_ref[sl, :]
            within = jnp.dot(tril_ref[...], sel.astype(BF16), preferred_element_type=F32)
            pos_ref[sl, :] = jnp.where(sel > 0.0, within + off, -1.0)
            off = off + jnp.sum(sel, axis=0, keepdims=True)

    def window(s):
        p0, p1 = pre_ref[s], pre_ref[s + 1]
        base = (p0 // ROW_ALIGN) * ROW_ALIGN
        return p0, base, (p1 - base + MOE_WIN - 1) // MOE_WIN

    @pl.when(f == 0)
    def _():
        run = jnp.int32(0)
        pre_ref[0] = run
        for s in range(n_sub):
            cnt = jnp.sum(sel_t_ref[pl.ds(e, 1), s * MOE_SUB:(s + 1) * MOE_SUB])
            run = run + cnt.astype(jnp.int32)
            pre_ref[s + 1] = run
        row = lax.broadcasted_iota(jnp.int32, (MOE_WIN, MOE_SUB), 0)
        row1 = lax.broadcasted_iota(jnp.int32, (MOE_WIN, 1), 0)

        def gather(s, j):
            p0, p1 = pre_ref[s], pre_ref[s + 1]
            _, base, _ = window(s)
            sl = slice(s * MOE_SUB, (s + 1) * MOE_SUB)
            r0 = pl.multiple_of(base + j * MOE_WIN, ROW_ALIGN)
            pos_i = pos_t_ref[pl.ds(e, 1), sl].astype(jnp.int32)
            one_hot = jnp.where(row == pos_i - r0, 1.0, 0.0).astype(BF16)
            rows = jnp.dot(one_hot, h_ref[sl, :], preferred_element_type=F32).astype(BF16)
            mine = (row1 >= p0 - r0) & (row1 < p1 - r0)
            hs_ref[pl.ds(r0, MOE_WIN), :] = jnp.where(mine, rows, hs_ref[pl.ds(r0, MOE_WIN), :])

        for s in range(n_sub):
            gather(s, 0)
        for s in range(n_sub):
            lax.fori_loop(1, window(s)[2], lambda j, c, s=s: (gather(s, j), c)[1], 0)

    n_units = (pre_ref[n_sub] + MOE_ROWS - 1) // MOE_ROWS
    n_full = n_units // 4
    rem = n_units % 4

    def expert_rows(r0, n_rows):
        y = _swiglu(hs_ref[pl.ds(r0, n_rows), :], w1_ref[0], w3_ref[0], w2_ref[0])

        @pl.when(f == 0)
        def _():
            y_ref[pl.ds(r0, n_rows), :] = y

        @pl.when(f > 0)
        def _():
            y_ref[pl.ds(r0, n_rows), :] += y

    def full_chunk(c, carry):
        expert_rows(pl.multiple_of(c * (4 * MOE_ROWS), 4 * MOE_ROWS), 4 * MOE_ROWS)
        return carry

    lax.fori_loop(0, n_full, full_chunk, 0)

    @pl.when(rem >= 2)
    def _():
        expert_rows(pl.multiple_of(n_full * (4 * MOE_ROWS), 2 * MOE_ROWS), 2 * MOE_ROWS)

    @pl.when(rem % 2 == 1)
    def _():
        expert_rows(pl.multiple_of((n_units - 1) * MOE_ROWS, MOE_ROWS), MOE_ROWS)

    @pl.when(f == n_f - 1)
    def _():
        lane = lax.broadcasted_iota(jnp.int32, (MOE_SUB, LANES), 1)

        def scatter(s, j):
            _, base, _ = window(s)
            sl = slice(s * MOE_SUB, (s + 1) * MOE_SUB)
            r0 = pl.multiple_of(base + j * MOE_WIN, ROW_ALIGN)
            mine = lane == e
            pos_i = jnp.sum(jnp.where(mine, pos_ref[sl, :], 0.0), axis=1, keepdims=True).astype(jnp.int32)
            weight = jnp.sum(jnp.where(mine, cw_ref[sl, :], 0.0), axis=1, keepdims=True)
            one_hot = jnp.where(lane == pos_i - r0, 1.0, 0.0).astype(BF16)
            rows = y_ref[pl.ds(r0, MOE_WIN), :].astype(BF16)
            back = jnp.dot(one_hot, rows, preferred_element_type=F32)
            o_ref[sl, :] = (o_ref[sl, :].astype(F32) + back * weight).astype(BF16)

        for s in range(n_sub):
            scatter(s, 0)
        for s in range(n_sub):
            lax.fori_loop(1, window(s)[2], lambda j, c, s=s: (scatter(s, j), c)[1], 0)


def _moe(h, sel_t, sel, cw, w1, w3, w2, tb, tf):
    n = h.shape[0]
    tb = min(tb, n)
    n_exp = w1.shape[0]
    triu = jnp.triu(jnp.ones((MOE_SUB, MOE_SUB), F32), k=1).astype(BF16)
    pad = 2 * MOE_WIN
    once = pl.Buffered(1)
    tok = pl.BlockSpec((tb, LANES), lambda i, e, f: (i, 0), pipeline_mode=once)
    tri = pl.BlockSpec((MOE_SUB, MOE_SUB), lambda i, e, f: (0, 0), pipeline_mode=once)
    in_specs = [pl.BlockSpec((tb, D_MODEL), lambda i, e, f: (i, 0), pipeline_mode=once),
                pl.BlockSpec((N_EXPERTS, tb), lambda i, e, f: (0, i), pipeline_mode=once),
                tok, tok,
                pl.BlockSpec((1, D_MODEL, tf), lambda i, e, f: (e, 0, f)),
                pl.BlockSpec((1, D_MODEL, tf), lambda i, e, f: (e, 0, f)),
                pl.BlockSpec((1, tf, D_MODEL), lambda i, e, f: (e, f, 0)),
                tri, tri]
    return pl.pallas_call(
        _moe_kernel, grid=(n // tb, n_exp, FFN_DIM // tf), in_specs=in_specs,
        out_specs=pl.BlockSpec((tb, D_MODEL), lambda i, e, f: (i, 0), pipeline_mode=once),
        out_shape=jax.ShapeDtypeStruct((n, D_MODEL), BF16),
        scratch_shapes=[pltpu.VMEM((N_EXPERTS, tb), F32),
                        pltpu.VMEM((tb, LANES), F32),
                        pltpu.VMEM((tb + pad, D_MODEL), BF16),
                        pltpu.VMEM((tb + pad, D_MODEL), F32),
                        pltpu.SMEM((tb // MOE_SUB + 1,), jnp.int32)],
        compiler_params=_params(("arbitrary", "arbitrary", "arbitrary")), name="moe",
    )(h, sel_t, sel, cw, w1, w3, w2, triu, triu.T)


def _residual_kernel(x_ref, y_ref, g_ref, o_ref, *, final_norm):
    v = x_ref[...] + y_ref[...].astype(F32)
    o_ref[...] = _rms(v, g_ref[...]) if final_norm else v


def _residual(x, y, gfin, final_norm, tm):
    n = x.shape[0]
    row = pl.BlockSpec((tm, D_MODEL), lambda i: (i, 0))
    return pl.pallas_call(
        functools.partial(_residual_kernel, final_norm=final_norm), grid=(n // tm,),
        in_specs=[row, row, pl.BlockSpec((1, D_MODEL), lambda i: (0, 0))], out_specs=row,
        out_shape=jax.ShapeDtypeStruct((n, D_MODEL), F32),
        compiler_params=_params(("parallel",)), name="residual",
    )(x, y, gfin)


def _qk_column_order():
    half = HEAD_DIM // 2
    order = []
    for g in range(len(DILATIONS)):
        base = g * GROUP_WIDTH
        for part in range(2):
            for hd in range(HEADS_PER_GROUP):
                order.extend(range(base + hd * HEAD_DIM + part * half, base + hd * HEAD_DIM + (part + 1) * half))
    return np.asarray(order, np.int32)


def _dft_tables(n):
    j = jnp.arange(n, dtype=jnp.int32)
    phase = (j[:, None] * j[None, :]) % n
    ang = phase.astype(F32) * (2.0 * np.pi / n)
    return jnp.cos(ang), jnp.sin(ang)


def _tables(s):
    half = HEAD_DIM // 2
    inv_freq = ROPE_THETA ** (-jnp.arange(half, dtype=F32) * 2.0 / HEAD_DIM)
    ang = jnp.arange(s, dtype=F32)[:, None] * inv_freq[None, :]
    cos_t = jnp.tile(jnp.cos(ang), (1, LANES // half))
    sin_t = jnp.tile(jnp.sin(ang), (1, LANES // half))
    cg, sg = _dft_tables(FOURIER_GROUP_DIM)
    eye = jnp.eye(FOURIER_GROUPS, dtype=F32)
    scale_c = FOURIER_GROUP_DIM ** -0.5
    cc = (jnp.kron(eye, cg) * scale_c).astype(BF16)
    sc = (jnp.kron(eye, sg) * scale_c).astype(BF16)
    q_len = s // SEQ_RADIX
    cq, sq = _dft_tables(q_len)
    seq_tab = (jnp.concatenate([cq, sq], axis=1) * s ** -0.5).astype(BF16)
    pos = jnp.arange(q_len, dtype=jnp.int32)[None, :]
    ang = ((jnp.arange(1, SEQ_RADIX, dtype=jnp.int32)[:, None] * pos) % s).astype(F32) * (2.0 * np.pi / s)
    seq_tw = jnp.broadcast_to(jnp.stack([jnp.cos(ang), jnp.sin(ang)], axis=1)[..., None],
                              (SEQ_RADIX - 1, 2, q_len, LANES))
    return cos_t, sin_t, cc, sc, seq_tab, seq_tw


def kernel(x, ln_mix, w_in, w_fourier, w_attn, w_out, ln_ffn, dense_w1, dense_w3, dense_w2,
           router_w, router_b, moe_w1, moe_w3, moe_w2, ln_final):
    b, s, _ = x.shape
    depth = w_in.shape[0]
    n = b * s
    tm = 512
    cos_t, sin_t, cc, sc, seq_tab, seq_tw = _tables(s)
    order = _qk_column_order()
    gfin = ln_final.reshape(1, D_MODEL)

    for layer in range(depth):
        w = w_in[layer]
        wq = w[:, COL_Q:COL_K][:, order]
        wk = w[:, COL_K:COL_V][:, order]
        w_l = jnp.concatenate([w[:, :COL_Q], wq, wk, w[:, COL_V:COL_GF]], axis=1).astype(BF16)
        w_gate = w[:, COL_GF:].astype(BF16)
        g1 = ln_mix[layer].reshape(1, D_MODEL)

        outs = _in_proj(x, ln_mix[layer].reshape(1, D_MODEL), w_l, cc, sc, cos_t, sin_t, min(1024, s))
        ab, qkv = outs[0], outs[1:10]
        yf = _seq_dft(ab, seq_tab, seq_tw, 256)
        attn = [_attention(qkv[3 * g], qkv[3 * g + 1], qkv[3 * g + 2]) for g in range(len(DILATIONS))]

        last = layer == depth - 1
        i = layer // 2
        if layer % 2 == 0:
            router = None
        else:
            rw = jnp.zeros((D_MODEL, LANES), F32).at[:, :N_EXPERTS].set(router_w[i])
            rw_hi = rw.astype(BF16)
            rw_lo = (rw - rw_hi.astype(F32)).astype(BF16)
            rb = jnp.full((1, LANES), NEG_INF, F32).at[0, :N_EXPERTS].set(router_b[i])
            router = (jnp.stack([rw_hi, rw_lo]), rb)
        res = _mix_out(x, yf, attn, g1, w_gate, w_fourier[layer].astype(BF16), w_attn[layer].astype(BF16),
                       w_out[layer].astype(BF16), ln_ffn[layer].reshape(1, D_MODEL), router, tm)
        xn, h2 = res[0].reshape(n, D_MODEL), res[1].reshape(n, D_MODEL)
        if layer % 2 == 0:
            y = _ffn(h2, xn, dense_w1[i].astype(BF16), dense_w3[i].astype(BF16),
                     dense_w2[i].astype(BF16), gfin, last, 1024, 512)
        else:
            y = _moe(h2, res[4], res[3], res[2], moe_w1[i].astype(BF16), moe_w3[i].astype(BF16),
                     moe_w2[i].astype(BF16), 2048, 1792)
            y = _residual(xn, y, gfin, last, 1024)
        x = y.reshape(b, s, D_MODEL)
    return x
```

```python
import functools

import jax
import jax.numpy as jnp
import numpy as np
from jax import lax
from jax.experimental import pallas as pl
from jax.experimental.pallas import tpu as pltpu

D_MODEL = 1024
FOURIER_GROUPS = 4
FOURIER_GROUP_DIM = 128
FOURIER_WIDTH = FOURIER_GROUPS * FOURIER_GROUP_DIM
DILATIONS = (1, 4, 16)
HALF_SPAN = 64
HEADS_PER_GROUP = 4
HEAD_DIM = 64
GROUP_WIDTH = HEADS_PER_GROUP * HEAD_DIM
ATTN_WIDTH = len(DILATIONS) * GROUP_WIDTH
ROPE_THETA = 10000.0
FFN_DIM = 3584
N_EXPERTS = 8
RMS_EPS = 1e-6
NEG_INF = -1e30
LOG2E = float(np.log2(np.e))
LN2 = float(np.log(2.0))
LANES = 128

COL_UF = 0
COL_Q = FOURIER_WIDTH
COL_K = COL_Q + ATTN_WIDTH
COL_V = COL_K + ATTN_WIDTH
COL_GF = COL_V + ATTN_WIDTH
COL_GA = COL_GF + D_MODEL
IN_COLS = COL_GA + D_MODEL

VMEM_LIMIT = 56 * 1024 * 1024

BF16 = jnp.bfloat16
F32 = jnp.float32


def _params(semantics):
    return pltpu.CompilerParams(dimension_semantics=semantics, vmem_limit_bytes=VMEM_LIMIT)


def _rms(x, gain):
    ms = jnp.mean(x * x, axis=-1, keepdims=True)
    return x * lax.rsqrt(ms + RMS_EPS) * gain


def _in_proj_kernel(x_ref, g_ref, w_ref, cc_ref, sc_ref, cos_ref, sin_ref,
                    ab_ref, q0_ref, k0_ref, v0_ref, q1_ref, k1_ref, v1_ref, q2_ref, k2_ref, v2_ref,
                    scr_ref):
    tm = x_ref.shape[1]
    h = _rms(x_ref[0], g_ref[...]).astype(BF16)

    def proj(col, width):
        return jnp.dot(h, w_ref[:, col:col + width], preferred_element_type=F32)

    uf = proj(COL_UF, FOURIER_WIDTH).astype(BF16)
    ab_ref[0] = jnp.dot(uf, cc_ref[...], preferred_element_type=F32).astype(BF16)
    ab_ref[1] = jnp.dot(uf, sc_ref[...], preferred_element_type=F32).astype(BF16)

    cos = cos_ref[...]
    sin = sin_ref[...]

    def rope(t):
        t1, t2 = t[:, :LANES], t[:, LANES:]
        return jnp.concatenate([t1 * cos - t2 * sin, t2 * cos + t1 * sin], axis=-1)

    def put(ref, val, d):
        if d == 1:
            ref[0, 0] = val.astype(BF16)
            return
        for c in range(GROUP_WIDTH // LANES):
            scr_ref[c] = val[:, c * LANES:(c + 1) * LANES]
        for r in range(d):
            for c in range(GROUP_WIDTH // LANES):
                ref[0, r, :, c * LANES:(c + 1) * LANES] = scr_ref[c, pl.ds(r, tm // d, stride=d), :].astype(BF16)

    outs = ((q0_ref, k0_ref, v0_ref), (q1_ref, k1_ref, v1_ref), (q2_ref, k2_ref, v2_ref))
    for g, d in enumerate(DILATIONS):
        qr, kr, vr = outs[g]
        put(qr, rope(proj(COL_Q + g * GROUP_WIDTH, GROUP_WIDTH)) * (HEAD_DIM ** -0.5 * LOG2E), d)
        put(kr, rope(proj(COL_K + g * GROUP_WIDTH, GROUP_WIDTH)), d)
        put(vr, proj(COL_V + g * GROUP_WIDTH, GROUP_WIDTH), d)


def _in_proj(x, gain, w, cc, sc, cos_t, sin_t, tm):
    b, s, _ = x.shape
    grid = (b, s // tm)
    qkv_shapes, qkv_specs = [], []
    for d in DILATIONS:
        for _ in range(3):
            qkv_shapes.append(jax.ShapeDtypeStruct((b, d, s // d, GROUP_WIDTH), BF16))
            qkv_specs.append(pl.BlockSpec((1, d, tm // d, GROUP_WIDTH), lambda bi, si: (bi, 0, si, 0)))
    out_shape = [jax.ShapeDtypeStruct((2, s, b * FOURIER_WIDTH), BF16)] + qkv_shapes
    out_specs = [pl.BlockSpec((2, tm, FOURIER_WIDTH), lambda bi, si: (0, si, bi))] + qkv_specs
    in_specs = [
        pl.BlockSpec((1, tm, D_MODEL), lambda bi, si: (bi, si, 0)),
        pl.BlockSpec((1, D_MODEL), lambda bi, si: (0, 0)),
        pl.BlockSpec((D_MODEL, COL_GF), lambda bi, si: (0, 0)),
        pl.BlockSpec((FOURIER_WIDTH, FOURIER_WIDTH), lambda bi, si: (0, 0)),
        pl.BlockSpec((FOURIER_WIDTH, FOURIER_WIDTH), lambda bi, si: (0, 0)),
        pl.BlockSpec((tm, LANES), lambda bi, si: (si, 0)),
        pl.BlockSpec((tm, LANES), lambda bi, si: (si, 0)),
    ]
    return pl.pallas_call(
        _in_proj_kernel, grid=grid, in_specs=in_specs, out_specs=out_specs, out_shape=out_shape,
        scratch_shapes=[pltpu.VMEM((GROUP_WIDTH // LANES, tm, LANES), F32)],
        compiler_params=_params(("parallel", "parallel")), name="in_proj",
    )(x, gain, w, cc, sc, cos_t, sin_t)


SEQ_RADIX = 4


def _seq_dft_kernel(ab_ref, tab_ref, tw_ref, o_ref, u_ref):
    q_len = tab_ref.shape[0]
    for c in range(o_ref.shape[2] // LANES):
        cols = slice(c * LANES, (c + 1) * LANES)
        re = [ab_ref[0, j, :, cols].astype(F32) for j in range(SEQ_RADIX)]
        im = [-ab_ref[1, j, :, cols].astype(F32) for j in range(SEQ_RADIX)]
        p_re, p_im, q_re, q_im = re[0] + re[2], im[0] + im[2], re[0] - re[2], im[0] - im[2]
        s_re, s_im, d_re, d_im = re[1] + re[3], im[1] + im[3], re[1] - re[3], im[1] - im[3]
        terms = ((p_re + s_re, p_im + s_im), (q_re + d_im, q_im - d_re),
                 (p_re - s_re, p_im - s_im), (q_re - d_im, q_im + d_re))
        for r, (t_re, t_im) in enumerate(terms):
            if r > 0:
                cos, sin = tw_ref[r - 1, 0], tw_ref[r - 1, 1]
                t_re, t_im = t_re * cos + t_im * sin, t_im * cos - t_re * sin
            u_ref[r, :q_len, cols] = t_re.astype(BF16)
            u_ref[r, q_len:, cols] = t_im.astype(BF16)
    for r in range(SEQ_RADIX):
        o_ref[r] = jnp.dot(tab_ref[...], u_ref[r], preferred_element_type=F32).astype(o_ref.dtype)


def _seq_dft(ab, tab, tw, bn):
    _, s, n = ab.shape
    q_len = s // SEQ_RADIX
    bn = min(bn, n)
    return pl.pallas_call(
        _seq_dft_kernel, grid=(n // bn,),
        in_specs=[pl.BlockSpec((2, SEQ_RADIX, q_len, bn), lambda j: (0, 0, 0, j)),
                  pl.BlockSpec((q_len, 2 * q_len), lambda j: (0, 0), pipeline_mode=pl.Buffered(1)),
                  pl.BlockSpec((SEQ_RADIX - 1, 2, q_len, LANES), lambda j: (0, 0, 0, 0),
                               pipeline_mode=pl.Buffered(1))],
        out_specs=pl.BlockSpec((SEQ_RADIX, q_len, bn), lambda j: (0, 0, j)),
        out_shape=jax.ShapeDtypeStruct((SEQ_RADIX, q_len, n), BF16),
        scratch_shapes=[pltpu.VMEM((SEQ_RADIX, 2 * q_len, bn), BF16)],
        compiler_params=_params(("parallel",)), name="seq_dft",
    )(ab.reshape(2, SEQ_RADIX, q_len, n), tab, tw)


ATTN_UNROLL = 4


def _attn_kernel(q_ref, k_ref, v_ref, o_ref, lse_ref, *, qb, tk, seq_len):
    lane = lax.broadcasted_iota(jnp.int32, (1, GROUP_WIDTH), 1)
    qk_head = (lane % LANES) // (HEAD_DIM // 2)
    v_head = lane // HEAD_DIM
    col_minus_row = (lax.broadcasted_iota(jnp.int32, (qb, tk), 1)
                     - lax.broadcasted_iota(jnp.int32, (qb, tk), 0))
    n_res = q_ref.shape[1]
    n_sub = seq_len // qb

    def query_block(res, sub):
        m0 = pl.multiple_of(sub * qb, qb)
        start = pl.multiple_of(jnp.clip(m0 - HALF_SPAN, 0, seq_len - tk), 16)
        q = q_ref[0, res, pl.ds(m0, qb), :]
        k = k_ref[0, res, pl.ds(start, tk), :]
        v = v_ref[0, res, pl.ds(start, tk), :]
        valid = jnp.abs(col_minus_row + (start - m0)) <= HALF_SPAN
        q_heads = jnp.concatenate(
            [jnp.where(qk_head == hd, q, jnp.zeros_like(q)) for hd in range(HEADS_PER_GROUP)], axis=0)
        s = lax.dot_general(q_heads, k, (((1,), (1,)), ((), ())), preferred_element_type=F32)
        s = jnp.where(valid[None], s.reshape(HEADS_PER_GROUP, qb, tk), NEG_INF)
        m = jnp.max(s, axis=-1, keepdims=True)
        p = jnp.exp2(s - m)
        l = jnp.sum(p, axis=-1, keepdims=True)
        pv = jnp.dot(p.astype(BF16).reshape(HEADS_PER_GROUP * qb, tk), v, preferred_element_type=F32)
        pv = pv.reshape(HEADS_PER_GROUP, qb, GROUP_WIDTH) / l
        lse = (m + jnp.log2(l)) * LN2
        o_acc = pv[0]
        lse_acc = jnp.broadcast_to(lse[0], (qb, GROUP_WIDTH))
        for hd in range(1, HEADS_PER_GROUP):
            o_acc = jnp.where(v_head == hd, pv[hd], o_acc)
            lse_acc = jnp.where(v_head == hd, lse[hd], lse_acc)
        o_ref[0, res, pl.ds(m0, qb), :] = o_acc.astype(o_ref.dtype)
        lse_ref[0, res, pl.ds(m0, qb), :] = lse_acc

    def body(it, carry):
        for j in range(ATTN_UNROLL):
            if n_sub % ATTN_UNROLL == 0:
                per_res = n_sub // ATTN_UNROLL
                query_block(it // per_res, (it % per_res) * ATTN_UNROLL + j)
            else:
                query_block(it * (ATTN_UNROLL // n_sub) + j // n_sub, j % n_sub)
        return carry

    lax.fori_loop(0, n_res * n_sub // ATTN_UNROLL, body, 0)


def _attention(q, k, v):
    b, d, seq_len, _ = q.shape
    qb = min(128, seq_len)
    tk = min(qb + 2 * HALF_SPAN, seq_len)
    n_sub = seq_len // qb
    n_res = min(d, max(1, 8 // n_sub))
    assert (n_res * n_sub) % ATTN_UNROLL == 0 and (n_sub % ATTN_UNROLL == 0 or ATTN_UNROLL % n_sub == 0)
    kern = functools.partial(_attn_kernel, qb=qb, tk=tk, seq_len=seq_len)
    blk = pl.BlockSpec((1, n_res, seq_len, GROUP_WIDTH), lambda bi, ri: (bi, ri, 0, 0))
    shape = (b, d, seq_len, GROUP_WIDTH)
    return pl.pallas_call(
        kern, grid=(b, d // n_res), in_specs=[blk, blk, blk], out_specs=[blk, blk],
        out_shape=[jax.ShapeDtypeStruct(shape, BF16), jax.ShapeDtypeStruct(shape, F32)],
        compiler_params=_params(("parallel", "parallel")), name=f"attn_d{d}",
    )(q, k, v)


def _mix_out_kernel(*refs, routed):
    (x_ref, yf_ref, o0_ref, l0_ref, o1_ref, l1_ref, o2_ref, l2_ref, g1_ref, wg_ref,
     wf_ref, wa_ref, wo_ref, g2_ref) = refs[:14]
    if routed:
        rw_ref, rb_ref, xn_ref, h2_ref, cw_ref, sel_ref, sel_t_ref, o_scr, l_scr, f_scr = refs[14:]
    else:
        xn_ref, h2_ref, o_scr, l_scr, f_scr = refs[14:]
    tm = x_ref.shape[1]

    def gathered(ref, scr, d):
        if d == 1:
            return ref[0]
        n_chunk = ref.shape[-1] // LANES
        for r in range(d):
            for c in range(n_chunk):
                scr[c, pl.ds(r, tm // d, stride=d), :] = ref[r, :, c * LANES:(c + 1) * LANES].astype(F32)
        return jnp.concatenate([scr[c] for c in range(n_chunk)], axis=-1)

    o_refs, l_refs = (o0_ref, o1_ref, o2_ref), (l0_ref, l1_ref, l2_ref)
    os_, ls_ = [], []
    for g, d in enumerate(DILATIONS):
        os_.append(gathered(o_refs[g].at[0], o_scr.at[g], d))
        ls_.append(gathered(l_refs[g].at[0], l_scr.at[g], d))
    mx = jnp.maximum(jnp.maximum(ls_[0], ls_[1]), ls_[2])
    es = [jnp.exp(l - mx) for l in ls_]
    den = es[0] + es[1] + es[2]
    o_att = (es[0] * os_[0] + es[1] * os_[1] + es[2] * os_[2]) / den

    y_a = jnp.dot(o_att.astype(BF16), wa_ref[...], preferred_element_type=F32)
    y_mix = gathered(yf_ref, f_scr, SEQ_RADIX).astype(BF16)
    y_f = jnp.dot(y_mix, wf_ref[...], preferred_element_type=F32)
    h1 = _rms(x_ref[0], g1_ref[...]).astype(BF16)
    gate_f = jax.nn.sigmoid(jnp.dot(h1, wg_ref[:, :D_MODEL], preferred_element_type=F32))
    gate_a = jax.nn.sigmoid(jnp.dot(h1, wg_ref[:, D_MODEL:], preferred_element_type=F32))
    z = gate_f * y_f + gate_a * y_a
    xn = x_ref[0] +jnp.dot(z.astype(BF16), wo_ref[...], preferred_element_type=F32)
    xn_ref[0] = xn
    h2 = _rms(xn, g2_ref[...])
    h2_hi = h2.astype(BF16)
    h2_ref[0] = h2_hi

    if routed:
        h2_lo = (h2 - h2_hi.astype(F32)).astype(BF16)
        hi = jnp.dot(h2_hi, rw_ref[...], preferred_element_type=F32)
        logits = (hi[:, :LANES] + hi[:, LANES:]
                  + jnp.dot(h2_lo, rw_ref[:, :LANES], preferred_element_type=F32)) + rb_ref[...]
        lane = lax.broadcasted_iota(jnp.int32, logits.shape, 1)
        big = jnp.int32(LANES)
        m1 = jnp.max(logits, axis=-1, keepdims=True)
        i1 = jnp.min(jnp.where(logits == m1, lane, big), axis=-1, keepdims=True)
        rest = jnp.where(lane == i1, NEG_INF * 2, logits)
        m2 = jnp.max(rest, axis=-1, keepdims=True)
        i2 = jnp.min(jnp.where(rest == m2, lane, big), axis=-1, keepdims=True)
        e2 = jnp.exp(m2 - m1)
        w1 = 1.0 / (1.0 + e2)
        w2 = e2 / (1.0 + e2)
        cw_ref[...] = jnp.where(lane == i1, w1, jnp.where(lane == i2, w2, 0.0))
        sel = jnp.where((lane == i1) | (lane == i2), 1.0, 0.0)
        sel_ref[...] = sel
        sel_t_ref[...] = sel.T[:N_EXPERTS, :]


def _mix_out(x, yf, attn, g1, wg, wf, wa, wo, g2, router, tm):
    b, s, _ = x.shape
    routed = router is not None
    row = pl.BlockSpec((1, tm, D_MODEL), lambda bi, si: (bi, si, 0))
    full = lambda shape: pl.BlockSpec(shape, lambda bi, si: (0,) * len(shape), pipeline_mode=pl.Buffered(1))
    in_specs = [row, pl.BlockSpec((SEQ_RADIX, tm // SEQ_RADIX, FOURIER_WIDTH), lambda bi, si: (0, si, bi))]
    args = [x, yf]
    for d, (o, l) in zip(DILATIONS, attn):
        spec = pl.BlockSpec((1, d, tm // d, GROUP_WIDTH), lambda bi, si: (bi, 0, si, 0))
        in_specs += [spec, spec]
        args += [o, l]
    in_specs += [full((1, D_MODEL)), full((D_MODEL, 2 * D_MODEL)), full((FOURIER_WIDTH, D_MODEL)),
                 full((GROUP_WIDTH, D_MODEL)), full((D_MODEL, D_MODEL)), full((1, D_MODEL))]
    args += [g1, wg, wf, wa, wo, g2]
    out_shape = [jax.ShapeDtypeStruct((b, s, D_MODEL), F32), jax.ShapeDtypeStruct((b, s, D_MODEL), BF16)]
    out_specs = [row, row]
    if routed:
        in_specs += [full((D_MODEL, 2 * LANES)), full((1, LANES))]
        args += list(router)
        tok = pl.BlockSpec((tm, LANES), lambda bi, si: (bi * (s // tm) + si, 0))
        out_shape += [jax.ShapeDtypeStruct((b * s, LANES), F32)] * 2 + [jax.ShapeDtypeStruct((N_EXPERTS, b * s), F32)]
        out_specs += [tok, tok, pl.BlockSpec((N_EXPERTS, tm), lambda bi, si: (0, bi * (s // tm) + si))]
    return pl.pallas_call(
        functools.partial(_mix_out_kernel, routed=routed), grid=(b, s // tm),
        in_specs=in_specs, out_specs=out_specs, out_shape=out_shape,
        scratch_shapes=[pltpu.VMEM((len(DILATIONS), GROUP_WIDTH // LANES, tm, LANES), F32)] * 2
        + [pltpu.VMEM((FOURIER_WIDTH // LANES, tm, LANES), F32)],
        compiler_params=_params(("parallel", "parallel")), name="mix_out",
    )(*args)


def _swiglu(h, w1, w3, w2):
    a = jnp.dot(h, w1, preferred_element_type=F32)
    g = jnp.dot(h, w3, preferred_element_type=F32)
    act = (a * jax.nn.sigmoid(a) * g).astype(BF16)
    return jnp.dot(act, w2, preferred_element_type=F32)


def _ffn_kernel(h_ref, x_ref, w1_ref, w3_ref, w2_ref, gfin_ref, *rest, final_norm):
    o_ref = rest[len(rest) // 2]
    f = pl.program_id(1)

    @pl.when(f == 0)
    def _():
        o_ref[...] = x_ref[...]

    o_ref[...] += _swiglu(h_ref[...], w1_ref[...], w3_ref[...], w2_ref[...])

    if final_norm:
        @pl.when(f == pl.num_programs(1) - 1)
        def _():
            o_ref[...] = _rms(o_ref[...], gfin_ref[...])

    n_cast = len(rest) // 2
    for src, dst in zip(rest[:n_cast], rest[n_cast + 1:]):
        dst[...] = src[...].astype(dst.dtype)


def _ffn(h, x, w1, w3, w2, gfin, final_norm, tm, tf, cast_cols=(), cast_rows=()):
    n = h.shape[0]
    grid = (n // tm, FFN_DIM // tf)
    row = pl.BlockSpec((tm, D_MODEL), lambda i, f: (i, 0))
    in_specs = [row, row,
                pl.BlockSpec((D_MODEL, tf), lambda i, f: (0, f)),
                pl.BlockSpec((D_MODEL, tf), lambda i, f: (0, f)),
                pl.BlockSpec((tf, D_MODEL), lambda i, f: (f, 0)),
                pl.BlockSpec((1, D_MODEL), lambda i, f: (0, 0))]
    cast_specs, cast_shapes = [], []
    for arr, axis in [(a, 2) for a in cast_cols] + [(a, 1) for a in cast_rows]:
        per_expert = arr.shape[axis] // LANES
        assert arr.shape[0] * per_expert == grid[0] * grid[1], "one cast block per grid step"
        shape = (1, arr.shape[1], LANES) if axis == 2 else (1, LANES, arr.shape[2])

        def index(i, f, per_expert=per_expert, axis=axis):
            t = i * grid[1] + f
            return (t // per_expert, 0, t % per_expert) if axis == 2 else (t // per_expert, t % per_expert, 0)

        cast_specs.append(pl.BlockSpec(shape, index))
        cast_shapes.append(jax.ShapeDtypeStruct(arr.shape, BF16))
    outs = pl.pallas_call(
        functools.partial(_ffn_kernel, final_norm=final_norm),
        grid=grid, in_specs=in_specs + cast_specs, out_specs=[row] + cast_specs,
        out_shape=[jax.ShapeDtypeStruct((n, D_MODEL), F32)] + cast_shapes,
        compiler_params=_params(("arbitrary", "arbitrary")), name="ffn",
    )(h, x, w1, w3, w2, gfin, *cast_cols, *cast_rows)
    return outs[0], outs[1:]


MOE_SUB = 256
MOE_WIN = 128
MOE_ROWS = 128
ROW_ALIGN = 16


def _moe_kernel(h_ref, sel_t_ref, sel_ref, cw_ref, w1_ref, w3_ref, w2_ref, triu_ref, tril_ref, o_ref,
                pos_t_ref, pos_ref, hs_ref, y_ref, pre_ref):
    tb = h_ref.shape[0]
    n_sub = tb // MOE_SUB
    blk, e, f = pl.program_id(0), pl.program_id(1), pl.program_id(2)
    n_f = pl.num_programs(2)

    @pl.when((blk == 0) & (e == 0) & (f == 0))
    def _():
        hs_ref[...] = jnp.zeros_like(hs_ref)
        y_ref[...] = jnp.zeros_like(y_ref)

    @pl.when((e == 0) & (f == 0))
    def _():
        o_ref[...] = jnp.zeros_like(o_ref)
        off_t = jnp.zeros((N_EXPERTS, 1), F32)
        off = jnp.zeros((1, LANES), F32)
        for s in range(n_sub):
            sl = slice(s * MOE_SUB, (s + 1) * MOE_SUB)
            sel_t = sel_t_ref[:, sl]
            within_t = jnp.dot(sel_t.astype(BF16), triu_ref[...], preferred_element_type=F32)
            pos_t_ref[:, sl] = jnp.where(sel_t > 0.0, within_t + off_t, -1.0)
            off_t = off_t + jnp.sum(sel_t, axis=1, keepdims=True)
            sel = sel_ref[sl, :]
            within = jnp.dot(tril_ref[...], sel.astype(BF16), preferred_element_type=F32)
            pos_ref[sl, :] = jnp.where(sel > 0.0, within + off, -1.0)
            off = off + jnp.sum(sel, axis=0, keepdims=True)

    def window(s):
        p0, p1 = pre_ref[s], pre_ref[s + 1]
        base = (p0 // ROW_ALIGN) * ROW_ALIGN
        return p0, base, (p1 - base + MOE_WIN - 1) // MOE_WIN

    @pl.when(f == 0)
    def _():
        run = jnp.int32(0)
        pre_ref[0] = run
        for s in range(n_sub):
            cnt = jnp.sum(sel_t_ref[pl.ds(e, 1), s * MOE_SUB:(s + 1) * MOE_SUB])
            run = run + cnt.astype(jnp.int32)
            pre_ref[s + 1] = run
        row = lax.broadcasted_iota(jnp.int32, (MOE_WIN, MOE_SUB), 0)
        row1 = lax.broadcasted_iota(jnp.int32, (MOE_WIN, 1), 0)

        def gather(s, j):
            p0, p1 = pre_ref[s], pre_ref[s + 1]
            _, base, _ = window(s)
            sl = slice(s * MOE_SUB, (s + 1) * MOE_SUB)
            r0 = pl.multiple_of(base + j * MOE_WIN, ROW_ALIGN)
            pos_i = pos_t_ref[pl.ds(e, 1), sl].astype(jnp.int32)
            one_hot = jnp.where(row == pos_i - r0, 1.0, 0.0).astype(BF16)
            rows = jnp.dot(one_hot, h_ref[sl, :], preferred_element_type=F32).astype(BF16)
            mine = (row1 >= p0 - r0) & (row1 < p1 - r0)
            hs_ref[pl.ds(r0, MOE_WIN), :] = jnp.where(mine, rows, hs_ref[pl.ds(r0, MOE_WIN), :])

        for s in range(n_sub):
            gather(s, 0)
        for s in range(n_sub):
            lax.fori_loop(1, window(s)[2], lambda j, c, s=s: (gather(s, j), c)[1], 0)

    n_units = (pre_ref[n_sub] + MOE_ROWS - 1) // MOE_ROWS
    n_full = n_units // 4
    rem = n_units % 4

    def expert_rows(r0, n_rows):
        y = _swiglu(hs_ref[pl.ds(r0, n_rows), :], w1_ref[0], w3_ref[0], w2_ref[0])

        @pl.when(f == 0)
        def _():
            y_ref[pl.ds(r0, n_rows), :] = y

        @pl.when(f > 0)
        def _():
            y_ref[pl.ds(r0, n_rows), :] += y

    def full_chunk(c, carry):
        expert_rows(pl.multiple_of(c * (4 * MOE_ROWS), 4 * MOE_ROWS), 4 * MOE_ROWS)
        return carry

    lax.fori_loop(0, n_full, full_chunk, 0)

    @pl.when(rem >= 2)
    def _():
        expert_rows(pl.multiple_of(n_full * (4 * MOE_ROWS), 2 * MOE_ROWS), 2 * MOE_ROWS)

    @pl.when(rem % 2 == 1)
    def _():
        expert_rows(pl.multiple_of((n_units - 1) * MOE_ROWS, MOE_ROWS), MOE_ROWS)

    @pl.when(f == n_f - 1)
    def _():
        lane = lax.broadcasted_iota(jnp.int32, (MOE_SUB, LANES), 1)

        def scatter(s, j):
            _, base, _ = window(s)
            sl = slice(s * MOE_SUB, (s + 1) * MOE_SUB)
            r0 = pl.multiple_of(base + j * MOE_WIN, ROW_ALIGN)
            mine = lane == e
            pos_i = jnp.sum(jnp.where(mine, pos_ref[sl, :], 0.0), axis=1, keepdims=True).astype(jnp.int32)
            weight = jnp.sum(jnp.where(mine, cw_ref[sl, :], 0.0), axis=1, keepdims=True)
            one_hot = jnp.where(lane == pos_i - r0, 1.0, 0.0).astype(BF16)
            rows = y_ref[pl.ds(r0, MOE_WIN), :].astype(BF16)
            back = jnp.dot(one_hot, rows, preferred_element_type=F32)
            o_ref[sl, :] = (o_ref[sl, :].astype(F32) + back * weight).astype(BF16)

        for s in range(n_sub):
            scatter(s, 0)
        for s in range(n_sub):
            lax.fori_loop(1, window(s)[2], lambda j, c, s=s: (scatter(s, j), c)[1], 0)


def _moe(h, sel_t, sel, cw, w1, w3, w2, tb, tf):
    n = h.shape[0]
    tb = min(tb, n)
    n_exp = w1.shape[0]
    triu = jnp.triu(jnp.ones((MOE_SUB, MOE_SUB), F32), k=1).astype(BF16)
    pad = 2 * MOE_WIN
    once = pl.Buffered(1)
    tok = pl.BlockSpec((tb, LANES), lambda i, e, f: (i, 0), pipeline_mode=once)
    tri = pl.BlockSpec((MOE_SUB, MOE_SUB), lambda i, e, f: (0, 0), pipeline_mode=once)
    in_specs = [pl.BlockSpec((tb, D_MODEL), lambda i, e, f: (i, 0), pipeline_mode=once),
                pl.BlockSpec((N_EXPERTS, tb), lambda i, e, f: (0, i), pipeline_mode=once),
                tok, tok,
                pl.BlockSpec((1, D_MODEL, tf), lambda i, e, f: (e, 0, f)),
                pl.BlockSpec((1, D_MODEL, tf), lambda i, e, f: (e, 0, f)),
                pl.BlockSpec((1, tf, D_MODEL), lambda i, e, f: (e, f, 0)),
                tri, tri]
    return pl.pallas_call(
        _moe_kernel, grid=(n // tb, n_exp, FFN_DIM // tf), in_specs=in_specs,
        out_specs=pl.BlockSpec((tb, D_MODEL), lambda i, e, f: (i, 0), pipeline_mode=once),
        out_shape=jax.ShapeDtypeStruct((n, D_MODEL), BF16),
        scratch_shapes=[pltpu.VMEM((N_EXPERTS, tb), F32),
                        pltpu.VMEM((tb, LANES), F32),
                        pltpu.VMEM((tb + pad, D_MODEL), BF16),
                        pltpu.VMEM((tb + pad, D_MODEL), F32),
                        pltpu.SMEM((tb // MOE_SUB + 1,), jnp.int32)],
        compiler_params=_params(("arbitrary", "arbitrary", "arbitrary")), name="moe",
    )(h, sel_t, sel, cw, w1, w3, w2, triu, triu.T)


def _residual_kernel(x_ref, y_ref, g_ref, o_ref, *, final_norm):
    v = x_ref[...] + y_ref[...].astype(F32)
    o_ref[...] = _rms(v, g_ref[...]) if final_norm else v


def _residual(x, y, gfin, final_norm, tm):
    n = x.shape[0]
    row = pl.BlockSpec((tm, D_MODEL), lambda i: (i, 0))
    return pl.pallas_call(
        functools.partial(_residual_kernel, final_norm=final_norm), grid=(n // tm,),
        in_specs=[row, row, pl.BlockSpec((1, D_MODEL), lambda i: (0, 0))], out_specs=row,
        out_shape=jax.ShapeDtypeStruct((n, D_MODEL), F32),
        compiler_params=_params(("parallel",)), name="residual",
    )(x, y, gfin)


def _qk_column_order():
    half = HEAD_DIM // 2
    order = []
    for g in range(len(DILATIONS)):
        base = g * GROUP_WIDTH
        for part in range(2):
            for hd in range(HEADS_PER_GROUP):
                order.extend(range(base + hd * HEAD_DIM + part * half, base + hd * HEAD_DIM + (part + 1) * half))
    return np.asarray(order, np.int32)


def _dft_tables(n):
    j = jnp.arange(n, dtype=jnp.int32)
    phase = (j[:, None] * j[None, :]) % n
    ang = phase.astype(F32) * (2.0 * np.pi / n)
    return jnp.cos(ang), jnp.sin(ang)


def _tables(s):
    half = HEAD_DIM // 2
    inv_freq = ROPE_THETA ** (-jnp.arange(half, dtype=F32) * 2.0 / HEAD_DIM)
    ang = jnp.arange(s, dtype=F32)[:, None] * inv_freq[None, :]
    cos_t = jnp.tile(jnp.cos(ang), (1, LANES // half))
    sin_t = jnp.tile(jnp.sin(ang), (1, LANES // half))
    cg, sg = _dft_tables(FOURIER_GROUP_DIM)
    eye = jnp.eye(FOURIER_GROUPS, dtype=F32)
    scale_c = FOURIER_GROUP_DIM ** -0.5
    cc = (jnp.kron(eye, cg) * scale_c).astype(BF16)
    sc = (jnp.kron(eye, sg) * scale_c).astype(BF16)
    q_len = s // SEQ_RADIX
    cq, sq = _dft_tables(q_len)
    seq_tab = (jnp.concatenate([cq, sq], axis=1) * s ** -0.5).astype(BF16)
    pos = jnp.arange(q_len, dtype=jnp.int32)[None, :]
    ang = ((jnp.arange(1, SEQ_RADIX, dtype=jnp.int32)[:, None] * pos) % s).astype(F32) * (2.0 * np.pi / s)
    seq_tw = jnp.broadcast_to(jnp.stack([jnp.cos(ang), jnp.sin(ang)], axis=1)[..., None],
                              (SEQ_RADIX - 1, 2, q_len, LANES))
    return cos_t, sin_t, cc, sc, seq_tab, seq_tw


def kernel(x, ln_mix, w_in, w_fourier, w_attn, w_out, ln_ffn, dense_w1, dense_w3, dense_w2,
           router_w, router_b, moe_w1, moe_w3, moe_w2, ln_final):
    b, s, _ = x.shape
    depth = w_in.shape[0]
    n = b * s
    tm = 512
    cos_t, sin_t, cc, sc, seq_tab, seq_tw = _tables(s)
    order = _qk_column_order()
    gfin = ln_final.reshape(1, D_MODEL)
    expert_bf16 = []

    for layer in range(depth):
        w = w_in[layer]
        wq = w[:, COL_Q:COL_K][:, order]
        wk = w[:, COL_K:COL_V][:, order]
        w_l = jnp.concatenate([w[:, :COL_Q], wq, wk, w[:, COL_V:COL_GF]], axis=1).astype(BF16)
        w_gate = w[:, COL_GF:].astype(BF16)
        g1 = ln_mix[layer].reshape(1, D_MODEL)

        outs = _in_proj(x, ln_mix[layer].reshape(1, D_MODEL), w_l, cc, sc, cos_t, sin_t, min(1024, s))
        ab, qkv = outs[0], outs[1:10]
        yf = _seq_dft(ab, seq_tab, seq_tw, 256)
        attn = [_attention(qkv[3 * g], qkv[3 * g + 1], qkv[3 * g + 2]) for g in range(len(DILATIONS))]

        last = layer == depth - 1
        i = layer // 2
        if layer % 2 == 0:
            router = None
        else:
            rw = jnp.zeros((D_MODEL, LANES), F32).at[:, :N_EXPERTS].set(router_w[i])
            rw_hi = rw.astype(BF16)
            rw_lo = (rw - rw_hi.astype(F32)).astype(BF16)
            rb = jnp.full((1, LANES), NEG_INF, F32).at[0, :N_EXPERTS].set(router_b[i])
            router = (jnp.concatenate([rw_hi, rw_lo], axis=1), rb)
        res = _mix_out(x, yf, attn, g1, w_gate, w_fourier[layer].astype(BF16), w_attn[layer].astype(BF16),
                       w_out[layer].astype(BF16), ln_ffn[layer].reshape(1, D_MODEL), router, tm)
        xn, h2 = res[0].reshape(n, D_MODEL), res[1].reshape(n, D_MODEL)
        if layer % 2 == 0:
            ffn_tm, ffn_tf = 1024, 512
            ride = (not last and
                    (n // ffn_tm) * (FFN_DIM // ffn_tf) == N_EXPERTS * (FFN_DIM // LANES))
            y, expert_bf16 = _ffn(h2, xn, dense_w1[i].astype(BF16), dense_w3[i].astype(BF16),
                                  dense_w2[i].astype(BF16), gfin, last, ffn_tm, ffn_tf,
                                  (moe_w1[i], moe_w3[i]) if ride else (), (moe_w2[i],) if ride else ())
        else:
            if not expert_bf16:
                expert_bf16 = [w[i].astype(BF16) for w in (moe_w1, moe_w3, moe_w2)]
            y = _moe(h2, res[4], res[3], res[2], *expert_bf16, 2048, 1792)
            y = _residual(xn, y, gfin, last, 1024)
        x = y.reshape(b, s, D_MODEL)
    return x
```

```python
import functools

import jax
import jax.numpy as jnp
import numpy as np
from jax import lax
from jax.experimental import pallas as pl
from jax.experimental.pallas import tpu as pltpu

D_MODEL = 1024
FOURIER_GROUPS = 4
FOURIER_GROUP_DIM = 128
FOURIER_WIDTH = FOURIER_GROUPS * FOURIER_GROUP_DIM
DILATIONS = (1, 4, 16)
HALF_SPAN = 64
HEADS_PER_GROUP = 4
HEAD_DIM = 64
GROUP_WIDTH = HEADS_PER_GROUP * HEAD_DIM
ATTN_WIDTH = len(DILATIONS) * GROUP_WIDTH
ROPE_THETA = 10000.0
FFN_DIM = 3584
N_EXPERTS = 8
RMS_EPS = 1e-6
NEG_INF = -1e30
LOG2E = float(np.log2(np.e))
LN2 = float(np.log(2.0))
LANES = 128

COL_UF = 0
COL_Q = FOURIER_WIDTH
COL_K = COL_Q + ATTN_WIDTH
COL_V = COL_K + ATTN_WIDTH
COL_GF = COL_V + ATTN_WIDTH
COL_GA = COL_GF + D_MODEL
IN_COLS = COL_GA + D_MODEL

VMEM_LIMIT = 56 * 1024 * 1024
MOE_VMEM_LIMIT = 60 * 1024 * 1024

BF16 = jnp.bfloat16
F32 = jnp.float32


def _params(semantics, vmem_limit=VMEM_LIMIT):
    return pltpu.CompilerParams(dimension_semantics=semantics, vmem_limit_bytes=vmem_limit)


def _rms(x, gain):
    ms = jnp.mean(x * x, axis=-1, keepdims=True)
    return x * lax.rsqrt(ms + RMS_EPS) * gain


def _in_proj_kernel(x_ref, g_ref, w_ref, cc_ref, sc_ref, cos_ref, sin_ref,
                    ab_ref, q0_ref, k0_ref, v0_ref, q1_ref, k1_ref, v1_ref, q2_ref, k2_ref, v2_ref,
                    scr_ref):
    tm = x_ref.shape[1]
    h = _rms(x_ref[0], g_ref[...]).astype(BF16)

    def proj(col, width):
        return jnp.dot(h, w_ref[:, col:col + width], preferred_element_type=F32)

    uf = proj(COL_UF, FOURIER_WIDTH).astype(BF16)
    ab_ref[0] = jnp.dot(uf, cc_ref[...], preferred_element_type=F32).astype(BF16)
    ab_ref[1] = jnp.dot(uf, sc_ref[...], preferred_element_type=F32).astype(BF16)

    cos = cos_ref[...]
    sin = sin_ref[...]

    def rope(t):
        t1, t2 = t[:, :LANES], t[:, LANES:]
        return jnp.concatenate([t1 * cos - t2 * sin, t2 * cos + t1 * sin], axis=-1)

    def put(ref, val, d):
        if d == 1:
            ref[0, 0] = val.astype(BF16)
            return
        for c in range(GROUP_WIDTH // LANES):
            scr_ref[c] = val[:, c * LANES:(c + 1) * LANES]
        for r in range(d):
            for c in range(GROUP_WIDTH // LANES):
                ref[0, r, :, c * LANES:(c + 1) * LANES] = scr_ref[c, pl.ds(r, tm // d, stride=d), :].astype(BF16)

    outs = ((q0_ref, k0_ref, v0_ref), (q1_ref, k1_ref, v1_ref), (q2_ref, k2_ref, v2_ref))
    for g, d in enumerate(DILATIONS):
        qr, kr, vr = outs[g]
        put(qr, rope(proj(COL_Q + g * GROUP_WIDTH, GROUP_WIDTH)) * (HEAD_DIM ** -0.5 * LOG2E), d)
        put(kr, rope(proj(COL_K + g * GROUP_WIDTH, GROUP_WIDTH)), d)
        put(vr, proj(COL_V + g * GROUP_WIDTH, GROUP_WIDTH), d)


def _in_proj(x, gain, w, cc, sc, cos_t, sin_t, tm):
    b, s, _ = x.shape
    grid = (b, s // tm)
    qkv_shapes, qkv_specs = [], []
    for d in DILATIONS:
        for _ in range(3):
            qkv_shapes.append(jax.ShapeDtypeStruct((b, d, s // d, GROUP_WIDTH), BF16))
            qkv_specs.append(pl.BlockSpec((1, d, tm // d, GROUP_WIDTH), lambda bi, si: (bi, 0, si, 0)))
    out_shape = [jax.ShapeDtypeStruct((2, s, b * FOURIER_WIDTH), BF16)] + qkv_shapes
    out_specs = [pl.BlockSpec((2, tm, FOURIER_WIDTH), lambda bi, si: (0, si, bi))] + qkv_specs
    in_specs = [
        pl.BlockSpec((1, tm, D_MODEL), lambda bi, si: (bi, si, 0)),
        pl.BlockSpec((1, D_MODEL), lambda bi, si: (0, 0)),
        pl.BlockSpec((D_MODEL, COL_GF), lambda bi, si: (0, 0)),
        pl.BlockSpec((FOURIER_WIDTH, FOURIER_WIDTH), lambda bi, si: (0, 0)),
        pl.BlockSpec((FOURIER_WIDTH, FOURIER_WIDTH), lambda bi, si: (0, 0)),
        pl.BlockSpec((tm, LANES), lambda bi, si: (si, 0)),
        pl.BlockSpec((tm, LANES), lambda bi, si: (si, 0)),
    ]
    return pl.pallas_call(
        _in_proj_kernel, grid=grid, in_specs=in_specs, out_specs=out_specs, out_shape=out_shape,
        scratch_shapes=[pltpu.VMEM((GROUP_WIDTH // LANES, tm, LANES), F32)],
        compiler_params=_params(("parallel", "parallel")), name="in_proj",
    )(x, gain, w, cc, sc, cos_t, sin_t)


SEQ_RADIX = 4


def _seq_dft_kernel(ab_ref, tab_ref, tw_ref, o_ref, u_ref):
    q_len = tab_ref.shape[0]
    for c in range(o_ref.shape[2] // LANES):
        cols = slice(c * LANES, (c + 1) * LANES)
        re = [ab_ref[0, j, :, cols].astype(F32) for j in range(SEQ_RADIX)]
        im = [-ab_ref[1, j, :, cols].astype(F32) for j in range(SEQ_RADIX)]
        p_re, p_im, q_re, q_im = re[0] + re[2], im[0] + im[2], re[0] - re[2], im[0] - im[2]
        s_re, s_im, d_re, d_im = re[1] + re[3], im[1] + im[3], re[1] - re[3], im[1] - im[3]
        terms = ((p_re + s_re, p_im + s_im), (q_re + d_im, q_im - d_re),
                 (p_re - s_re, p_im - s_im), (q_re - d_im, q_im + d_re))
        for r, (t_re, t_im) in enumerate(terms):
            if r > 0:
                cos, sin = tw_ref[r - 1, 0], tw_ref[r - 1, 1]
                t_re, t_im = t_re * cos + t_im * sin, t_im * cos - t_re * sin
            u_ref[r, :q_len, cols] = t_re.astype(BF16)
            u_ref[r, q_len:, cols] = t_im.astype(BF16)
    for r in range(SEQ_RADIX):
        o_ref[r] = jnp.dot(tab_ref[...], u_ref[r], preferred_element_type=F32).astype(o_ref.dtype)


def _seq_dft(ab, tab, tw, bn):
    _, s, n = ab.shape
    q_len = s // SEQ_RADIX
    bn = min(bn, n)
    return pl.pallas_call(
        _seq_dft_kernel, grid=(n // bn,),
        in_specs=[pl.BlockSpec((2, SEQ_RADIX, q_len, bn), lambda j: (0, 0, 0, j)),
                  pl.BlockSpec((q_len, 2 * q_len), lambda j: (0, 0), pipeline_mode=pl.Buffered(1)),
                  pl.BlockSpec((SEQ_RADIX - 1, 2, q_len, LANES), lambda j: (0, 0, 0, 0),
                               pipeline_mode=pl.Buffered(1))],
        out_specs=pl.BlockSpec((SEQ_RADIX, q_len, bn), lambda j: (0, 0, j)),
        out_shape=jax.ShapeDtypeStruct((SEQ_RADIX, q_len, n), BF16),
        scratch_shapes=[pltpu.VMEM((SEQ_RADIX, 2 * q_len, bn), BF16)],
        compiler_params=_params(("parallel",)), name="seq_dft",
    )(ab.reshape(2, SEQ_RADIX, q_len, n), tab, tw)


ATTN_UNROLL = 8


def _attn_kernel(q_ref, k_ref, v_ref, o_ref, lse_ref, *, qb, tk, seq_len, unroll):
    lane = lax.broadcasted_iota(jnp.int32, (1, GROUP_WIDTH), 1)
    qk_head = (lane % LANES) // (HEAD_DIM // 2)
    v_head = lane // HEAD_DIM
    col_minus_row = (lax.broadcasted_iota(jnp.int32, (qb, tk), 1)
                     - lax.broadcasted_iota(jnp.int32, (qb, tk), 0))
    n_res = q_ref.shape[1]
    n_sub = seq_len // qb

    def query_block(res, sub):
        m0 = pl.multiple_of(sub * qb, qb)
        start = pl.multiple_of(jnp.clip(m0 - HALF_SPAN, 0, seq_len - tk), 16)
        q = q_ref[0, res, pl.ds(m0, qb), :]
        k = k_ref[0, res, pl.ds(start, tk), :]
        v = v_ref[0, res, pl.ds(start, tk), :]
        valid = jnp.abs(col_minus_row + (start - m0)) <= HALF_SPAN
        q_heads = jnp.concatenate(
            [jnp.where(qk_head == hd, q, jnp.zeros_like(q)) for hd in range(HEADS_PER_GROUP)], axis=0)
        s = lax.dot_general(q_heads, k, (((1,), (1,)), ((), ())), preferred_element_type=F32)
        s = jnp.where(valid[None], s.reshape(HEADS_PER_GROUP, qb, tk), NEG_INF)
        m = jnp.max(s, axis=-1, keepdims=True)
        p = jnp.exp2(s - m)
        l = jnp.sum(p, axis=-1, keepdims=True)
        pv = jnp.dot(p.astype(BF16).reshape(HEADS_PER_GROUP * qb, tk), v, preferred_element_type=F32)
        pv = pv.reshape(HEADS_PER_GROUP, qb, GROUP_WIDTH) / l
        lse = (m + jnp.log2(l)) * LN2
        o_acc = pv[0]
        lse_acc = jnp.broadcast_to(lse[0], (qb, GROUP_WIDTH))
        for hd in range(1, HEADS_PER_GROUP):
            o_acc = jnp.where(v_head == hd, pv[hd], o_acc)
            lse_acc = jnp.where(v_head == hd, lse[hd], lse_acc)
        o_ref[0, res, pl.ds(m0, qb), :] = o_acc.astype(o_ref.dtype)
        lse_ref[0, res, pl.ds(m0, qb), :] = lse_acc

    def body(it, carry):
        for j in range(unroll):
            if n_sub % unroll == 0:
                per_res = n_sub // unroll
                query_block(it // per_res, (it % per_res) * unroll + j)
            else:
                query_block(it * (unroll // n_sub) + j // n_sub, j % n_sub)
        return carry

    lax.fori_loop(0, n_res * n_sub // unroll, body, 0)


def _attention(q, k, v):
    b, d, seq_len, _ = q.shape
    qb = min(128, seq_len)
    tk = min(qb + 2 * HALF_SPAN, seq_len)
    n_sub = seq_len // qb
    n_res = min(d, max(1, ATTN_UNROLL // n_sub))
    unroll = min(ATTN_UNROLL, n_res * n_sub)
    assert (n_res * n_sub) % unroll == 0 and (n_sub % unroll == 0 or unroll % n_sub == 0)
    kern = functools.partial(_attn_kernel, qb=qb, tk=tk, seq_len=seq_len, unroll=unroll)
    blk = pl.BlockSpec((1, n_res, seq_len, GROUP_WIDTH), lambda bi, ri: (bi, ri, 0, 0))
    shape = (b, d, seq_len, GROUP_WIDTH)
    return pl.pallas_call(
        kern, grid=(b, d // n_res), in_specs=[blk, blk, blk], out_specs=[blk, blk],
        out_shape=[jax.ShapeDtypeStruct(shape, BF16), jax.ShapeDtypeStruct(shape, F32)],
        compiler_params=_params(("parallel", "parallel")), name=f"attn_d{d}",
    )(q, k, v)


def _mix_out_kernel(*refs, routed):
    (x_ref, yf_ref, o0_ref, l0_ref, o1_ref, l1_ref, o2_ref, l2_ref, g1_ref, wg_ref,
     wf_ref, wa_ref, wo_ref, g2_ref) = refs[:14]
    if routed:
        rw_ref, rb_ref, xn_ref, h2_ref, cw_ref, sel_ref, sel_t_ref, o_scr, l_scr, f_scr = refs[14:]
    else:
        xn_ref, h2_ref, o_scr, l_scr, f_scr = refs[14:]
    tm = x_ref.shape[1]

    def gathered(ref, scr, d):
        if d == 1:
            return ref[0]
        n_chunk = ref.shape[-1] // LANES
        for r in range(d):
            for c in range(n_chunk):
                scr[c, pl.ds(r, tm // d, stride=d), :] = ref[r, :, c * LANES:(c + 1) * LANES].astype(F32)
        return jnp.concatenate([scr[c] for c in range(n_chunk)], axis=-1)

    o_refs, l_refs = (o0_ref, o1_ref, o2_ref), (l0_ref, l1_ref, l2_ref)
    os_, ls_ = [], []
    for g, d in enumerate(DILATIONS):
        os_.append(gathered(o_refs[g].at[0], o_scr.at[g], d))
        ls_.append(gathered(l_refs[g].at[0], l_scr.at[g], d))
    mx = jnp.maximum(jnp.maximum(ls_[0], ls_[1]), ls_[2])
    es = [jnp.exp(l - mx) for l in ls_]
    den = es[0] + es[1] + es[2]
    o_att = (es[0] * os_[0] + es[1] * os_[1] + es[2] * os_[2]) / den

    y_a = jnp.dot(o_att.astype(BF16), wa_ref[...], preferred_element_type=F32)
    y_mix = gathered(yf_ref, f_scr, SEQ_RADIX).astype(BF16)
    y_f = jnp.dot(y_mix, wf_ref[...], preferred_element_type=F32)
    h1 = _rms(x_ref[0], g1_ref[...]).astype(BF16)
    gate_f = jax.nn.sigmoid(jnp.dot(h1, wg_ref[:, :D_MODEL], preferred_element_type=F32))
    gate_a = jax.nn.sigmoid(jnp.dot(h1, wg_ref[:, D_MODEL:], preferred_element_type=F32))
    z = gate_f * y_f + gate_a * y_a
    xn = x_ref[0] +jnp.dot(z.astype(BF16), wo_ref[...], preferred_element_type=F32)
    xn_ref[0] = xn
    h2 = _rms(xn, g2_ref[...])
    h2_hi = h2.astype(BF16)
    h2_ref[0] = h2_hi

    if routed:
        h2_lo = (h2 - h2_hi.astype(F32)).astype(BF16)
        hi = jnp.dot(h2_hi, rw_ref[...], preferred_element_type=F32)
        logits = (hi[:, :LANES] + hi[:, LANES:]
                  + jnp.dot(h2_lo, rw_ref[:, :LANES], preferred_element_type=F32)) + rb_ref[...]
        lane = lax.broadcasted_iota(jnp.int32, logits.shape, 1)
        big = jnp.int32(LANES)
        m1 = jnp.max(logits, axis=-1, keepdims=True)
        i1 = jnp.min(jnp.where(logits == m1, lane, big), axis=-1, keepdims=True)
        rest = jnp.where(lane == i1, NEG_INF * 2, logits)
        m2 = jnp.max(rest, axis=-1, keepdims=True)
        i2 = jnp.min(jnp.where(rest == m2, lane, big), axis=-1, keepdims=True)
        e2 = jnp.exp(m2 - m1)
        w1 = 1.0 / (1.0 + e2)
        w2 = e2 / (1.0 + e2)
        cw_ref[...] = jnp.where(lane == i1, w1, jnp.where(lane == i2, w2, 0.0))
        sel = jnp.where((lane == i1) | (lane == i2), 1.0, 0.0)
        sel_ref[...] = sel
        sel_t_ref[...] = sel.T[:N_EXPERTS, :]


def _mix_out(x, yf, attn, g1, wg, wf, wa, wo, g2, router, tm):
    b, s, _ = x.shape
    routed = router is not None
    row = pl.BlockSpec((1, tm, D_MODEL), lambda bi, si: (bi, si, 0))
    full = lambda shape: pl.BlockSpec(shape, lambda bi, si: (0,) * len(shape), pipeline_mode=pl.Buffered(1))
    in_specs = [row, pl.BlockSpec((SEQ_RADIX, tm // SEQ_RADIX, FOURIER_WIDTH), lambda bi, si: (0, si, bi))]
    args = [x, yf]
    for d, (o, l) in zip(DILATIONS, attn):
        spec = pl.BlockSpec((1, d, tm // d, GROUP_WIDTH), lambda bi, si: (bi, 0, si, 0))
        in_specs += [spec, spec]
        args += [o, l]
    in_specs += [full((1, D_MODEL)), full((D_MODEL, 2 * D_MODEL)), full((FOURIER_WIDTH, D_MODEL)),
                 full((GROUP_WIDTH, D_MODEL)), full((D_MODEL, D_MODEL)), full((1, D_MODEL))]
    args += [g1, wg, wf, wa, wo, g2]
    out_shape = [jax.ShapeDtypeStruct((b, s, D_MODEL), F32), jax.ShapeDtypeStruct((b, s, D_MODEL), BF16)]
    out_specs = [row, row]
    if routed:
        in_specs += [full((D_MODEL, 2 * LANES)), full((1, LANES))]
        args += list(router)
        tok = pl.BlockSpec((tm, LANES), lambda bi, si: (bi * (s // tm) + si, 0))
        out_shape += [jax.ShapeDtypeStruct((b * s, LANES), F32)] * 2 + [jax.ShapeDtypeStruct((N_EXPERTS, b * s), F32)]
        out_specs += [tok, tok, pl.BlockSpec((N_EXPERTS, tm), lambda bi, si: (0, bi * (s // tm) + si))]
    return pl.pallas_call(
        functools.partial(_mix_out_kernel, routed=routed), grid=(b, s // tm),
        in_specs=in_specs, out_specs=out_specs, out_shape=out_shape,
        scratch_shapes=[pltpu.VMEM((len(DILATIONS), GROUP_WIDTH // LANES, tm, LANES), F32)] * 2
        + [pltpu.VMEM((FOURIER_WIDTH // LANES, tm, LANES), F32)],
        compiler_params=_params(("parallel", "parallel")), name="mix_out",
    )(*args)


def _swiglu(h, w1, w3, w2):
    a = jnp.dot(h, w1, preferred_element_type=F32)
    g = jnp.dot(h, w3, preferred_element_type=F32)
    act = (a * jax.nn.sigmoid(a) * g).astype(BF16)
    return jnp.dot(act, w2, preferred_element_type=F32)


def _ffn_kernel(h_ref, x_ref, w1_ref, w3_ref, w2_ref, gfin_ref, *rest, final_norm):
    o_ref = rest[len(rest) // 2]
    f = pl.program_id(1)

    @pl.when(f == 0)
    def _():
        o_ref[...] = x_ref[...]

    o_ref[...] += _swiglu(h_ref[...], w1_ref[...], w3_ref[...], w2_ref[...])

    if final_norm:
        @pl.when(f == pl.num_programs(1) - 1)
        def _():
            o_ref[...] = _rms(o_ref[...], gfin_ref[...])

    n_cast = len(rest) // 2
    for src, dst in zip(rest[:n_cast], rest[n_cast + 1:]):
        dst[...] = src[...].astype(dst.dtype)


def _ffn(h, x, w1, w3, w2, gfin, final_norm, tm, tf, cast_cols=(), cast_rows=()):
    n = h.shape[0]
    grid = (n // tm, FFN_DIM // tf)
    row = pl.BlockSpec((tm, D_MODEL), lambda i, f: (i, 0))
    in_specs = [row, row,
                pl.BlockSpec((D_MODEL, tf), lambda i, f: (0, f)),
                pl.BlockSpec((D_MODEL, tf), lambda i, f: (0, f)),
                pl.BlockSpec((tf, D_MODEL), lambda i, f: (f, 0)),
                pl.BlockSpec((1, D_MODEL), lambda i, f: (0, 0))]
    cast_specs, cast_shapes = [], []
    for arr, axis in [(a, 2) for a in cast_cols] + [(a, 1) for a in cast_rows]:
        per_expert = arr.shape[axis] // LANES
        assert arr.shape[0] * per_expert == grid[0] * grid[1], "one cast block per grid step"
        shape = (1, arr.shape[1], LANES) if axis == 2 else (1, LANES, arr.shape[2])

        def index(i, f, per_expert=per_expert, axis=axis):
            t = i * grid[1] + f
            return (t // per_expert, 0, t % per_expert) if axis == 2 else (t // per_expert, t % per_expert, 0)

        cast_specs.append(pl.BlockSpec(shape, index))
        cast_shapes.append(jax.ShapeDtypeStruct(arr.shape, BF16))
    outs = pl.pallas_call(
        functools.partial(_ffn_kernel, final_norm=final_norm),
        grid=grid, in_specs=in_specs + cast_specs, out_specs=[row] + cast_specs,
        out_shape=[jax.ShapeDtypeStruct((n, D_MODEL), F32)] + cast_shapes,
        compiler_params=_params(("arbitrary", "arbitrary")), name="ffn",
    )(h, x, w1, w3, w2, gfin, *cast_cols, *cast_rows)
    return outs[0], outs[1:]


MOE_SUB = 256
MOE_WIN = 128
MOE_ROWS = 128
ROW_ALIGN = 16


def _moe_kernel(h_ref, x_hbm, sel_t_ref, sel_ref, cw_ref, w1_ref, w3_ref, w2_ref, triu_ref, tril_ref, gfin_ref,
                o_ref, pos_t_ref, pos_ref, hs_ref, y_ref, pre_ref, x_sem, *, final_norm):
    tb = h_ref.shape[0]
    n_sub = tb // MOE_SUB
    blk, e, f = pl.program_id(0), pl.program_id(1), pl.program_id(2)
    n_e, n_f = pl.num_programs(1), pl.num_programs(2)

    @pl.when((blk == 0) & (e == 0) & (f == 0))
    def _():
        hs_ref[...] = jnp.zeros_like(hs_ref)
        y_ref[...] = jnp.zeros_like(y_ref)

    @pl.when((e == 0) & (f == 0))
    def _():
        x_copy = pltpu.make_async_copy(x_hbm.at[pl.ds(pl.multiple_of(blk * tb, tb), tb), :], o_ref, x_sem)
        x_copy.start()
        off_t = jnp.zeros((N_EXPERTS, 1), F32)
        off = jnp.zeros((1, LANES), F32)
        for s in range(n_sub):
            sl = slice(s * MOE_SUB, (s + 1) * MOE_SUB)
            sel_t = sel_t_ref[:, sl]
            within_t = jnp.dot(sel_t.astype(BF16), triu_ref[...], preferred_element_type=F32)
            pos_t_ref[:, sl] = jnp.where(sel_t > 0.0, within_t + off_t, -1.0)
            off_t = off_t + jnp.sum(sel_t, axis=1, keepdims=True)
            sel = sel_ref[sl, :]
            within = jnp.dot(tril_ref[...], sel.astype(BF16), preferred_element_type=F32)
            pos_ref[sl, :] = jnp.where(sel > 0.0, within + off, -1.0)
            off = off + jnp.sum(sel, axis=0, keepdims=True)
        x_copy.wait()

    def window(s):
        p0, p1 = pre_ref[s], pre_ref[s + 1]
        base = (p0 // ROW_ALIGN) * ROW_ALIGN
        return p0, base, (p1 - base + MOE_WIN - 1) // MOE_WIN

    @pl.when(f == 0)
    def _():
        run = jnp.int32(0)
        pre_ref[0] = run
        for s in range(n_sub):
            cnt = jnp.sum(sel_t_ref[pl.ds(e, 1), s * MOE_SUB:(s + 1) * MOE_SUB])
            run = run + cnt.astype(jnp.int32)
            pre_ref[s + 1] = run
        row = lax.broadcasted_iota(jnp.int32, (MOE_WIN, MOE_SUB), 0)
        row1 = lax.broadcasted_iota(jnp.int32, (MOE_WIN, 1), 0)

        def gather(s, j):
            p0, p1 = pre_ref[s], pre_ref[s + 1]
            _, base, _ = window(s)
            sl = slice(s * MOE_SUB, (s + 1) * MOE_SUB)
            r0 = pl.multiple_of(base + j * MOE_WIN, ROW_ALIGN)
            pos_i = pos_t_ref[pl.ds(e, 1), sl].astype(jnp.int32)
            one_hot = jnp.where(row == pos_i - r0, 1.0, 0.0).astype(BF16)
            rows = jnp.dot(one_hot, h_ref[sl, :], preferred_element_type=F32).astype(BF16)
            mine = (row1 >= p0 - r0) & (row1 < p1 - r0)
            hs_ref[pl.ds(r0, MOE_WIN), :] = jnp.where(mine, rows, hs_ref[pl.ds(r0, MOE_WIN), :])

        for s in range(n_sub):
            gather(s, 0)
        for s in range(n_sub):
            lax.fori_loop(1, window(s)[2], lambda j, c, s=s: (gather(s, j), c)[1], 0)

    n_units = (pre_ref[n_sub] + MOE_ROWS - 1) // MOE_ROWS
    n_full = n_units // 4
    rem = n_units % 4

    def expert_rows(r0, n_rows):
        y = _swiglu(hs_ref[pl.ds(r0, n_rows), :], w1_ref[0], w3_ref[0], w2_ref[0])

        @pl.when(f == 0)
        def _():
            y_ref[pl.ds(r0, n_rows), :] = y

        @pl.when(f > 0)
        def _():
            y_ref[pl.ds(r0, n_rows), :] += y

    def full_chunk(c, carry):
        expert_rows(pl.multiple_of(c * (4 * MOE_ROWS), 4 * MOE_ROWS), 4 * MOE_ROWS)
        return carry

    lax.fori_loop(0, n_full, full_chunk, 0)

    @pl.when(rem >= 2)
    def _():
        expert_rows(pl.multiple_of(n_full * (4 * MOE_ROWS), 2 * MOE_ROWS), 2 * MOE_ROWS)

    @pl.when(rem % 2 == 1)
    def _():
        expert_rows(pl.multiple_of((n_units - 1) * MOE_ROWS, MOE_ROWS), MOE_ROWS)

    @pl.when(f == n_f - 1)
    def _():
        lane = lax.broadcasted_iota(jnp.int32, (MOE_SUB, LANES), 1)

        def scatter(s, j):
            _, base, _ = window(s)
            sl = slice(s * MOE_SUB, (s + 1) * MOE_SUB)
            r0 = pl.multiple_of(base + j * MOE_WIN, ROW_ALIGN)
            mine = lane == e
            pos_i = jnp.sum(jnp.where(mine, pos_ref[sl, :], 0.0), axis=1, keepdims=True).astype(jnp.int32)
            weight = jnp.sum(jnp.where(mine, cw_ref[sl, :], 0.0), axis=1, keepdims=True)
            one_hot = jnp.where(lane == pos_i - r0, 1.0, 0.0).astype(BF16)
            rows = y_ref[pl.ds(r0, MOE_WIN), :].astype(BF16)
            back = jnp.dot(one_hot, rows, preferred_element_type=F32)
            o_ref[sl, :] += back * weight

        for s in range(n_sub):
            scatter(s, 0)
        for s in range(n_sub):
            lax.fori_loop(1, window(s)[2], lambda j, c, s=s: (scatter(s, j), c)[1], 0)

        if final_norm:
            @pl.when(e == n_e - 1)
            def _():
                o_ref[...] = _rms(o_ref[...], gfin_ref[...])


def _moe(h, x, sel_t, sel, cw, w1, w3, w2, gfin, final_norm, tb, tf):
    n = h.shape[0]
    tb = min(tb, n)
    n_exp = w1.shape[0]
    triu = jnp.triu(jnp.ones((MOE_SUB, MOE_SUB), F32), k=1).astype(BF16)
    pad = 2 * MOE_WIN
    once = pl.Buffered(1)
    tok = pl.BlockSpec((tb, LANES), lambda i, e, f: (i, 0), pipeline_mode=once)
    tri = pl.BlockSpec((MOE_SUB, MOE_SUB), lambda i, e, f: (0, 0), pipeline_mode=once)
    in_specs = [pl.BlockSpec((tb, D_MODEL), lambda i, e, f: (i, 0), pipeline_mode=once),
                pl.BlockSpec(memory_space=pl.ANY),
                pl.BlockSpec((N_EXPERTS, tb), lambda i, e, f: (0, i), pipeline_mode=once),
                tok, tok,
                pl.BlockSpec((1, D_MODEL, tf), lambda i, e, f: (e, 0, f)),
                pl.BlockSpec((1, D_MODEL, tf), lambda i, e, f: (e, 0, f)),
                pl.BlockSpec((1, tf, D_MODEL), lambda i, e, f: (e, f, 0)),
                tri, tri,
                pl.BlockSpec((1, D_MODEL), lambda i, e, f: (0, 0), pipeline_mode=once)]
    return pl.pallas_call(
        functools.partial(_moe_kernel, final_norm=final_norm),
        grid=(n // tb, n_exp, FFN_DIM // tf), in_specs=in_specs,
        out_specs=pl.BlockSpec((tb, D_MODEL), lambda i, e, f: (i, 0), pipeline_mode=once),
        out_shape=jax.ShapeDtypeStruct((n, D_MODEL), F32),
        scratch_shapes=[pltpu.VMEM((N_EXPERTS, tb), F32),
                        pltpu.VMEM((tb, LANES), F32),
                        pltpu.VMEM((tb + pad, D_MODEL), BF16),
                        pltpu.VMEM((tb + pad, D_MODEL), F32),
                        pltpu.SMEM((tb // MOE_SUB + 1,), jnp.int32),
                        pltpu.SemaphoreType.DMA(())],
        compiler_params=_params(("arbitrary", "arbitrary", "arbitrary"), MOE_VMEM_LIMIT), name="moe",
    )(h, x, sel_t, sel, cw, w1, w3, w2, triu, triu.T, gfin)


def _qk_column_order():
    half = HEAD_DIM // 2
    order = []
    for g in range(len(DILATIONS)):
        base = g * GROUP_WIDTH
        for part in range(2):
            for hd in range(HEADS_PER_GROUP):
                order.extend(range(base + hd * HEAD_DIM + part * half, base + hd * HEAD_DIM + (part + 1) * half))
    return np.asarray(order, np.int32)


def _dft_tables(n):
    j = jnp.arange(n, dtype=jnp.int32)
    phase = (j[:, None] * j[None, :]) % n
    ang = phase.astype(F32) * (2.0 * np.pi / n)
    return jnp.cos(ang), jnp.sin(ang)


def _tables(s):
    half = HEAD_DIM // 2
    inv_freq = ROPE_THETA ** (-jnp.arange(half, dtype=F32) * 2.0 / HEAD_DIM)
    ang = jnp.arange(s, dtype=F32)[:, None] * inv_freq[None, :]
    cos_t = jnp.tile(jnp.cos(ang), (1, LANES // half))
    sin_t = jnp.tile(jnp.sin(ang), (1, LANES // half))
    cg, sg = _dft_tables(FOURIER_GROUP_DIM)
    eye = jnp.eye(FOURIER_GROUPS, dtype=F32)
    scale_c = FOURIER_GROUP_DIM ** -0.5
    cc = (jnp.kron(eye, cg) * scale_c).astype(BF16)
    sc = (jnp.kron(eye, sg) * scale_c).astype(BF16)
    q_len = s // SEQ_RADIX
    cq, sq = _dft_tables(q_len)
    seq_tab = (jnp.concatenate([cq, sq], axis=1) * s ** -0.5).astype(BF16)
    pos = jnp.arange(q_len, dtype=jnp.int32)[None, :]
    ang = ((jnp.arange(1, SEQ_RADIX, dtype=jnp.int32)[:, None] * pos) % s).astype(F32) * (2.0 * np.pi / s)
    seq_tw = jnp.broadcast_to(jnp.stack([jnp.cos(ang), jnp.sin(ang)], axis=1)[..., None],
                              (SEQ_RADIX - 1, 2, q_len, LANES))
    return cos_t, sin_t, cc, sc, seq_tab, seq_tw


def kernel(x, ln_mix, w_in, w_fourier, w_attn, w_out, ln_ffn, dense_w1, dense_w3, dense_w2,
           router_w, router_b, moe_w1, moe_w3, moe_w2, ln_final):
    b, s, _ = x.shape
    depth = w_in.shape[0]
    n = b * s
    tm = 512
    cos_t, sin_t, cc, sc, seq_tab, seq_tw = _tables(s)
    order = _qk_column_order()
    gfin = ln_final.reshape(1, D_MODEL)
    expert_bf16 = []

    for layer in range(depth):
        w = w_in[layer]
        wq = w[:, COL_Q:COL_K][:, order]
        wk = w[:, COL_K:COL_V][:, order]
        w_l = jnp.concatenate([w[:, :COL_Q], wq, wk, w[:, COL_V:COL_GF]], axis=1).astype(BF16)
        w_gate = w[:, COL_GF:].astype(BF16)
        g1 = ln_mix[layer].reshape(1, D_MODEL)

        outs = _in_proj(x, ln_mix[layer].reshape(1, D_MODEL), w_l, cc, sc, cos_t, sin_t, min(1024, s))
        ab, qkv = outs[0], outs[1:10]
        yf = _seq_dft(ab, seq_tab, seq_tw, 256)
        attn = [_attention(qkv[3 * g], qkv[3 * g + 1], qkv[3 * g + 2]) for g in range(len(DILATIONS))]

        last = layer == depth - 1
        i = layer // 2
        if layer % 2 == 0:
            router = None
        else:
            rw = jnp.zeros((D_MODEL, LANES), F32).at[:, :N_EXPERTS].set(router_w[i])
            rw_hi = rw.astype(BF16)
            rw_lo = (rw - rw_hi.astype(F32)).astype(BF16)
            rb = jnp.full((1, LANES), NEG_INF, F32).at[0, :N_EXPERTS].set(router_b[i])
            router = (jnp.concatenate([rw_hi, rw_lo], axis=1), rb)
        res = _mix_out(x, yf, attn, g1, w_gate, w_fourier[layer].astype(BF16), w_attn[layer].astype(BF16),
                       w_out[layer].astype(BF16), ln_ffn[layer].reshape(1, D_MODEL), router, tm)
        xn, h2 = res[0].reshape(n, D_MODEL), res[1].reshape(n, D_MODEL)
        if layer % 2 == 0:
            ffn_tm, ffn_tf = 1024, 512
            ride = (not last and
                    (n // ffn_tm) * (FFN_DIM // ffn_tf) == N_EXPERTS * (FFN_DIM // LANES))
            y, expert_bf16 = _ffn(h2, xn, dense_w1[i].astype(BF16), dense_w3[i].astype(BF16),
                                  dense_w2[i].astype(BF16), gfin, last, ffn_tm, ffn_tf,
                                  (moe_w1[i], moe_w3[i]) if ride else (), (moe_w2[i],) if ride else ())
        else:
            if not expert_bf16:
                expert_bf16 = [w[i].astype(BF16) for w in (moe_w1, moe_w3, moe_w2)]
            y = _moe(h2, xn, res[4], res[3], res[2], *expert_bf16, gfin, last, 2048, 1792)
        x = y.reshape(b, s, D_MODEL)
    return x
```

```python
import functools

import jax
import jax.numpy as jnp
import numpy as np
from jax import lax
from jax.experimental import pallas as pl
from jax.experimental.pallas import tpu as pltpu

D_MODEL = 1024
FOURIER_GROUPS = 4
FOURIER_GROUP_DIM = 128
FOURIER_WIDTH = FOURIER_GROUPS * FOURIER_GROUP_DIM
DILATIONS = (1, 4, 16)
HALF_SPAN = 64
HEADS_PER_GROUP = 4
HEAD_DIM = 64
GROUP_WIDTH = HEADS_PER_GROUP * HEAD_DIM
ATTN_WIDTH = len(DILATIONS) * GROUP_WIDTH
ROPE_THETA = 10000.0
FFN_DIM = 3584
N_EXPERTS = 8
RMS_EPS = 1e-6
NEG_INF = -1e30
LOG2E = float(np.log2(np.e))
LN2 = float(np.log(2.0))
LANES = 128

COL_UF = 0
COL_Q = FOURIER_WIDTH
COL_K = COL_Q + ATTN_WIDTH
COL_V = COL_K + ATTN_WIDTH
COL_GF = COL_V + ATTN_WIDTH
COL_GA = COL_GF + D_MODEL
IN_COLS = COL_GA + D_MODEL

VMEM_LIMIT = 56 * 1024 * 1024
MOE_VMEM_LIMIT = 60 * 1024 * 1024

BF16 = jnp.bfloat16
F32 = jnp.float32


def _params(semantics, vmem_limit=VMEM_LIMIT):
    return pltpu.CompilerParams(dimension_semantics=semantics, vmem_limit_bytes=vmem_limit)


def _rms(x, gain):
    ms = jnp.mean(x * x, axis=-1, keepdims=True)
    return x * lax.rsqrt(ms + RMS_EPS) * gain


def _in_proj_kernel(x_ref, g_ref, w_ref, cc_ref, sc_ref, cos_ref, sin_ref,
                    ab_ref, q0_ref, k0_ref, v0_ref, q1_ref, k1_ref, v1_ref, q2_ref, k2_ref, v2_ref,
                    scr_ref):
    tm = x_ref.shape[1]
    h = _rms(x_ref[0], g_ref[...]).astype(BF16)

    def proj(col, width):
        return jnp.dot(h, w_ref[:, col:col + width], preferred_element_type=F32)

    uf = proj(COL_UF, FOURIER_WIDTH).astype(BF16)
    ab_ref[0] = jnp.dot(uf, cc_ref[...], preferred_element_type=F32).astype(BF16)
    ab_ref[1] = jnp.dot(uf, sc_ref[...], preferred_element_type=F32).astype(BF16)

    cos = cos_ref[...]
    sin = sin_ref[...]

    def rope(t):
        t1, t2 = t[:, :LANES], t[:, LANES:]
        return jnp.concatenate([t1 * cos - t2 * sin, t2 * cos + t1 * sin], axis=-1)

    def put(ref, val, d):
        if d == 1:
            ref[0, 0] = val.astype(BF16)
            return
        for c in range(GROUP_WIDTH // LANES):
            scr_ref[c] = val[:, c * LANES:(c + 1) * LANES]
        for r in range(d):
            for c in range(GROUP_WIDTH // LANES):
                ref[0, r, :, c * LANES:(c + 1) * LANES] = scr_ref[c, pl.ds(r, tm // d, stride=d), :].astype(BF16)

    outs = ((q0_ref, k0_ref, v0_ref), (q1_ref, k1_ref, v1_ref), (q2_ref, k2_ref, v2_ref))
    for g, d in enumerate(DILATIONS):
        qr, kr, vr = outs[g]
        put(qr, rope(proj(COL_Q + g * GROUP_WIDTH, GROUP_WIDTH)) * (HEAD_DIM ** -0.5 * LOG2E), d)
        put(kr, rope(proj(COL_K + g * GROUP_WIDTH, GROUP_WIDTH)), d)
        put(vr, proj(COL_V + g * GROUP_WIDTH, GROUP_WIDTH), d)


def _in_proj(x, gain, w, cc, sc, cos_t, sin_t, tm):
    b, s, _ = x.shape
    grid = (b, s // tm)
    qkv_shapes, qkv_specs = [], []
    for d in DILATIONS:
        for _ in range(3):
            qkv_shapes.append(jax.ShapeDtypeStruct((b, d, s // d, GROUP_WIDTH), BF16))
            qkv_specs.append(pl.BlockSpec((1, d, tm // d, GROUP_WIDTH), lambda bi, si: (bi, 0, si, 0)))
    out_shape = [jax.ShapeDtypeStruct((2, s, b * FOURIER_WIDTH), BF16)] + qkv_shapes
    out_specs = [pl.BlockSpec((2, tm, FOURIER_WIDTH), lambda bi, si: (0, si, bi))] + qkv_specs
    in_specs = [
        pl.BlockSpec((1, tm, D_MODEL), lambda bi, si: (bi, si, 0)),
        pl.BlockSpec((1, D_MODEL), lambda bi, si: (0, 0)),
        pl.BlockSpec((D_MODEL, COL_GF), lambda bi, si: (0, 0)),
        pl.BlockSpec((FOURIER_WIDTH, FOURIER_WIDTH), lambda bi, si: (0, 0)),
        pl.BlockSpec((FOURIER_WIDTH, FOURIER_WIDTH), lambda bi, si: (0, 0)),
        pl.BlockSpec((tm, LANES), lambda bi, si: (si, 0)),
        pl.BlockSpec((tm, LANES), lambda bi, si: (si, 0)),
    ]
    return pl.pallas_call(
        _in_proj_kernel, grid=grid, in_specs=in_specs, out_specs=out_specs, out_shape=out_shape,
        scratch_shapes=[pltpu.VMEM((GROUP_WIDTH // LANES, tm, LANES), F32)],
        compiler_params=_params(("parallel", "parallel")), name="in_proj",
    )(x, gain, w, cc, sc, cos_t, sin_t)


SEQ_RADIX = 4


def _seq_dft_kernel(ab_ref, tab_ref, tw_ref, o_ref, u_ref):
    q_len = tab_ref.shape[0]
    for c in range(o_ref.shape[2] // LANES):
        cols = slice(c * LANES, (c + 1) * LANES)
        re = [ab_ref[0, j, :, cols].astype(F32) for j in range(SEQ_RADIX)]
        im = [-ab_ref[1, j, :, cols].astype(F32) for j in range(SEQ_RADIX)]
        p_re, p_im, q_re, q_im = re[0] + re[2], im[0] + im[2], re[0] - re[2], im[0] - im[2]
        s_re, s_im, d_re, d_im = re[1] + re[3], im[1] + im[3], re[1] - re[3], im[1] - im[3]
        terms = ((p_re + s_re, p_im + s_im), (q_re + d_im, q_im - d_re),
                 (p_re - s_re, p_im - s_im), (q_re - d_im, q_im + d_re))
        for r, (t_re, t_im) in enumerate(terms):
            if r > 0:
                cos, sin = tw_ref[r - 1, 0], tw_ref[r - 1, 1]
                t_re, t_im = t_re * cos + t_im * sin, t_im * cos - t_re * sin
            u_ref[r, :q_len, cols] = t_re.astype(BF16)
            u_ref[r, q_len:, cols] = t_im.astype(BF16)
    for r in range(SEQ_RADIX):
        o_ref[r] = jnp.dot(tab_ref[...], u_ref[r], preferred_element_type=F32).astype(o_ref.dtype)


def _seq_dft(ab, tab, tw, bn):
    _, s, n = ab.shape
    q_len = s // SEQ_RADIX
    bn = min(bn, n)
    return pl.pallas_call(
        _seq_dft_kernel, grid=(n // bn,),
        in_specs=[pl.BlockSpec((2, SEQ_RADIX, q_len, bn), lambda j: (0, 0, 0, j)),
                  pl.BlockSpec((q_len, 2 * q_len), lambda j: (0, 0), pipeline_mode=pl.Buffered(1)),
                  pl.BlockSpec((SEQ_RADIX - 1, 2, q_len, LANES), lambda j: (0, 0, 0, 0),
                               pipeline_mode=pl.Buffered(1))],
        out_specs=pl.BlockSpec((SEQ_RADIX, q_len, bn), lambda j: (0, 0, j)),
        out_shape=jax.ShapeDtypeStruct((SEQ_RADIX, q_len, n), BF16),
        scratch_shapes=[pltpu.VMEM((SEQ_RADIX, 2 * q_len, bn), BF16)],
        compiler_params=_params(("parallel",)), name="seq_dft",
    )(ab.reshape(2, SEQ_RADIX, q_len, n), tab, tw)


ATTN_UNROLL = 8


def _attn_kernel(q_ref, k_ref, v_ref, o_ref, lse_ref, *, qb, tk, seq_len, unroll):
    lane = lax.broadcasted_iota(jnp.int32, (1, GROUP_WIDTH), 1)
    qk_head = (lane % LANES) // (HEAD_DIM // 2)
    v_head = lane // HEAD_DIM
    col_minus_row = (lax.broadcasted_iota(jnp.int32, (qb, tk), 1)
                     - lax.broadcasted_iota(jnp.int32, (qb, tk), 0))
    n_res = q_ref.shape[1]
    n_sub = seq_len // qb

    def query_block(res, sub):
        m0 = pl.multiple_of(sub * qb, qb)
        start = pl.multiple_of(jnp.clip(m0 - HALF_SPAN, 0, seq_len - tk), 16)
        q = q_ref[0, res, pl.ds(m0, qb), :]
        k = k_ref[0, res, pl.ds(start, tk), :]
        v = v_ref[0, res, pl.ds(start, tk), :]
        valid = jnp.abs(col_minus_row + (start - m0)) <= HALF_SPAN
        q_heads = jnp.concatenate(
            [jnp.where(qk_head == hd, q, jnp.zeros_like(q)) for hd in range(HEADS_PER_GROUP)], axis=0)
        s = lax.dot_general(q_heads, k, (((1,), (1,)), ((), ())), preferred_element_type=F32)
        s = jnp.where(valid[None], s.reshape(HEADS_PER_GROUP, qb, tk), NEG_INF)
        m = jnp.max(s, axis=-1, keepdims=True)
        p = jnp.exp2(s - m)
        l = jnp.sum(p, axis=-1, keepdims=True)
        pv = jnp.dot(p.astype(BF16).reshape(HEADS_PER_GROUP * qb, tk), v, preferred_element_type=F32)
        pv = pv.reshape(HEADS_PER_GROUP, qb, GROUP_WIDTH) / l
        lse = (m + jnp.log2(l)) * LN2
        o_acc = pv[0]
        lse_acc = jnp.broadcast_to(lse[0], (qb, GROUP_WIDTH))
        for hd in range(1, HEADS_PER_GROUP):
            o_acc = jnp.where(v_head == hd, pv[hd], o_acc)
            lse_acc = jnp.where(v_head == hd, lse[hd], lse_acc)
        o_ref[0, res, pl.ds(m0, qb), :] = o_acc.astype(o_ref.dtype)
        lse_ref[0, res, pl.ds(m0, qb), :] = lse_acc

    def body(it, carry):
        for j in range(unroll):
            if n_sub % unroll == 0:
                per_res = n_sub // unroll
                query_block(it // per_res, (it % per_res) * unroll + j)
            else:
                query_block(it * (unroll // n_sub) + j // n_sub, j % n_sub)
        return carry

    lax.fori_loop(0, n_res * n_sub // unroll, body, 0)


def _attention(q, k, v):
    b, d, seq_len, _ = q.shape
    qb = min(128, seq_len)
    tk = min(qb + 2 * HALF_SPAN, seq_len)
    n_sub = seq_len // qb
    n_res = min(d, max(1, ATTN_UNROLL // n_sub))
    unroll = min(ATTN_UNROLL, n_res * n_sub)
    assert (n_res * n_sub) % unroll == 0 and (n_sub % unroll == 0 or unroll % n_sub == 0)
    kern = functools.partial(_attn_kernel, qb=qb, tk=tk, seq_len=seq_len, unroll=unroll)
    blk = pl.BlockSpec((1, n_res, seq_len, GROUP_WIDTH), lambda bi, ri: (bi, ri, 0, 0))
    shape = (b, d, seq_len, GROUP_WIDTH)
    return pl.pallas_call(
        kern, grid=(b, d // n_res), in_specs=[blk, blk, blk], out_specs=[blk, blk],
        out_shape=[jax.ShapeDtypeStruct(shape, BF16), jax.ShapeDtypeStruct(shape, F32)],
        compiler_params=_params(("parallel", "parallel")), name=f"attn_d{d}",
    )(q, k, v)


def _mix_out_kernel(*refs, routed):
    (x_ref, yf_ref, o0_ref, l0_ref, o1_ref, l1_ref, o2_ref, l2_ref, g1_ref, wg_ref,
     wf_ref, wa_ref, wo_ref, g2_ref) = refs[:14]
    if routed:
        rw_ref, rb_ref, xn_ref, h2_ref, cw_ref, sel_ref, sel_t_ref, o_scr, l_scr, f_scr = refs[14:]
    else:
        xn_ref, h2_ref, o_scr, l_scr, f_scr = refs[14:]
    tm = x_ref.shape[1]

    def gathered(ref, scr, d):
        if d == 1:
            return ref[0]
        n_chunk = ref.shape[-1] // LANES
        for r in range(d):
            for c in range(n_chunk):
                scr[c, pl.ds(r, tm // d, stride=d), :] = ref[r, :, c * LANES:(c + 1) * LANES].astype(F32)
        return jnp.concatenate([scr[c] for c in range(n_chunk)], axis=-1)

    o_refs, l_refs = (o0_ref, o1_ref, o2_ref), (l0_ref, l1_ref, l2_ref)
    os_, ls_ = [], []
    for g, d in enumerate(DILATIONS):
        os_.append(gathered(o_refs[g].at[0], o_scr.at[g], d))
        ls_.append(gathered(l_refs[g].at[0], l_scr.at[g], d))
    mx = jnp.maximum(jnp.maximum(ls_[0], ls_[1]), ls_[2])
    es = [jnp.exp(l - mx) for l in ls_]
    den = es[0] + es[1] + es[2]
    o_att = (es[0] * os_[0] + es[1] * os_[1] + es[2] * os_[2]) / den

    y_a = jnp.dot(o_att.astype(BF16), wa_ref[...], preferred_element_type=F32)
    y_mix = gathered(yf_ref, f_scr, SEQ_RADIX).astype(BF16)
    y_f = jnp.dot(y_mix, wf_ref[...], preferred_element_type=F32)
    h1 = _rms(x_ref[0], g1_ref[...]).astype(BF16)
    gate_f = jax.nn.sigmoid(jnp.dot(h1, wg_ref[:, :D_MODEL], preferred_element_type=F32))
    gate_a = jax.nn.sigmoid(jnp.dot(h1, wg_ref[:, D_MODEL:], preferred_element_type=F32))
    z = gate_f * y_f + gate_a * y_a
    xn = x_ref[0] +jnp.dot(z.astype(BF16), wo_ref[...], preferred_element_type=F32)
    xn_ref[0] = xn
    h2 = _rms(xn, g2_ref[...])
    h2_hi = h2.astype(BF16)
    h2_ref[0] = h2_hi

    if routed:
        h2_lo = (h2 - h2_hi.astype(F32)).astype(BF16)
        hi = jnp.dot(h2_hi, rw_ref[...], preferred_element_type=F32)
        logits = (hi[:, :LANES] + hi[:, LANES:]
                  + jnp.dot(h2_lo, rw_ref[:, :LANES], preferred_element_type=F32)) + rb_ref[...]
        lane = lax.broadcasted_iota(jnp.int32, logits.shape, 1)
        big = jnp.int32(LANES)
        m1 = jnp.max(logits, axis=-1, keepdims=True)
        i1 = jnp.min(jnp.where(logits == m1, lane, big), axis=-1, keepdims=True)
        rest = jnp.where(lane == i1, NEG_INF * 2, logits)
        m2 = jnp.max(rest, axis=-1, keepdims=True)
        i2 = jnp.min(jnp.where(rest == m2, lane, big), axis=-1, keepdims=True)
        e2 = jnp.exp(m2 - m1)
        w1 = 1.0 / (1.0 + e2)
        w2 = e2 / (1.0 + e2)
        cw_ref[...] = jnp.where(lane == i1, w1, jnp.where(lane == i2, w2, 0.0))
        sel = jnp.where((lane == i1) | (lane == i2), 1.0, 0.0)
        sel_ref[...] = sel
        sel_t_ref[...] = sel.T[:N_EXPERTS, :]


def _mix_out(x, yf, attn, g1, wg, wf, wa, wo, g2, router, tm):
    b, s, _ = x.shape
    routed = router is not None
    row = pl.BlockSpec((1, tm, D_MODEL), lambda bi, si: (bi, si, 0))
    full = lambda shape: pl.BlockSpec(shape, lambda bi, si: (0,) * len(shape), pipeline_mode=pl.Buffered(1))
    in_specs = [row, pl.BlockSpec((SEQ_RADIX, tm // SEQ_RADIX, FOURIER_WIDTH), lambda bi, si: (0, si, bi))]
    args = [x, yf]
    for d, (o, l) in zip(DILATIONS, attn):
        spec = pl.BlockSpec((1, d, tm // d, GROUP_WIDTH), lambda bi, si: (bi, 0, si, 0))
        in_specs += [spec, spec]
        args += [o, l]
    in_specs += [full((1, D_MODEL)), full((D_MODEL, 2 * D_MODEL)), full((FOURIER_WIDTH, D_MODEL)),
                 full((GROUP_WIDTH, D_MODEL)), full((D_MODEL, D_MODEL)), full((1, D_MODEL))]
    args += [g1, wg, wf, wa, wo, g2]
    out_shape = [jax.ShapeDtypeStruct((b, s, D_MODEL), F32), jax.ShapeDtypeStruct((b, s, D_MODEL), BF16)]
    out_specs = [row, row]
    if routed:
        in_specs += [full((D_MODEL, 2 * LANES)), full((1, LANES))]
        args += list(router)
        tok = pl.BlockSpec((tm, LANES), lambda bi, si: (bi * (s // tm) + si, 0))
        out_shape += [jax.ShapeDtypeStruct((b * s, LANES), F32)] * 2 + [jax.ShapeDtypeStruct((N_EXPERTS, b * s), F32)]
        out_specs += [tok, tok, pl.BlockSpec((N_EXPERTS, tm), lambda bi, si: (0, bi * (s // tm) + si))]
    return pl.pallas_call(
        functools.partial(_mix_out_kernel, routed=routed), grid=(b, s // tm),
        in_specs=in_specs, out_specs=out_specs, out_shape=out_shape,
        scratch_shapes=[pltpu.VMEM((len(DILATIONS), GROUP_WIDTH // LANES, tm, LANES), F32)] * 2
        + [pltpu.VMEM((FOURIER_WIDTH // LANES, tm, LANES), F32)],
        compiler_params=_params(("parallel", "parallel")), name="mix_out",
    )(*args)


def _swiglu(h, w1, w3, w2):
    a = jnp.dot(h, w1, preferred_element_type=F32)
    g = jnp.dot(h, w3, preferred_element_type=F32)
    act = (a * jax.nn.sigmoid(a) * g).astype(BF16)
    return jnp.dot(act, w2, preferred_element_type=F32)


def _ffn_kernel(h_ref, x_ref, w1_ref, w3_ref, w2_ref, gfin_ref, *rest, final_norm):
    o_ref = rest[len(rest) // 2]
    f = pl.program_id(1)

    @pl.when(f == 0)
    def _():
        o_ref[...] = x_ref[...]

    o_ref[...] += _swiglu(h_ref[...], w1_ref[...], w3_ref[...], w2_ref[...])

    if final_norm:
        @pl.when(f == pl.num_programs(1) - 1)
        def _():
            o_ref[...] = _rms(o_ref[...], gfin_ref[...])

    n_cast = len(rest) // 2
    for src, dst in zip(rest[:n_cast], rest[n_cast + 1:]):
        dst[...] = src[...].astype(dst.dtype)


def _ffn(h, x, w1, w3, w2, gfin, final_norm, tm, tf, cast_cols=(), cast_rows=()):
    n = h.shape[0]
    grid = (n // tm, FFN_DIM // tf)
    row = pl.BlockSpec((tm, D_MODEL), lambda i, f: (i, 0))
    in_specs = [row, row,
                pl.BlockSpec((D_MODEL, tf), lambda i, f: (0, f)),
                pl.BlockSpec((D_MODEL, tf), lambda i, f: (0, f)),
                pl.BlockSpec((tf, D_MODEL), lambda i, f: (f, 0)),
                pl.BlockSpec((1, D_MODEL), lambda i, f: (0, 0))]
    cast_specs, cast_shapes = [], []
    for arr, axis in [(a, 2) for a in cast_cols] + [(a, 1) for a in cast_rows]:
        per_expert = arr.shape[axis] // LANES
        assert arr.shape[0] * per_expert == grid[0] * grid[1], "one cast block per grid step"
        shape = (1, arr.shape[1], LANES) if axis == 2 else (1, LANES, arr.shape[2])

        def index(i, f, per_expert=per_expert, axis=axis):
            t = i * grid[1] + f
            return (t // per_expert, 0, t % per_expert) if axis == 2 else (t // per_expert, t % per_expert, 0)

        cast_specs.append(pl.BlockSpec(shape, index))
        cast_shapes.append(jax.ShapeDtypeStruct(arr.shape, BF16))
    outs = pl.pallas_call(
        functools.partial(_ffn_kernel, final_norm=final_norm),
        grid=grid, in_specs=in_specs + cast_specs, out_specs=[row] + cast_specs,
        out_shape=[jax.ShapeDtypeStruct((n, D_MODEL), F32)] + cast_shapes,
        compiler_params=_params(("arbitrary", "arbitrary")), name="ffn",
    )(h, x, w1, w3, w2, gfin, *cast_cols, *cast_rows)
    return outs[0], outs[1:]


MOE_SUB = 256
MOE_WIN = 128
MOE_ROWS = 128
MOE_TAIL = 64
ROW_ALIGN = 16


def _moe_kernel(h_ref, x_hbm, sel_t_ref, sel_ref, cw_ref, w1_ref, w3_ref, w2_ref, triu_ref, tril_ref, gfin_ref,
                o_ref, pos_t_ref, pos_ref, hs_ref, y_ref, pre_ref, x_sem, *, final_norm):
    tb = h_ref.shape[0]
    n_sub = tb // MOE_SUB
    blk, e, f = pl.program_id(0), pl.program_id(1), pl.program_id(2)
    n_e, n_f = pl.num_programs(1), pl.num_programs(2)

    @pl.when((blk == 0) & (e == 0) & (f == 0))
    def _():
        hs_ref[...] = jnp.zeros_like(hs_ref)
        y_ref[...] = jnp.zeros_like(y_ref)

    @pl.when((e == 0) & (f == 0))
    def _():
        x_copy = pltpu.make_async_copy(x_hbm.at[pl.ds(pl.multiple_of(blk * tb, tb), tb), :], o_ref, x_sem)
        x_copy.start()
        off_t = jnp.zeros((N_EXPERTS, 1), F32)
        off = jnp.zeros((1, LANES), F32)
        for s in range(n_sub):
            sl = slice(s * MOE_SUB, (s + 1) * MOE_SUB)
            sel_t = sel_t_ref[:, sl]
            within_t = jnp.dot(sel_t.astype(BF16), triu_ref[...], preferred_element_type=F32)
            pos_t_ref[:, sl] = jnp.where(sel_t > 0.0, within_t + off_t, -1.0)
            off_t = off_t + jnp.sum(sel_t, axis=1, keepdims=True)
            sel = sel_ref[sl, :]
            within = jnp.dot(tril_ref[...], sel.astype(BF16), preferred_element_type=F32)
            pos_ref[sl, :] = jnp.where(sel > 0.0, within + off, -1.0)
            off = off + jnp.sum(sel, axis=0, keepdims=True)
        x_copy.wait()

    def window(s):
        p0, p1 = pre_ref[s], pre_ref[s + 1]
        base = (p0 // ROW_ALIGN) * ROW_ALIGN
        return p0, base, (p1 - base + MOE_WIN - 1) // MOE_WIN

    @pl.when(f == 0)
    def _():
        run = jnp.int32(0)
        pre_ref[0] = run
        for s in range(n_sub):
            cnt = jnp.sum(sel_t_ref[pl.ds(e, 1), s * MOE_SUB:(s + 1) * MOE_SUB])
            run = run + cnt.astype(jnp.int32)
            pre_ref[s + 1] = run
        row = lax.broadcasted_iota(jnp.int32, (MOE_WIN, MOE_SUB), 0)
        row1 = lax.broadcasted_iota(jnp.int32, (MOE_WIN, 1), 0)

        def gather(s, j):
            p0, p1 = pre_ref[s], pre_ref[s + 1]
            _, base, _ = window(s)
            sl = slice(s * MOE_SUB, (s + 1) * MOE_SUB)
            r0 = pl.multiple_of(base + j * MOE_WIN, ROW_ALIGN)
            pos_i = pos_t_ref[pl.ds(e, 1), sl].astype(jnp.int32)
            one_hot = jnp.where(row == pos_i - r0, 1.0, 0.0).astype(BF16)
            rows = jnp.dot(one_hot, h_ref[sl, :], preferred_element_type=F32).astype(BF16)
            mine = (row1 >= p0 - r0) & (row1 < p1 - r0)
            hs_ref[pl.ds(r0, MOE_WIN), :] = jnp.where(mine, rows, hs_ref[pl.ds(r0, MOE_WIN), :])

        for s in range(n_sub):
            gather(s, 0)
        for s in range(n_sub):
            lax.fori_loop(1, window(s)[2], lambda j, c, s=s: (gather(s, j), c)[1], 0)

    full = 4 * MOE_ROWS
    n_total = pre_ref[n_sub]
    n_whole = n_total // full
    tail = n_total - n_whole * full
    widen = (tail > 0) & (tail <= MOE_TAIL) & (n_whole > 0)
    n_full = n_whole - widen.astype(jnp.int32)
    rest_start = n_whole * full
    n_units = jnp.where(widen, 0, (tail + MOE_ROWS - 1) // MOE_ROWS)
    rem = n_units % 4

    def expert_rows(r0, n_rows):
        y = _swiglu(hs_ref[pl.ds(r0, n_rows), :], w1_ref[0], w3_ref[0], w2_ref[0])

        @pl.when(f == 0)
        def _():
            y_ref[pl.ds(r0, n_rows), :] = y

        @pl.when(f > 0)
        def _():
            y_ref[pl.ds(r0, n_rows), :] += y

    def full_chunk(c, carry):
        expert_rows(pl.multiple_of(c * full, full), full)
        return carry

    lax.fori_loop(0, n_full, full_chunk, 0)

    @pl.when(widen)
    def _():
        expert_rows(pl.multiple_of(n_full * full, full), full + MOE_TAIL)

    @pl.when(n_units == 4)
    def _():
        expert_rows(pl.multiple_of(rest_start, full), full)

    @pl.when((rem == 2) | (rem == 3))
    def _():
        expert_rows(pl.multiple_of(rest_start, 2 * MOE_ROWS), 2 * MOE_ROWS)

    @pl.when(rem % 2 == 1)
    def _():
        expert_rows(pl.multiple_of(rest_start + (n_units - 1) * MOE_ROWS, MOE_ROWS), MOE_ROWS)

    @pl.when(f == n_f - 1)
    def _():
        lane = lax.broadcasted_iota(jnp.int32, (MOE_SUB, LANES), 1)

        def scatter(s, j):
            _, base, _ = window(s)
            sl = slice(s * MOE_SUB, (s + 1) * MOE_SUB)
            r0 = pl.multiple_of(base + j * MOE_WIN, ROW_ALIGN)
            mine = lane == e
            pos_i = jnp.sum(jnp.where(mine, pos_ref[sl, :], 0.0), axis=1, keepdims=True).astype(jnp.int32)
            weight = jnp.sum(jnp.where(mine, cw_ref[sl, :], 0.0), axis=1, keepdims=True)
            one_hot = jnp.where(lane == pos_i - r0, 1.0, 0.0).astype(BF16)
            rows = y_ref[pl.ds(r0, MOE_WIN), :].astype(BF16)
            back = jnp.dot(one_hot, rows, preferred_element_type=F32)
            o_ref[sl, :] += back * weight

        for s in range(n_sub):
            scatter(s, 0)
        for s in range(n_sub):
            lax.fori_loop(1, window(s)[2], lambda j, c, s=s: (scatter(s, j), c)[1], 0)

        if final_norm:
            @pl.when(e == n_e - 1)
            def _():
                o_ref[...] = _rms(o_ref[...], gfin_ref[...])


def _moe(h, x, sel_t, sel, cw, w1, w3, w2, gfin, final_norm, tb, tf):
    n = h.shape[0]
    tb = min(tb, n)
    n_exp = w1.shape[0]
    triu = jnp.triu(jnp.ones((MOE_SUB, MOE_SUB), F32), k=1).astype(BF16)
    pad = 2 * MOE_WIN
    once = pl.Buffered(1)
    tok = pl.BlockSpec((tb, LANES), lambda i, e, f: (i, 0), pipeline_mode=once)
    tri = pl.BlockSpec((MOE_SUB, MOE_SUB), lambda i, e, f: (0, 0), pipeline_mode=once)
    in_specs = [pl.BlockSpec((tb, D_MODEL), lambda i, e, f: (i, 0), pipeline_mode=once),
                pl.BlockSpec(memory_space=pl.ANY),
                pl.BlockSpec((N_EXPERTS, tb), lambda i, e, f: (0, i), pipeline_mode=once),
                tok, tok,
                pl.BlockSpec((1, D_MODEL, tf), lambda i, e, f: (e, 0, f)),
                pl.BlockSpec((1, D_MODEL, tf), lambda i, e, f: (e, 0, f)),
                pl.BlockSpec((1, tf, D_MODEL), lambda i, e, f: (e, f, 0)),
                tri, tri,
                pl.BlockSpec((1, D_MODEL), lambda i, e, f: (0, 0), pipeline_mode=once)]
    return pl.pallas_call(
        functools.partial(_moe_kernel, final_norm=final_norm),
        grid=(n // tb, n_exp, FFN_DIM // tf), in_specs=in_specs,
        out_specs=pl.BlockSpec((tb, D_MODEL), lambda i, e, f: (i, 0), pipeline_mode=once),
        out_shape=jax.ShapeDtypeStruct((n, D_MODEL), F32),
        scratch_shapes=[pltpu.VMEM((N_EXPERTS, tb), F32),
                        pltpu.VMEM((tb, LANES), F32),
                        pltpu.VMEM((tb + pad, D_MODEL), BF16),
                        pltpu.VMEM((tb + pad, D_MODEL), F32),
                        pltpu.SMEM((tb // MOE_SUB + 1,), jnp.int32),
                        pltpu.SemaphoreType.DMA(())],
        compiler_params=_params(("arbitrary", "arbitrary", "arbitrary"), MOE_VMEM_LIMIT), name="moe",
    )(h, x, sel_t, sel, cw, w1, w3, w2, triu, triu.T, gfin)


def _qk_column_order():
    half = HEAD_DIM // 2
    order = []
    for g in range(len(DILATIONS)):
        base = g * GROUP_WIDTH
        for part in range(2):
            for hd in range(HEADS_PER_GROUP):
                order.extend(range(base + hd * HEAD_DIM + part * half, base + hd * HEAD_DIM + (part + 1) * half))
    return np.asarray(order, np.int32)


def _dft_tables(n):
    j = jnp.arange(n, dtype=jnp.int32)
    phase = (j[:, None] * j[None, :]) % n
    ang = phase.astype(F32) * (2.0 * np.pi / n)
    return jnp.cos(ang), jnp.sin(ang)


def _tables(s):
    half = HEAD_DIM // 2
    inv_freq = ROPE_THETA ** (-jnp.arange(half, dtype=F32) * 2.0 / HEAD_DIM)
    ang = jnp.arange(s, dtype=F32)[:, None] * inv_freq[None, :]
    cos_t = jnp.tile(jnp.cos(ang), (1, LANES // half))
    sin_t = jnp.tile(jnp.sin(ang), (1, LANES // half))
    cg, sg = _dft_tables(FOURIER_GROUP_DIM)
    eye = jnp.eye(FOURIER_GROUPS, dtype=F32)
    scale_c = FOURIER_GROUP_DIM ** -0.5
    cc = (jnp.kron(eye, cg) * scale_c).astype(BF16)
    sc = (jnp.kron(eye, sg) * scale_c).astype(BF16)
    q_len = s // SEQ_RADIX
    cq, sq = _dft_tables(q_len)
    seq_tab = (jnp.concatenate([cq, sq], axis=1) * s ** -0.5).astype(BF16)
    pos = jnp.arange(q_len, dtype=jnp.int32)[None, :]
    ang = ((jnp.arange(1, SEQ_RADIX, dtype=jnp.int32)[:, None] * pos) % s).astype(F32) * (2.0 * np.pi / s)
    seq_tw = jnp.broadcast_to(jnp.stack([jnp.cos(ang), jnp.sin(ang)], axis=1)[..., None],
                              (SEQ_RADIX - 1, 2, q_len, LANES))
    return cos_t, sin_t, cc, sc, seq_tab, seq_tw


def kernel(x, ln_mix, w_in, w_fourier, w_attn, w_out, ln_ffn, dense_w1, dense_w3, dense_w2,
           router_w, router_b, moe_w1, moe_w3, moe_w2, ln_final):
    b, s, _ = x.shape
    depth = w_in.shape[0]
    n = b * s
    tm = 512
    cos_t, sin_t, cc, sc, seq_tab, seq_tw = _tables(s)
    order = _qk_column_order()
    gfin = ln_final.reshape(1, D_MODEL)
    expert_bf16 = []

    for layer in range(depth):
        w = w_in[layer]
        wq = w[:, COL_Q:COL_K][:, order]
        wk = w[:, COL_K:COL_V][:, order]
        w_l = jnp.concatenate([w[:, :COL_Q], wq, wk, w[:, COL_V:COL_GF]], axis=1).astype(BF16)
        w_gate = w[:, COL_GF:].astype(BF16)
        g1 = ln_mix[layer].reshape(1, D_MODEL)

        outs = _in_proj(x, ln_mix[layer].reshape(1, D_MODEL), w_l, cc, sc, cos_t, sin_t, min(1024, s))
        ab, qkv = outs[0], outs[1:10]
        yf = _seq_dft(ab, seq_tab, seq_tw, 256)
        attn = [_attention(qkv[3 * g], qkv[3 * g + 1], qkv[3 * g + 2]) for g in range(len(DILATIONS))]

        last = layer == depth - 1
        i = layer // 2
        if layer % 2 == 0:
            router = None
        else:
            rw = jnp.zeros((D_MODEL, LANES), F32).at[:, :N_EXPERTS].set(router_w[i])
            rw_hi = rw.astype(BF16)
            rw_lo = (rw - rw_hi.astype(F32)).astype(BF16)
            rb = jnp.full((1, LANES), NEG_INF, F32).at[0, :N_EXPERTS].set(router_b[i])
            router = (jnp.concatenate([rw_hi, rw_lo], axis=1), rb)
        res = _mix_out(x, yf, attn, g1, w_gate, w_fourier[layer].astype(BF16), w_attn[layer].astype(BF16),
                       w_out[layer].astype(BF16), ln_ffn[layer].reshape(1, D_MODEL), router, tm)
        xn, h2 = res[0].reshape(n, D_MODEL), res[1].reshape(n, D_MODEL)
        if layer % 2 == 0:
            ffn_tm, ffn_tf = 1024, 512
            ride = (not last and
                    (n // ffn_tm) * (FFN_DIM // ffn_tf) == N_EXPERTS * (FFN_DIM // LANES))
            y, expert_bf16 = _ffn(h2, xn, dense_w1[i].astype(BF16), dense_w3[i].astype(BF16),
                                  dense_w2[i].astype(BF16), gfin, last, ffn_tm, ffn_tf,
                                  (moe_w1[i], moe_w3[i]) if ride else (), (moe_w2[i],) if ride else ())
        else:
            if not expert_bf16:
                expert_bf16 = [w[i].astype(BF16) for w in (moe_w1, moe_w3, moe_w2)]
            y = _moe(h2, xn, res[4], res[3], res[2], *expert_bf16, gfin, last, 2048, 1792)
        x = y.reshape(b, s, D_MODEL)
    return x
```

```python
import functools

import jax
import jax.numpy as jnp
import numpy as np
from jax import lax
from jax.experimental import pallas as pl
from jax.experimental.pallas import tpu as pltpu

D_MODEL = 1024
FOURIER_GROUPS = 4
FOURIER_GROUP_DIM = 128
FOURIER_WIDTH = FOURIER_GROUPS * FOURIER_GROUP_DIM
DILATIONS = (1, 4, 16)
HALF_SPAN = 64
HEADS_PER_GROUP = 4
HEAD_DIM = 64
GROUP_WIDTH = HEADS_PER_GROUP * HEAD_DIM
ATTN_WIDTH = len(DILATIONS) * GROUP_WIDTH
ROPE_THETA = 10000.0
FFN_DIM = 3584
N_EXPERTS = 8
RMS_EPS = 1e-6
NEG_INF = -1e30
LOG2E = float(np.log2(np.e))
LN2 = float(np.log(2.0))
LANES = 128

COL_UF = 0
COL_Q = FOURIER_WIDTH
COL_K = COL_Q + ATTN_WIDTH
COL_V = COL_K + ATTN_WIDTH
COL_GF = COL_V + ATTN_WIDTH
COL_GA = COL_GF + D_MODEL
IN_COLS = COL_GA + D_MODEL

VMEM_LIMIT = 56 * 1024 * 1024
MOE_VMEM_LIMIT = 60 * 1024 * 1024

IN_PROJ_ROWS = 1024
SEQ_DFT_COLS = 256
MIX_ROWS = 512
FFN_ROWS = 1024
FFN_COLS = 512
MOE_TOKENS = 2048
MOE_COLS = 1792

BF16 = jnp.bfloat16
F32 = jnp.float32


def _params(semantics, vmem_limit=VMEM_LIMIT):
    return pltpu.CompilerParams(dimension_semantics=semantics, vmem_limit_bytes=vmem_limit)


def _rms(x, gain):
    ms = jnp.mean(x * x, axis=-1, keepdims=True)
    return x * lax.rsqrt(ms + RMS_EPS) * gain


def _in_proj_kernel(x_ref, g_ref, w_ref, cc_ref, sc_ref, cos_ref, sin_ref,
                    ab_ref, q0_ref, k0_ref, v0_ref, q1_ref, k1_ref, v1_ref, q2_ref, k2_ref, v2_ref,
                    scr_ref):
    tm = x_ref.shape[1]
    h = _rms(x_ref[0], g_ref[...]).astype(BF16)

    def proj(col, width):
        return jnp.dot(h, w_ref[:, col:col + width], preferred_element_type=F32)

    uf = proj(COL_UF, FOURIER_WIDTH).astype(BF16)
    ab_ref[0] = jnp.dot(uf, cc_ref[...], preferred_element_type=F32).astype(BF16)
    ab_ref[1] = jnp.dot(uf, sc_ref[...], preferred_element_type=F32).astype(BF16)

    cos = cos_ref[...]
    sin = sin_ref[...]

    def rope(t):
        t1, t2 = t[:, :LANES], t[:, LANES:]
        return jnp.concatenate([t1 * cos - t2 * sin, t2 * cos + t1 * sin], axis=-1)

    def put(ref, val, d):
        if d == 1:
            ref[0, 0] = val.astype(BF16)
            return
        for c in range(GROUP_WIDTH // LANES):
            scr_ref[c] = val[:, c * LANES:(c + 1) * LANES]
        for r in range(d):
            for c in range(GROUP_WIDTH // LANES):
                ref[0, r, :, c * LANES:(c + 1) * LANES] = scr_ref[c, pl.ds(r, tm // d, stride=d), :].astype(BF16)

    outs = ((q0_ref, k0_ref, v0_ref), (q1_ref, k1_ref, v1_ref), (q2_ref, k2_ref, v2_ref))
    for g, d in enumerate(DILATIONS):
        qr, kr, vr = outs[g]
        put(qr, rope(proj(COL_Q + g * GROUP_WIDTH, GROUP_WIDTH)) * (HEAD_DIM ** -0.5 * LOG2E), d)
        put(kr, rope(proj(COL_K + g * GROUP_WIDTH, GROUP_WIDTH)), d)
        put(vr, proj(COL_V + g * GROUP_WIDTH, GROUP_WIDTH), d)


def _in_proj(x, gain, w, cc, sc, cos_t, sin_t, tm):
    b, s, _ = x.shape
    grid = (b, s // tm)
    qkv_shapes, qkv_specs = [], []
    for d in DILATIONS:
        for _ in range(3):
            qkv_shapes.append(jax.ShapeDtypeStruct((b, d, s // d, GROUP_WIDTH), BF16))
            qkv_specs.append(pl.BlockSpec((1, d, tm // d, GROUP_WIDTH), lambda bi, si: (bi, 0, si, 0)))
    out_shape = [jax.ShapeDtypeStruct((2, s, b * FOURIER_WIDTH), BF16)] + qkv_shapes
    out_specs = [pl.BlockSpec((2, tm, FOURIER_WIDTH), lambda bi, si: (0, si, bi))] + qkv_specs
    in_specs = [
        pl.BlockSpec((1, tm, D_MODEL), lambda bi, si: (bi, si, 0)),
        pl.BlockSpec((1, D_MODEL), lambda bi, si: (0, 0)),
        pl.BlockSpec((D_MODEL, COL_GF), lambda bi, si: (0, 0)),
        pl.BlockSpec((FOURIER_WIDTH, FOURIER_WIDTH), lambda bi, si: (0, 0)),
        pl.BlockSpec((FOURIER_WIDTH, FOURIER_WIDTH), lambda bi, si: (0, 0)),
        pl.BlockSpec((tm, LANES), lambda bi, si: (si, 0)),
        pl.BlockSpec((tm, LANES), lambda bi, si: (si, 0)),
    ]
    return pl.pallas_call(
        _in_proj_kernel, grid=grid, in_specs=in_specs, out_specs=out_specs, out_shape=out_shape,
        scratch_shapes=[pltpu.VMEM((GROUP_WIDTH // LANES, tm, LANES), F32)],
        compiler_params=_params(("parallel", "parallel")), name="in_proj",
    )(x, gain, w, cc, sc, cos_t, sin_t)


SEQ_RADIX = 4


def _seq_dft_kernel(ab_ref, tab_ref, tw_ref, o_ref, u_ref):
    q_len = tab_ref.shape[0]
    for c in range(o_ref.shape[2] // LANES):
        cols = slice(c * LANES, (c + 1) * LANES)
        re = [ab_ref[0, j, :, cols].astype(F32) for j in range(SEQ_RADIX)]
        im = [-ab_ref[1, j, :, cols].astype(F32) for j in range(SEQ_RADIX)]
        p_re, p_im, q_re, q_im = re[0] + re[2], im[0] + im[2], re[0] - re[2], im[0] - im[2]
        s_re, s_im, d_re, d_im = re[1] + re[3], im[1] + im[3], re[1] - re[3], im[1] - im[3]
        terms = ((p_re + s_re, p_im + s_im), (q_re + d_im, q_im - d_re),
                 (p_re - s_re, p_im - s_im), (q_re - d_im, q_im + d_re))
        for r, (t_re, t_im) in enumerate(terms):
            if r > 0:
                cos, sin = tw_ref[r - 1, 0], tw_ref[r - 1, 1]
                t_re, t_im = t_re * cos + t_im * sin, t_im * cos - t_re * sin
            u_ref[r, :q_len, cols] = t_re.astype(BF16)
            u_ref[r, q_len:, cols] = t_im.astype(BF16)
    for r in range(SEQ_RADIX):
        o_ref[r] = jnp.dot(tab_ref[...], u_ref[r], preferred_element_type=F32).astype(o_ref.dtype)


def _seq_dft(ab, tab, tw, bn):
    _, s, n = ab.shape
    q_len = s // SEQ_RADIX
    bn = min(bn, n)
    return pl.pallas_call(
        _seq_dft_kernel, grid=(n // bn,),
        in_specs=[pl.BlockSpec((2, SEQ_RADIX, q_len, bn), lambda j: (0, 0, 0, j)),
                  pl.BlockSpec((q_len, 2 * q_len), lambda j: (0, 0), pipeline_mode=pl.Buffered(1)),
                  pl.BlockSpec((SEQ_RADIX - 1, 2, q_len, LANES), lambda j: (0, 0, 0, 0),
                               pipeline_mode=pl.Buffered(1))],
        out_specs=pl.BlockSpec((SEQ_RADIX, q_len, bn), lambda j: (0, 0, j)),
        out_shape=jax.ShapeDtypeStruct((SEQ_RADIX, q_len, n), BF16),
        scratch_shapes=[pltpu.VMEM((SEQ_RADIX, 2 * q_len, bn), BF16)],
        compiler_params=_params(("parallel",)), name="seq_dft",
    )(ab.reshape(2, SEQ_RADIX, q_len, n), tab, tw)


ATTN_UNROLL = 8


def _attn_kernel(q_ref, k_ref, v_ref, o_ref, lse_ref, *, qb, tk, seq_len, unroll):
    lane = lax.broadcasted_iota(jnp.int32, (1, GROUP_WIDTH), 1)
    qk_head = (lane % LANES) // (HEAD_DIM // 2)
    v_head = lane // HEAD_DIM
    col_minus_row = (lax.broadcasted_iota(jnp.int32, (qb, tk), 1)
                     - lax.broadcasted_iota(jnp.int32, (qb, tk), 0))
    n_res = q_ref.shape[1]
    n_sub = seq_len // qb

    def query_block(res, sub):
        m0 = pl.multiple_of(sub * qb, qb)
        start = pl.multiple_of(jnp.clip(m0 - HALF_SPAN, 0, seq_len - tk), 16)
        q = q_ref[0, res, pl.ds(m0, qb), :]
        k = k_ref[0, res, pl.ds(start, tk), :]
        v = v_ref[0, res, pl.ds(start, tk), :]
        valid = jnp.abs(col_minus_row + (start - m0)) <= HALF_SPAN
        q_heads = jnp.concatenate(
            [jnp.where(qk_head == hd, q, jnp.zeros_like(q)) for hd in range(HEADS_PER_GROUP)], axis=0)
        s = lax.dot_general(q_heads, k, (((1,), (1,)), ((), ())), preferred_element_type=F32)
        s = jnp.where(valid[None], s.reshape(HEADS_PER_GROUP, qb, tk), NEG_INF)
        m = jnp.max(s, axis=-1, keepdims=True)
        p = jnp.exp2(s - m)
        l = jnp.sum(p, axis=-1, keepdims=True)
        pv = jnp.dot(p.astype(BF16).reshape(HEADS_PER_GROUP * qb, tk), v, preferred_element_type=F32)
        pv = pv.reshape(HEADS_PER_GROUP, qb, GROUP_WIDTH) / l
        lse = (m + jnp.log2(l)) * LN2
        o_acc = pv[0]
        lse_acc = jnp.broadcast_to(lse[0], (qb, GROUP_WIDTH))
        for hd in range(1, HEADS_PER_GROUP):
            o_acc = jnp.where(v_head == hd, pv[hd], o_acc)
            lse_acc = jnp.where(v_head == hd, lse[hd], lse_acc)
        o_ref[0, res, pl.ds(m0, qb), :] = o_acc.astype(o_ref.dtype)
        lse_ref[0, res, pl.ds(m0, qb), :] = lse_acc

    def body(it, carry):
        for j in range(unroll):
            if n_sub % unroll == 0:
                per_res = n_sub // unroll
                query_block(it // per_res, (it % per_res) * unroll + j)
            else:
                query_block(it * (unroll // n_sub) + j // n_sub, j % n_sub)
        return carry

    lax.fori_loop(0, n_res * n_sub // unroll, body, 0)


def _attention(q, k, v):
    b, d, seq_len, _ = q.shape
    qb = min(128, seq_len)
    tk = min(qb + 2 * HALF_SPAN, seq_len)
    n_sub = seq_len // qb
    n_res = min(d, max(1, ATTN_UNROLL // n_sub))
    unroll = min(ATTN_UNROLL, n_res * n_sub)
    assert (n_res * n_sub) % unroll == 0 and (n_sub % unroll == 0 or unroll % n_sub == 0)
    kern = functools.partial(_attn_kernel, qb=qb, tk=tk, seq_len=seq_len, unroll=unroll)
    blk = pl.BlockSpec((1, n_res, seq_len, GROUP_WIDTH), lambda bi, ri: (bi, ri, 0, 0))
    shape = (b, d, seq_len, GROUP_WIDTH)
    return pl.pallas_call(
        kern, grid=(b, d // n_res), in_specs=[blk, blk, blk], out_specs=[blk, blk],
        out_shape=[jax.ShapeDtypeStruct(shape, BF16), jax.ShapeDtypeStruct(shape, F32)],
        compiler_params=_params(("parallel", "parallel")), name=f"attn_d{d}",
    )(q, k, v)


def _mix_out_kernel(*refs, routed):
    (x_ref, yf_ref, o0_ref, l0_ref, o1_ref, l1_ref, o2_ref, l2_ref, g1_ref, wg_ref,
     wf_ref, wa_ref, wo_ref, g2_ref) = refs[:14]
    if routed:
        rw_ref, rb_ref, xn_ref, h2_ref, cw_ref, sel_ref, sel_t_ref, o_scr, l_scr, f_scr = refs[14:]
    else:
        xn_ref, h2_ref, o_scr, l_scr, f_scr = refs[14:]
    tm = x_ref.shape[1]

    def gathered(ref, scr, d):
        if d == 1:
            return ref[0]
        n_chunk = ref.shape[-1] // LANES
        for r in range(d):
            for c in range(n_chunk):
                scr[c, pl.ds(r, tm // d, stride=d), :] = ref[r, :, c * LANES:(c + 1) * LANES].astype(F32)
        return jnp.concatenate([scr[c] for c in range(n_chunk)], axis=-1)

    h1 =_rms(x_ref[0], g1_ref[...]).astype(BF16)
    gate_f = jax.nn.sigmoid(jnp.dot(h1, wg_ref[:, :D_MODEL], preferred_element_type=F32))
    gate_a = jax.nn.sigmoid(jnp.dot(h1, wg_ref[:, D_MODEL:], preferred_element_type=F32))
    y_mix = gathered(yf_ref, f_scr, SEQ_RADIX).astype(BF16)
    y_f = jnp.dot(y_mix, wf_ref[...], preferred_element_type=F32)

    o_refs, l_refs = (o0_ref, o1_ref, o2_ref), (l0_ref, l1_ref, l2_ref)
    os_, ls_ = [], []
    for g, d in enumerate(DILATIONS):
        os_.append(gathered(o_refs[g].at[0], o_scr.at[g], d))
        ls_.append(gathered(l_refs[g].at[0], l_scr.at[g], d))
    mx = jnp.maximum(jnp.maximum(ls_[0], ls_[1]), ls_[2])
    es = [jnp.exp(l - mx) for l in ls_]
    den = es[0] + es[1] + es[2]
    o_att = (es[0] * os_[0] + es[1] * os_[1] + es[2] * os_[2]) / den
    y_a = jnp.dot(o_att.astype(BF16), wa_ref[...], preferred_element_type=F32)
    z = gate_f * y_f + gate_a * y_a
    xn = x_ref[0] +jnp.dot(z.astype(BF16), wo_ref[...], preferred_element_type=F32)
    xn_ref[0] = xn
    h2 = _rms(xn, g2_ref[...])
    h2_hi = h2.astype(BF16)
    h2_ref[0] = h2_hi

    if routed:
        h2_lo = (h2 - h2_hi.astype(F32)).astype(BF16)
        hi = jnp.dot(h2_hi, rw_ref[...], preferred_element_type=F32)
        logits = (hi[:, :LANES] + hi[:, LANES:]
                  + jnp.dot(h2_lo, rw_ref[:, :LANES], preferred_element_type=F32)) + rb_ref[...]
        lane = lax.broadcasted_iota(jnp.int32, logits.shape, 1)
        big = jnp.int32(LANES)
        m1 = jnp.max(logits, axis=-1, keepdims=True)
        i1 = jnp.min(jnp.where(logits == m1, lane, big), axis=-1, keepdims=True)
        rest = jnp.where(lane == i1, NEG_INF * 2, logits)
        m2 = jnp.max(rest, axis=-1, keepdims=True)
        i2 = jnp.min(jnp.where(rest == m2, lane, big), axis=-1, keepdims=True)
        e2 = jnp.exp(m2 - m1)
        w1 = 1.0 / (1.0 + e2)
        w2 = e2 / (1.0 + e2)
        cw_ref[...] = jnp.where(lane == i1, w1, jnp.where(lane == i2, w2, 0.0))
        sel = jnp.where((lane == i1) | (lane == i2), 1.0, 0.0)
        sel_ref[...] = sel
        sel_t_ref[...] = sel.T[:N_EXPERTS, :]


def _mix_out(x, yf, attn, g1, wg, wf, wa, wo, g2, router, tm):
    b, s, _ = x.shape
    routed = router is not None
    row = pl.BlockSpec((1, tm, D_MODEL), lambda bi, si: (bi, si, 0))
    full = lambda shape: pl.BlockSpec(shape, lambda bi, si: (0,) * len(shape), pipeline_mode=pl.Buffered(1))
    in_specs = [row, pl.BlockSpec((SEQ_RADIX, tm // SEQ_RADIX, FOURIER_WIDTH), lambda bi, si: (0, si, bi))]
    args = [x, yf]
    for d, (o, l) in zip(DILATIONS, attn):
        spec = pl.BlockSpec((1, d, tm // d, GROUP_WIDTH), lambda bi, si: (bi, 0, si, 0))
        in_specs += [spec, spec]
        args += [o, l]
    in_specs += [full((1, D_MODEL)), full((D_MODEL, 2 * D_MODEL)), full((FOURIER_WIDTH, D_MODEL)),
                 full((GROUP_WIDTH, D_MODEL)), full((D_MODEL, D_MODEL)), full((1, D_MODEL))]
    args += [g1, wg, wf, wa, wo, g2]
    out_shape = [jax.ShapeDtypeStruct((b, s, D_MODEL), F32), jax.ShapeDtypeStruct((b, s, D_MODEL), BF16)]
    out_specs = [row, row]
    if routed:
        in_specs += [full((D_MODEL, 2 * LANES)), full((1, LANES))]
        args += list(router)
        tok = pl.BlockSpec((tm, LANES), lambda bi, si: (bi * (s // tm) + si, 0))
        out_shape += [jax.ShapeDtypeStruct((b * s, LANES), F32)] * 2 + [jax.ShapeDtypeStruct((N_EXPERTS, b * s), F32)]
        out_specs += [tok, tok, pl.BlockSpec((N_EXPERTS, tm), lambda bi, si: (0, bi * (s // tm) + si))]
    return pl.pallas_call(
        functools.partial(_mix_out_kernel, routed=routed), grid=(b, s // tm),
        in_specs=in_specs, out_specs=out_specs, out_shape=out_shape,
        scratch_shapes=[pltpu.VMEM((len(DILATIONS), GROUP_WIDTH // LANES, tm, LANES), F32)] * 2
        + [pltpu.VMEM((FOURIER_WIDTH // LANES, tm, LANES), F32)],
        compiler_params=_params(("parallel", "parallel")), name="mix_out",
    )(*args)


def _swiglu(h, w1, w3, w2):
    a = jnp.dot(h, w1, preferred_element_type=F32)
    g = jnp.dot(h, w3, preferred_element_type=F32)
    act = (a * jax.nn.sigmoid(a) * g).astype(BF16)
    return jnp.dot(act, w2, preferred_element_type=F32)


def _ffn_kernel(h_ref, x_ref, w1_ref, w3_ref, w2_ref, gfin_ref, *rest, final_norm):
    o_ref = rest[len(rest) // 2]
    f = pl.program_id(1)

    @pl.when(f == 0)
    def _():
        o_ref[...] = x_ref[...]

    o_ref[...] += _swiglu(h_ref[...], w1_ref[...], w3_ref[...], w2_ref[...])

    if final_norm:
        @pl.when(f == pl.num_programs(1) - 1)
        def _():
            o_ref[...] = _rms(o_ref[...], gfin_ref[...])

    n_cast = len(rest) // 2
    for src, dst in zip(rest[:n_cast], rest[n_cast + 1:]):
        dst[...] = src[...].astype(dst.dtype)


def _ffn(h, x, w1, w3, w2, gfin, final_norm, tm, tf, cast_cols=(), cast_rows=()):
    n = h.shape[0]
    grid = (n // tm, FFN_DIM // tf)
    row = pl.BlockSpec((tm, D_MODEL), lambda i, f: (i, 0))
    in_specs = [row, row,
                pl.BlockSpec((D_MODEL, tf), lambda i, f: (0, f)),
                pl.BlockSpec((D_MODEL, tf), lambda i, f: (0, f)),
                pl.BlockSpec((tf, D_MODEL), lambda i, f: (f, 0)),
                pl.BlockSpec((1, D_MODEL), lambda i, f: (0, 0))]
    cast_specs, cast_shapes = [], []
    for arr, axis in [(a, 2) for a in cast_cols] + [(a, 1) for a in cast_rows]:
        per_expert = arr.shape[axis] // LANES
        assert arr.shape[0] * per_expert == grid[0] * grid[1], "one cast block per grid step"
        shape = (1, arr.shape[1], LANES) if axis == 2 else (1, LANES, arr.shape[2])

        def index(i, f, per_expert=per_expert, axis=axis):
            t = i * grid[1] + f
            return (t // per_expert, 0, t % per_expert) if axis == 2 else (t // per_expert, t % per_expert, 0)

        cast_specs.append(pl.BlockSpec(shape, index))
        cast_shapes.append(jax.ShapeDtypeStruct(arr.shape, BF16))
    outs = pl.pallas_call(
        functools.partial(_ffn_kernel, final_norm=final_norm),
        grid=grid, in_specs=in_specs + cast_specs, out_specs=[row] + cast_specs,
        out_shape=[jax.ShapeDtypeStruct((n, D_MODEL), F32)] + cast_shapes,
        compiler_params=_params(("arbitrary", "arbitrary")), name="ffn",
    )(h, x, w1, w3, w2, gfin, *cast_cols, *cast_rows)
    return outs[0], outs[1:]


MOE_SUB = 256
MOE_WIN = 128
MOE_ROWS = 128
MOE_TAIL = 64
ROW_ALIGN = 16


def _moe_kernel(h_ref, x_hbm, sel_t_ref, sel_ref, cw_ref, w1_ref, w3_ref, w2_ref, triu_ref, tril_ref, gfin_ref,
                o_ref, pos_t_ref, pos_ref, hs_ref, y_ref, pre_ref, x_sem, *, final_norm):
    tb = h_ref.shape[0]
    n_sub = tb // MOE_SUB
    blk, e, f = pl.program_id(0), pl.program_id(1), pl.program_id(2)
    n_e, n_f = pl.num_programs(1), pl.num_programs(2)

    @pl.when((blk == 0) & (e == 0) & (f == 0))
    def _():
        hs_ref[...] = jnp.zeros_like(hs_ref)
        y_ref[...] = jnp.zeros_like(y_ref)

    @pl.when((e == 0) & (f == 0))
    def _():
        x_copy = pltpu.make_async_copy(x_hbm.at[pl.ds(pl.multiple_of(blk * tb, tb), tb), :], o_ref, x_sem)
        x_copy.start()
        off_t = jnp.zeros((N_EXPERTS, 1), F32)
        off = jnp.zeros((1, LANES), F32)
        for s in range(n_sub):
            sl = slice(s * MOE_SUB, (s + 1) * MOE_SUB)
            sel_t = sel_t_ref[:, sl]
            within_t = jnp.dot(sel_t.astype(BF16), triu_ref[...], preferred_element_type=F32)
            pos_t_ref[:, sl] = jnp.where(sel_t > 0.0, within_t + off_t, -1.0)
            off_t = off_t + jnp.sum(sel_t, axis=1, keepdims=True)
            sel = sel_ref[sl, :]
            within = jnp.dot(tril_ref[...], sel.astype(BF16), preferred_element_type=F32)
            pos_ref[sl, :] = jnp.where(sel > 0.0, within + off, -1.0)
            off = off + jnp.sum(sel, axis=0, keepdims=True)
        x_copy.wait()

    def window(s):
        p0, p1 = pre_ref[s], pre_ref[s + 1]
        base = (p0 // ROW_ALIGN) * ROW_ALIGN
        return p0, base, (p1 - base + MOE_WIN - 1) // MOE_WIN

    @pl.when(f == 0)
    def _():
        run = jnp.int32(0)
        pre_ref[0] = run
        for s in range(n_sub):
            cnt = jnp.sum(sel_t_ref[pl.ds(e, 1), s * MOE_SUB:(s + 1) * MOE_SUB])
            run = run + cnt.astype(jnp.int32)
            pre_ref[s + 1] = run
        row = lax.broadcasted_iota(jnp.int32, (MOE_WIN, MOE_SUB), 0)
        row1 = lax.broadcasted_iota(jnp.int32, (MOE_WIN, 1), 0)

        def gather(s, j):
            p0, p1 = pre_ref[s], pre_ref[s + 1]
            _, base, _ = window(s)
            sl = slice(s * MOE_SUB, (s + 1) * MOE_SUB)
            r0 = pl.multiple_of(base + j * MOE_WIN, ROW_ALIGN)
            pos_i = pos_t_ref[pl.ds(e, 1), sl].astype(jnp.int32)
            one_hot = jnp.where(row == pos_i - r0, 1.0, 0.0).astype(BF16)
            rows = jnp.dot(one_hot, h_ref[sl, :], preferred_element_type=F32).astype(BF16)
            mine = (row1 >= p0 - r0) & (row1 < p1 - r0)
            hs_ref[pl.ds(r0, MOE_WIN), :] = jnp.where(mine, rows, hs_ref[pl.ds(r0, MOE_WIN), :])

        for s in range(n_sub):
            gather(s, 0)
        for s in range(n_sub):
            lax.fori_loop(1, window(s)[2], lambda j, c, s=s: (gather(s, j), c)[1], 0)

    full = 4 * MOE_ROWS
    n_total = pre_ref[n_sub]
    n_whole = n_total // full
    tail = n_total - n_whole * full
    widen = (tail > 0) & (tail <= MOE_TAIL) & (n_whole > 0)
    n_full = n_whole - widen.astype(jnp.int32)
    rest_start = n_whole * full
    n_units = jnp.where(widen, 0, (tail + MOE_ROWS - 1) // MOE_ROWS)
    rem = n_units % 4

    def expert_rows(r0, n_rows):
        y = _swiglu(hs_ref[pl.ds(r0, n_rows), :], w1_ref[0], w3_ref[0], w2_ref[0])
        y_ref[pl.ds(r0, n_rows), :] = jnp.where(f == 0, y, y_ref[pl.ds(r0, n_rows), :] + y)

    def full_chunk(c, carry):
        expert_rows(pl.multiple_of(c * full, full), full)
        return carry

    lax.fori_loop(0, n_full, full_chunk, 0)

    @pl.when(widen)
    def _():
        expert_rows(pl.multiple_of(n_full * full, full), full + MOE_TAIL)

    @pl.when(n_units == 4)
    def _():
        expert_rows(pl.multiple_of(rest_start, full), full)

    @pl.when((rem == 2) | (rem == 3))
    def _():
        expert_rows(pl.multiple_of(rest_start, 2 * MOE_ROWS), 2 * MOE_ROWS)

    @pl.when(rem % 2 == 1)
    def _():
        expert_rows(pl.multiple_of(rest_start + (n_units - 1) * MOE_ROWS, MOE_ROWS), MOE_ROWS)

    @pl.when(f == n_f - 1)
    def _():
        lane = lax.broadcasted_iota(jnp.int32, (MOE_SUB, LANES), 1)

        def scatter(s, j):
            _, base, _ = window(s)
            sl = slice(s * MOE_SUB, (s + 1) * MOE_SUB)
            r0 = pl.multiple_of(base + j * MOE_WIN, ROW_ALIGN)
            mine = lane == e
            pos_i = jnp.sum(jnp.where(mine, pos_ref[sl, :], 0.0), axis=1, keepdims=True).astype(jnp.int32)
            weight = jnp.sum(jnp.where(mine, cw_ref[sl, :], 0.0), axis=1, keepdims=True)
            one_hot = jnp.where(lane == pos_i - r0, 1.0, 0.0).astype(BF16)
            rows = y_ref[pl.ds(r0, MOE_WIN), :].astype(BF16)
            back = jnp.dot(one_hot, rows, preferred_element_type=F32)
            o_ref[sl, :] += back * weight

        for s in range(n_sub):
            scatter(s, 0)
        for s in range(n_sub):
            lax.fori_loop(1, window(s)[2], lambda j, c, s=s: (scatter(s, j), c)[1], 0)

        if final_norm:
            @pl.when(e == n_e - 1)
            def _():
                o_ref[...] = _rms(o_ref[...], gfin_ref[...])


def _moe(h, x, sel_t, sel, cw, w1, w3, w2, gfin, final_norm, tb, tf):
    n = h.shape[0]
    tb = min(tb, n)
    n_exp = w1.shape[0]
    triu = jnp.triu(jnp.ones((MOE_SUB, MOE_SUB), F32), k=1).astype(BF16)
    pad = 2 * MOE_WIN
    once = pl.Buffered(1)
    tok = pl.BlockSpec((tb, LANES), lambda i, e, f: (i, 0), pipeline_mode=once)
    tri = pl.BlockSpec((MOE_SUB, MOE_SUB), lambda i, e, f: (0, 0), pipeline_mode=once)
    in_specs = [pl.BlockSpec((tb, D_MODEL), lambda i, e, f: (i, 0), pipeline_mode=once),
                pl.BlockSpec(memory_space=pl.ANY),
                pl.BlockSpec((N_EXPERTS, tb), lambda i, e, f: (0, i), pipeline_mode=once),
                tok, tok,
                pl.BlockSpec((1, D_MODEL, tf), lambda i, e, f: (e, 0, f)),
                pl.BlockSpec((1, D_MODEL, tf), lambda i, e, f: (e, 0, f)),
                pl.BlockSpec((1, tf, D_MODEL), lambda i, e, f: (e, f, 0)),
                tri, tri,
                pl.BlockSpec((1, D_MODEL), lambda i, e, f: (0, 0), pipeline_mode=once)]
    return pl.pallas_call(
        functools.partial(_moe_kernel, final_norm=final_norm),
        grid=(n // tb, n_exp, FFN_DIM // tf), in_specs=in_specs,
        out_specs=pl.BlockSpec((tb, D_MODEL), lambda i, e, f: (i, 0), pipeline_mode=once),
        out_shape=jax.ShapeDtypeStruct((n, D_MODEL), F32),
        scratch_shapes=[pltpu.VMEM((N_EXPERTS, tb), F32),
                        pltpu.VMEM((tb, LANES), F32),
                        pltpu.VMEM((tb + pad, D_MODEL), BF16),
                        pltpu.VMEM((tb + pad, D_MODEL), F32),
                        pltpu.SMEM((tb // MOE_SUB + 1,), jnp.int32),
                        pltpu.SemaphoreType.DMA(())],
        compiler_params=_params(("arbitrary", "arbitrary", "arbitrary"), MOE_VMEM_LIMIT), name="moe",
    )(h, x, sel_t, sel, cw, w1, w3, w2, triu, triu.T, gfin)


def _qk_column_order():
    half = HEAD_DIM // 2
    order = []
    for g in range(len(DILATIONS)):
        base = g * GROUP_WIDTH
        for part in range(2):
            for hd in range(HEADS_PER_GROUP):
                order.extend(range(base + hd * HEAD_DIM + part * half, base + hd * HEAD_DIM + (part + 1) * half))
    return np.asarray(order, np.int32)


def _dft_tables(n):
    j = jnp.arange(n, dtype=jnp.int32)
    phase = (j[:, None] * j[None, :]) % n
    ang = phase.astype(F32) * (2.0 * np.pi / n)
    return jnp.cos(ang), jnp.sin(ang)


def _tables(s):
    half = HEAD_DIM // 2
    inv_freq = ROPE_THETA ** (-jnp.arange(half, dtype=F32) * 2.0 / HEAD_DIM)
    ang = jnp.arange(s, dtype=F32)[:, None] * inv_freq[None, :]
    cos_t = jnp.tile(jnp.cos(ang), (1, LANES // half))
    sin_t = jnp.tile(jnp.sin(ang), (1, LANES // half))
    cg, sg = _dft_tables(FOURIER_GROUP_DIM)
    eye = jnp.eye(FOURIER_GROUPS, dtype=F32)
    scale_c = FOURIER_GROUP_DIM ** -0.5
    cc = (jnp.kron(eye, cg) * scale_c).astype(BF16)
    sc = (jnp.kron(eye, sg) * scale_c).astype(BF16)
    q_len = s // SEQ_RADIX
    cq, sq = _dft_tables(q_len)
    seq_tab = (jnp.concatenate([cq, sq], axis=1) * s ** -0.5).astype(BF16)
    pos = jnp.arange(q_len, dtype=jnp.int32)[None, :]
    ang = ((jnp.arange(1, SEQ_RADIX, dtype=jnp.int32)[:, None] * pos) % s).astype(F32) * (2.0 * np.pi / s)
    seq_tw = jnp.broadcast_to(jnp.stack([jnp.cos(ang), jnp.sin(ang)], axis=1)[..., None],
                              (SEQ_RADIX - 1, 2, q_len, LANES))
    return cos_t, sin_t, cc, sc, seq_tab, seq_tw


def kernel(x, ln_mix, w_in, w_fourier, w_attn, w_out, ln_ffn, dense_w1, dense_w3, dense_w2,
           router_w, router_b, moe_w1, moe_w3, moe_w2, ln_final):
    b, s, _ = x.shape
    depth = w_in.shape[0]
    n = b * s
    cos_t, sin_t, cc, sc, seq_tab, seq_tw = _tables(s)
    order = _qk_column_order()
    gfin = ln_final.reshape(1, D_MODEL)
    expert_bf16 = []

    for layer in range(depth):
        w = w_in[layer]
        wq = w[:, COL_Q:COL_K][:, order]
        wk = w[:, COL_K:COL_V][:, order]
        w_l = jnp.concatenate([w[:, :COL_Q], wq, wk, w[:, COL_V:COL_GF]], axis=1).astype(BF16)
        w_gate = w[:, COL_GF:].astype(BF16)
        g1 = ln_mix[layer].reshape(1, D_MODEL)

        outs = _in_proj(x, g1, w_l, cc, sc, cos_t, sin_t, min(IN_PROJ_ROWS, s))
        ab, qkv = outs[0], outs[1:10]
        yf = _seq_dft(ab, seq_tab, seq_tw, SEQ_DFT_COLS)
        attn = [_attention(qkv[3 * g], qkv[3 * g + 1], qkv[3 * g + 2]) for g in range(len(DILATIONS))]

        last = layer == depth - 1
        i = layer // 2
        if layer % 2 == 0:
            router = None
        else:
            rw = jnp.zeros((D_MODEL, LANES), F32).at[:, :N_EXPERTS].set(router_w[i])
            rw_hi = rw.astype(BF16)
            rw_lo = (rw - rw_hi.astype(F32)).astype(BF16)
            rb = jnp.full((1, LANES), NEG_INF, F32).at[0, :N_EXPERTS].set(router_b[i])
            router = (jnp.concatenate([rw_hi, rw_lo], axis=1), rb)
        res = _mix_out(x, yf, attn, g1, w_gate, w_fourier[layer].astype(BF16), w_attn[layer].astype(BF16),
                       w_out[layer].astype(BF16), ln_ffn[layer].reshape(1, D_MODEL), router, MIX_ROWS)
        xn, h2 = res[0].reshape(n, D_MODEL), res[1].reshape(n, D_MODEL)
        if layer % 2 == 0:
            ride = (not last and
                    (n // FFN_ROWS) * (FFN_DIM // FFN_COLS) == N_EXPERTS * (FFN_DIM // LANES))
            y, expert_bf16 = _ffn(h2, xn, dense_w1[i].astype(BF16), dense_w3[i].astype(BF16),
                                  dense_w2[i].astype(BF16), gfin, last, FFN_ROWS, FFN_COLS,
                                  (moe_w1[i], moe_w3[i]) if ride else (), (moe_w2[i],) if ride else ())
        else:
            if not expert_bf16:
                expert_bf16 = [w[i].astype(BF16) for w in (moe_w1, moe_w3, moe_w2)]
            y = _moe(h2, xn, res[4], res[3], res[2], *expert_bf16, gfin, last, MOE_TOKENS, MOE_COLS)
        x = y.reshape(b, s, D_MODEL)
    return x
```

```python
import functools

import jax
import jax.numpy as jnp
import numpy as np
from jax import lax
from jax.experimental import pallas as pl
from jax.experimental.pallas import tpu as pltpu

D_MODEL = 1024
FOURIER_GROUPS = 4
FOURIER_GROUP_DIM = 128
FOURIER_WIDTH = FOURIER_GROUPS * FOURIER_GROUP_DIM
DILATIONS = (1, 4, 16)
HALF_SPAN = 64
HEADS_PER_GROUP = 4
HEAD_DIM = 64
GROUP_WIDTH = HEADS_PER_GROUP * HEAD_DIM
ATTN_WIDTH = len(DILATIONS) * GROUP_WIDTH
ROPE_THETA = 10000.0
FFN_DIM = 3584
N_EXPERTS = 8
RMS_EPS = 1e-6
NEG_INF = -1e30
LOG2E = float(np.log2(np.e))
LN2 = float(np.log(2.0))
LANES = 128

COL_UF = 0
COL_Q = FOURIER_WIDTH
COL_K = COL_Q + ATTN_WIDTH
COL_V = COL_K + ATTN_WIDTH
COL_GF = COL_V + ATTN_WIDTH
COL_GA = COL_GF + D_MODEL
IN_COLS = COL_GA + D_MODEL

VMEM_LIMIT = 56 * 1024 * 1024
MOE_VMEM_LIMIT = 60 * 1024 * 1024

IN_PROJ_ROWS = 1024
SEQ_DFT_COLS = 256
MIX_ROWS = 512
FFN_ROWS = 1024
FFN_COLS = 512
MOE_TOKENS = 2048
MOE_COLS = 1792

BF16 = jnp.bfloat16
F32 = jnp.float32


def _params(semantics, vmem_limit=VMEM_LIMIT):
    return pltpu.CompilerParams(dimension_semantics=semantics, vmem_limit_bytes=vmem_limit)


def _rms(x, gain):
    ms = jnp.mean(x * x, axis=-1, keepdims=True)
    return x * lax.rsqrt(ms + RMS_EPS) * gain


def _in_proj_kernel(x_ref, g_ref, w_ref, cc_ref, sc_ref, cos_ref, sin_ref,
                    ab_ref, q0_ref, k0_ref, v0_ref, q1_ref, k1_ref, v1_ref, q2_ref, k2_ref, v2_ref,
                    scr_ref):
    tm = x_ref.shape[1]
    h = _rms(x_ref[0], g_ref[...]).astype(BF16)

    def proj(col, width):
        return jnp.dot(h, w_ref[:, col:col + width], preferred_element_type=F32)

    uf = proj(COL_UF, FOURIER_WIDTH).astype(BF16)
    ab_ref[0] = jnp.dot(uf, cc_ref[...], preferred_element_type=F32).astype(BF16)
    ab_ref[1] = jnp.dot(uf, sc_ref[...], preferred_element_type=F32).astype(BF16)

    cos = cos_ref[...]
    sin = sin_ref[...]

    def rope(t):
        t1, t2 = t[:, :LANES], t[:, LANES:]
        return jnp.concatenate([t1 * cos - t2 * sin, t2 * cos + t1 * sin], axis=-1)

    def put(ref, val, d):
        if d == 1:
            ref[0, 0] = val.astype(BF16)
            return
        for c in range(GROUP_WIDTH // LANES):
            scr_ref[c] = val[:, c * LANES:(c + 1) * LANES]
        for r in range(d):
            for c in range(GROUP_WIDTH // LANES):
                ref[0, r, :, c * LANES:(c + 1) * LANES] = scr_ref[c, pl.ds(r, tm // d, stride=d), :].astype(BF16)

    outs = ((q0_ref, k0_ref, v0_ref), (q1_ref, k1_ref, v1_ref), (q2_ref, k2_ref, v2_ref))
    for g, d in enumerate(DILATIONS):
        qr, kr, vr = outs[g]
        put(qr, rope(proj(COL_Q + g * GROUP_WIDTH, GROUP_WIDTH)) * (HEAD_DIM ** -0.5 * LOG2E), d)
        put(kr, rope(proj(COL_K + g * GROUP_WIDTH, GROUP_WIDTH)), d)
        put(vr, proj(COL_V + g * GROUP_WIDTH, GROUP_WIDTH), d)


def _in_proj(x, gain, w, cc, sc, cos_t, sin_t, tm):
    b, s, _ = x.shape
    grid = (b, s // tm)
    qkv_shapes, qkv_specs = [], []
    for d in DILATIONS:
        for _ in range(3):
            qkv_shapes.append(jax.ShapeDtypeStruct((b, d, s // d, GROUP_WIDTH), BF16))
            qkv_specs.append(pl.BlockSpec((1, d, tm // d, GROUP_WIDTH), lambda bi, si: (bi, 0, si, 0)))
    out_shape = [jax.ShapeDtypeStruct((2, s, b * FOURIER_WIDTH), BF16)] + qkv_shapes
    out_specs = [pl.BlockSpec((2, tm, FOURIER_WIDTH), lambda bi, si: (0, si, bi))] + qkv_specs
    in_specs = [
        pl.BlockSpec((1, tm, D_MODEL), lambda bi, si: (bi, si, 0)),
        pl.BlockSpec((1, D_MODEL), lambda bi, si: (0, 0)),
        pl.BlockSpec((D_MODEL, COL_GF), lambda bi, si: (0, 0)),
        pl.BlockSpec((FOURIER_WIDTH, FOURIER_WIDTH), lambda bi, si: (0, 0)),
        pl.BlockSpec((FOURIER_WIDTH, FOURIER_WIDTH), lambda bi, si: (0, 0)),
        pl.BlockSpec((tm, LANES), lambda bi, si: (si, 0)),
        pl.BlockSpec((tm, LANES), lambda bi, si: (si, 0)),
    ]
    return pl.pallas_call(
        _in_proj_kernel, grid=grid, in_specs=in_specs, out_specs=out_specs, out_shape=out_shape,
        scratch_shapes=[pltpu.VMEM((GROUP_WIDTH // LANES, tm, LANES), F32)],
        compiler_params=_params(("parallel", "parallel")), name="in_proj",
    )(x, gain, w, cc, sc, cos_t, sin_t)


SEQ_RADIX = 4


def _seq_dft_kernel(ab_ref, tab_ref, tw_ref, o_ref, u_ref):
    q_len = tab_ref.shape[0]
    for c in range(o_ref.shape[2] // LANES):
        cols = slice(c * LANES, (c + 1) * LANES)
        re = [ab_ref[0, j, :, cols].astype(F32) for j in range(SEQ_RADIX)]
        im = [-ab_ref[1, j, :, cols].astype(F32) for j in range(SEQ_RADIX)]
        p_re, p_im, q_re, q_im = re[0] + re[2], im[0] + im[2], re[0] - re[2], im[0] - im[2]
        s_re, s_im, d_re, d_im = re[1] + re[3], im[1] + im[3], re[1] - re[3], im[1] - im[3]
        terms = ((p_re + s_re, p_im + s_im), (q_re + d_im, q_im - d_re),
                 (p_re - s_re, p_im - s_im), (q_re - d_im, q_im + d_re))
        for r, (t_re, t_im) in enumerate(terms):
            if r > 0:
                cos, sin = tw_ref[r - 1, 0], tw_ref[r - 1, 1]
                t_re, t_im = t_re * cos + t_im * sin, t_im * cos - t_re * sin
            u_ref[r, :q_len, cols] = t_re.astype(BF16)
            u_ref[r, q_len:, cols] = t_im.astype(BF16)
    for r in range(SEQ_RADIX):
        o_ref[r] = jnp.dot(tab_ref[...], u_ref[r], preferred_element_type=F32).astype(o_ref.dtype)


def _seq_dft(ab, tab, tw, bn):
    _, s, n = ab.shape
    q_len = s // SEQ_RADIX
    bn = min(bn, n)
    return pl.pallas_call(
        _seq_dft_kernel, grid=(n // bn,),
        in_specs=[pl.BlockSpec((2, SEQ_RADIX, q_len, bn), lambda j: (0, 0, 0, j)),
                  pl.BlockSpec((q_len, 2 * q_len), lambda j: (0, 0), pipeline_mode=pl.Buffered(1)),
                  pl.BlockSpec((SEQ_RADIX - 1, 2, q_len, LANES), lambda j: (0, 0, 0, 0),
                               pipeline_mode=pl.Buffered(1))],
        out_specs=pl.BlockSpec((SEQ_RADIX, q_len, bn), lambda j: (0, 0, j)),
        out_shape=jax.ShapeDtypeStruct((SEQ_RADIX, q_len, n), BF16),
        scratch_shapes=[pltpu.VMEM((SEQ_RADIX, 2 * q_len, bn), BF16)],
        compiler_params=_params(("parallel",)), name="seq_dft",
    )(ab.reshape(2, SEQ_RADIX, q_len, n), tab, tw)


ATTN_UNROLL = 16


def _attn_kernel(q_ref, k_ref, v_ref, o_ref, lse_ref, *, qb, tk, seq_len, unroll):
    lane = lax.broadcasted_iota(jnp.int32, (1, GROUP_WIDTH), 1)
    qk_head = (lane % LANES) // (HEAD_DIM // 2)
    v_head = lane // HEAD_DIM
    col_minus_row = (lax.broadcasted_iota(jnp.int32, (qb, tk), 1)
                     - lax.broadcasted_iota(jnp.int32, (qb, tk), 0))
    n_res = q_ref.shape[1]
    n_sub = seq_len // qb

    def query_block(res, sub):
        m0 = pl.multiple_of(sub * qb, qb)
        start = pl.multiple_of(jnp.clip(m0 - HALF_SPAN, 0, seq_len - tk), 16)
        q = q_ref[0, res, pl.ds(m0, qb), :]
        k = k_ref[0, res, pl.ds(start, tk), :]
        v = v_ref[0, res, pl.ds(start, tk), :]
        valid = jnp.abs(col_minus_row + (start - m0)) <= HALF_SPAN
        q_heads = jnp.concatenate(
            [jnp.where(qk_head == hd, q, jnp.zeros_like(q)) for hd in range(HEADS_PER_GROUP)], axis=0)
        s = lax.dot_general(q_heads, k, (((1,), (1,)), ((), ())), preferred_element_type=F32)
        s = jnp.where(valid[None], s.reshape(HEADS_PER_GROUP, qb, tk), NEG_INF)
        m = jnp.max(s, axis=-1, keepdims=True)
        p = jnp.exp2(s - m)
        l = jnp.sum(p, axis=-1, keepdims=True)
        pv = jnp.dot(p.astype(BF16).reshape(HEADS_PER_GROUP * qb, tk), v, preferred_element_type=F32)
        pv = pv.reshape(HEADS_PER_GROUP, qb, GROUP_WIDTH) / l
        lse = (m + jnp.log2(l)) * LN2
        o_acc = pv[0]
        lse_acc = jnp.broadcast_to(lse[0], (qb, GROUP_WIDTH))
        for hd in range(1, HEADS_PER_GROUP):
            o_acc = jnp.where(v_head == hd, pv[hd], o_acc)
            lse_acc = jnp.where(v_head == hd, lse[hd], lse_acc)
        o_ref[0, res, pl.ds(m0, qb), :] = o_acc.astype(o_ref.dtype)
        lse_ref[0, res, pl.ds(m0, qb), :] = lse_acc

    def body(it, carry):
        for j in range(unroll):
            if n_sub % unroll == 0:
                per_res = n_sub // unroll
                query_block(it // per_res, (it % per_res) * unroll + j)
            else:
                query_block(it * (unroll // n_sub) + j // n_sub, j % n_sub)
        return carry

    lax.fori_loop(0, n_res * n_sub // unroll, body, 0)


def _attention(q, k, v):
    b, d, seq_len, _ = q.shape
    qb = min(128, seq_len)
    tk = min(qb + 2 * HALF_SPAN, seq_len)
    n_sub = seq_len // qb
    n_res = min(d, max(1, ATTN_UNROLL // n_sub))
    unroll = min(ATTN_UNROLL, n_res * n_sub)
    assert (n_res * n_sub) % unroll == 0 and (n_sub % unroll == 0 or unroll % n_sub == 0)
    kern = functools.partial(_attn_kernel, qb=qb, tk=tk, seq_len=seq_len, unroll=unroll)
    blk = pl.BlockSpec((1, n_res, seq_len, GROUP_WIDTH), lambda bi, ri: (bi, ri, 0, 0))
    shape = (b, d, seq_len, GROUP_WIDTH)
    return pl.pallas_call(
        kern, grid=(b, d // n_res), in_specs=[blk, blk, blk], out_specs=[blk, blk],
        out_shape=[jax.ShapeDtypeStruct(shape, BF16), jax.ShapeDtypeStruct(shape, F32)],
        compiler_params=_params(("parallel", "parallel")), name=f"attn_d{d}",
    )(q, k, v)


def _mix_out_kernel(*refs, routed):
    (x_ref, yf_ref, o0_ref, l0_ref, o1_ref, l1_ref, o2_ref, l2_ref, g1_ref, wg_ref,
     wf_ref, wa_ref, wo_ref, g2_ref) = refs[:14]
    if routed:
        rw_ref, rb_ref, xn_ref, h2_ref, cw_ref, sel_ref, sel_t_ref, o_scr, l_scr, f_scr = refs[14:]
    else:
        xn_ref, h2_ref, o_scr, l_scr, f_scr = refs[14:]
    tm = x_ref.shape[1]

    def gathered(ref, scr, d):
        if d == 1:
            return ref[0]
        n_chunk = ref.shape[-1] // LANES
        for r in range(d):
            for c in range(n_chunk):
                scr[c, pl.ds(r, tm // d, stride=d), :] = ref[r, :, c * LANES:(c + 1) * LANES].astype(F32)
        return jnp.concatenate([scr[c] for c in range(n_chunk)], axis=-1)

    h1 =_rms(x_ref[0], g1_ref[...]).astype(BF16)
    gate_f = jax.nn.sigmoid(jnp.dot(h1, wg_ref[:, :D_MODEL], preferred_element_type=F32))
    gate_a = jax.nn.sigmoid(jnp.dot(h1, wg_ref[:, D_MODEL:], preferred_element_type=F32))
    y_mix = gathered(yf_ref, f_scr, SEQ_RADIX).astype(BF16)
    y_f = jnp.dot(y_mix, wf_ref[...], preferred_element_type=F32)

    o_refs, l_refs = (o0_ref, o1_ref, o2_ref), (l0_ref, l1_ref, l2_ref)
    os_, ls_ = [], []
    for g, d in enumerate(DILATIONS):
        os_.append(gathered(o_refs[g].at[0], o_scr.at[g], d))
        ls_.append(gathered(l_refs[g].at[0], l_scr.at[g], d))
    mx = jnp.maximum(jnp.maximum(ls_[0], ls_[1]), ls_[2])
    es = [jnp.exp(l - mx) for l in ls_]
    den = es[0] + es[1] + es[2]
    o_att = (es[0] * os_[0] + es[1] * os_[1] + es[2] * os_[2]) / den
    y_a = jnp.dot(o_att.astype(BF16), wa_ref[...], preferred_element_type=F32)
    z = gate_f * y_f + gate_a * y_a
    xn = x_ref[0] +jnp.dot(z.astype(BF16), wo_ref[...], preferred_element_type=F32)
    xn_ref[0] = xn
    h2 = _rms(xn, g2_ref[...])
    h2_hi = h2.astype(BF16)
    h2_ref[0] = h2_hi

    if routed:
        h2_lo = (h2 - h2_hi.astype(F32)).astype(BF16)
        hi = jnp.dot(h2_hi, rw_ref[...], preferred_element_type=F32)
        logits = (hi[:, :LANES] + hi[:, LANES:]
                  + jnp.dot(h2_lo, rw_ref[:, :LANES], preferred_element_type=F32)) + rb_ref[...]
        lane = lax.broadcasted_iota(jnp.int32, logits.shape, 1)
        big = jnp.int32(LANES)
        m1 = jnp.max(logits, axis=-1, keepdims=True)
        i1 = jnp.min(jnp.where(logits == m1, lane, big), axis=-1, keepdims=True)
        rest = jnp.where(lane == i1, NEG_INF * 2, logits)
        m2 = jnp.max(rest, axis=-1, keepdims=True)
        i2 = jnp.min(jnp.where(rest == m2, lane, big), axis=-1, keepdims=True)
        e2 = jnp.exp(m2 - m1)
        w1 = 1.0 / (1.0 + e2)
        w2 = e2 / (1.0 + e2)
        cw_ref[...] = jnp.where(lane == i1, w1, jnp.where(lane == i2, w2, 0.0))
        sel = jnp.where((lane == i1) | (lane == i2), 1.0, 0.0)
        sel_ref[...] = sel
        sel_t_ref[...] = sel.T[:N_EXPERTS, :]


def _mix_out(x, yf, attn, g1, wg, wf, wa, wo, g2, router, tm):
    b, s, _ = x.shape
    routed = router is not None
    row = pl.BlockSpec((1, tm, D_MODEL), lambda bi, si: (bi, si, 0))
    full = lambda shape: pl.BlockSpec(shape, lambda bi, si: (0,) * len(shape), pipeline_mode=pl.Buffered(1))
    in_specs = [row, pl.BlockSpec((SEQ_RADIX, tm // SEQ_RADIX, FOURIER_WIDTH), lambda bi, si: (0, si, bi))]
    args = [x, yf]
    for d, (o, l) in zip(DILATIONS, attn):
        spec = pl.BlockSpec((1, d, tm // d, GROUP_WIDTH), lambda bi, si: (bi, 0, si, 0))
        in_specs += [spec, spec]
        args += [o, l]
    in_specs += [full((1, D_MODEL)), full((D_MODEL, 2 * D_MODEL)), full((FOURIER_WIDTH, D_MODEL)),
                 full((GROUP_WIDTH, D_MODEL)), full((D_MODEL, D_MODEL)), full((1, D_MODEL))]
    args += [g1, wg, wf, wa, wo, g2]
    out_shape = [jax.ShapeDtypeStruct((b, s, D_MODEL), F32), jax.ShapeDtypeStruct((b, s, D_MODEL), BF16)]
    out_specs = [row, row]
    if routed:
        in_specs += [full((D_MODEL, 2 * LANES)), full((1, LANES))]
        args += list(router)
        tok = pl.BlockSpec((tm, LANES), lambda bi, si: (bi * (s // tm) + si, 0))
        out_shape += [jax.ShapeDtypeStruct((b * s, LANES), F32)] * 2 + [jax.ShapeDtypeStruct((N_EXPERTS, b * s), F32)]
        out_specs += [tok, tok, pl.BlockSpec((N_EXPERTS, tm), lambda bi, si: (0, bi * (s // tm) + si))]
    return pl.pallas_call(
        functools.partial(_mix_out_kernel, routed=routed), grid=(b, s // tm),
        in_specs=in_specs, out_specs=out_specs, out_shape=out_shape,
        scratch_shapes=[pltpu.VMEM((len(DILATIONS), GROUP_WIDTH // LANES, tm, LANES), F32)] * 2
        + [pltpu.VMEM((FOURIER_WIDTH // LANES, tm, LANES), F32)],
        compiler_params=_params(("parallel", "parallel")), name="mix_out",
    )(*args)


def _swiglu(h, w1, w3, w2):
    a = jnp.dot(h, w1, preferred_element_type=F32)
    g = jnp.dot(h, w3, preferred_element_type=F32)
    act = (a * jax.nn.sigmoid(a) * g).astype(BF16)
    return jnp.dot(act, w2, preferred_element_type=F32)


def _ffn_kernel(h_ref, x_ref, w1_ref, w3_ref, w2_ref, gfin_ref, *rest, final_norm):
    o_ref = rest[len(rest) // 2]
    f = pl.program_id(1)

    @pl.when(f == 0)
    def _():
        o_ref[...] = x_ref[...]

    o_ref[...] += _swiglu(h_ref[...], w1_ref[...], w3_ref[...], w2_ref[...])

    if final_norm:
        @pl.when(f == pl.num_programs(1) - 1)
        def _():
            o_ref[...] = _rms(o_ref[...], gfin_ref[...])

    n_cast = len(rest) // 2
    for src, dst in zip(rest[:n_cast], rest[n_cast + 1:]):
        dst[...] = src[...].astype(dst.dtype)


def _ffn(h, x, w1, w3, w2, gfin, final_norm, tm, tf, cast_cols=(), cast_rows=()):
    n = h.shape[0]
    grid = (n // tm, FFN_DIM // tf)
    row = pl.BlockSpec((tm, D_MODEL), lambda i, f: (i, 0))
    in_specs = [row, row,
                pl.BlockSpec((D_MODEL, tf), lambda i, f: (0, f)),
                pl.BlockSpec((D_MODEL, tf), lambda i, f: (0, f)),
                pl.BlockSpec((tf, D_MODEL), lambda i, f: (f, 0)),
                pl.BlockSpec((1, D_MODEL), lambda i, f: (0, 0))]
    cast_specs, cast_shapes = [], []
    for arr, axis in [(a, 2) for a in cast_cols] + [(a, 1) for a in cast_rows]:
        per_expert = arr.shape[axis] // LANES
        assert arr.shape[0] * per_expert == grid[0] * grid[1], "one cast block per grid step"
        shape = (1, arr.shape[1], LANES) if axis == 2 else (1, LANES, arr.shape[2])

        def index(i, f, per_expert=per_expert, axis=axis):
            t = i * grid[1] + f
            return (t // per_expert, 0, t % per_expert) if axis == 2 else (t // per_expert, t % per_expert, 0)

        cast_specs.append(pl.BlockSpec(shape, index))
        cast_shapes.append(jax.ShapeDtypeStruct(arr.shape, BF16))
    outs = pl.pallas_call(
        functools.partial(_ffn_kernel, final_norm=final_norm),
        grid=grid, in_specs=in_specs + cast_specs, out_specs=[row] + cast_specs,
        out_shape=[jax.ShapeDtypeStruct((n, D_MODEL), F32)] + cast_shapes,
        compiler_params=_params(("arbitrary", "arbitrary")), name="ffn",
    )(h, x, w1, w3, w2, gfin, *cast_cols, *cast_rows)
    return outs[0], outs[1:]


MOE_SUB = 256
MOE_WIN = 128
MOE_ROWS = 128
MOE_TAIL = 64
ROW_ALIGN = 16


def _moe_kernel(h_ref, x_hbm, sel_t_ref, sel_ref, cw_ref, w1_ref, w3_ref, w2_ref, triu_ref, tril_ref, gfin_ref,
                o_ref, pos_t_ref, pos_ref, hs_ref, y_ref, pre_ref, x_sem, *, final_norm):
    tb = h_ref.shape[0]
    n_sub = tb // MOE_SUB
    blk, e, f = pl.program_id(0), pl.program_id(1), pl.program_id(2)
    n_e, n_f = pl.num_programs(1), pl.num_programs(2)

    @pl.when((blk == 0) & (e == 0) & (f == 0))
    def _():
        hs_ref[...] = jnp.zeros_like(hs_ref)
        y_ref[...] = jnp.zeros_like(y_ref)

    @pl.when((e == 0) & (f == 0))
    def _():
        x_copy = pltpu.make_async_copy(x_hbm.at[pl.ds(pl.multiple_of(blk * tb, tb), tb), :], o_ref, x_sem)
        x_copy.start()
        off_t = jnp.zeros((N_EXPERTS, 1), F32)
        off = jnp.zeros((1, LANES), F32)
        for s in range(n_sub):
            sl = slice(s * MOE_SUB, (s + 1) * MOE_SUB)
            sel_t = sel_t_ref[:, sl]
            within_t = jnp.dot(sel_t.astype(BF16), triu_ref[...], preferred_element_type=F32)
            pos_t_ref[:, sl] = jnp.where(sel_t > 0.0, within_t + off_t, -1.0)
            off_t = off_t + jnp.sum(sel_t, axis=1, keepdims=True)
            sel = sel_ref[sl, :]
            within = jnp.dot(tril_ref[...], sel.astype(BF16), preferred_element_type=F32)
            pos_ref[sl, :] = jnp.where(sel > 0.0, within + off, -1.0)
            off = off + jnp.sum(sel, axis=0, keepdims=True)
        x_copy.wait()

    def window(s):
        p0, p1 = pre_ref[s], pre_ref[s + 1]
        base = (p0 // ROW_ALIGN) * ROW_ALIGN
        return p0, base, (p1 - base + MOE_WIN - 1) // MOE_WIN

    @pl.when(f == 0)
    def _():
        run = jnp.int32(0)
        pre_ref[0] = run
        for s in range(n_sub):
            cnt = jnp.sum(sel_t_ref[pl.ds(e, 1), s * MOE_SUB:(s + 1) * MOE_SUB])
            run = run + cnt.astype(jnp.int32)
            pre_ref[s + 1] = run
        row = lax.broadcasted_iota(jnp.int32, (MOE_WIN, MOE_SUB), 0)
        row1 = lax.broadcasted_iota(jnp.int32, (MOE_WIN, 1), 0)

        def gather(s, j):
            p0, p1 = pre_ref[s], pre_ref[s + 1]
            _, base, _ = window(s)
            sl = slice(s * MOE_SUB, (s + 1) * MOE_SUB)
            r0 = pl.multiple_of(base + j * MOE_WIN, ROW_ALIGN)
            pos_i = pos_t_ref[pl.ds(e, 1), sl].astype(jnp.int32)
            one_hot = jnp.where(row == pos_i - r0, 1.0, 0.0).astype(BF16)
            rows = jnp.dot(one_hot, h_ref[sl, :], preferred_element_type=F32).astype(BF16)
            mine = (row1 >= p0 - r0) & (row1 < p1 - r0)
            hs_ref[pl.ds(r0, MOE_WIN), :] = jnp.where(mine, rows, hs_ref[pl.ds(r0, MOE_WIN), :])

        for s in range(n_sub):
            gather(s, 0)
        for s in range(n_sub):
            lax.fori_loop(1, window(s)[2], lambda j, c, s=s: (gather(s, j), c)[1], 0)

    full = 4 * MOE_ROWS
    n_total = pre_ref[n_sub]
    n_whole = n_total // full
    tail = n_total - n_whole * full
    widen = (tail > 0) & (tail <= MOE_TAIL) & (n_whole > 0)
    n_full = n_whole - widen.astype(jnp.int32)
    rest_start = n_whole * full
    n_units = jnp.where(widen, 0, (tail + MOE_ROWS - 1) // MOE_ROWS)
    rem = n_units % 4

    def expert_rows(r0, n_rows):
        y = _swiglu(hs_ref[pl.ds(r0, n_rows), :], w1_ref[0], w3_ref[0], w2_ref[0])
        y_ref[pl.ds(r0, n_rows), :] = jnp.where(f == 0, y, y_ref[pl.ds(r0, n_rows), :] + y)

    def full_chunk(c, carry):
        expert_rows(pl.multiple_of(c * full, full), full)
        return carry

    lax.fori_loop(0, n_full, full_chunk, 0)

    @pl.when(widen)
    def _():
        expert_rows(pl.multiple_of(n_full * full, full), full + MOE_TAIL)

    @pl.when(n_units == 4)
    def _():
        expert_rows(pl.multiple_of(rest_start, full), full)

    @pl.when((rem == 2) | (rem == 3))
    def _():
        expert_rows(pl.multiple_of(rest_start, 2 * MOE_ROWS), 2 * MOE_ROWS)

    @pl.when(rem % 2 == 1)
    def _():
        expert_rows(pl.multiple_of(rest_start + (n_units - 1) * MOE_ROWS, MOE_ROWS), MOE_ROWS)

    @pl.when(f == n_f - 1)
    def _():
        lane = lax.broadcasted_iota(jnp.int32, (MOE_SUB, LANES), 1)

        def scatter(s, j):
            _, base, _ = window(s)
            sl = slice(s * MOE_SUB, (s + 1) * MOE_SUB)
            r0 = pl.multiple_of(base + j * MOE_WIN, ROW_ALIGN)
            mine = lane == e
            pos_i = jnp.sum(jnp.where(mine, pos_ref[sl, :], 0.0), axis=1, keepdims=True).astype(jnp.int32)
            weight = jnp.sum(jnp.where(mine, cw_ref[sl, :], 0.0), axis=1, keepdims=True)
            one_hot = jnp.where(lane == pos_i - r0, 1.0, 0.0).astype(BF16)
            rows = y_ref[pl.ds(r0, MOE_WIN), :].astype(BF16)
            back = jnp.dot(one_hot, rows, preferred_element_type=F32)
            o_ref[sl, :] += back * weight

        for s in range(n_sub):
            scatter(s, 0)
        for s in range(n_sub):
            lax.fori_loop(1, window(s)[2], lambda j, c, s=s: (scatter(s, j), c)[1], 0)

        if final_norm:
            @pl.when(e == n_e - 1)
            def _():
                o_ref[...] = _rms(o_ref[...], gfin_ref[...])


def _moe(h, x, sel_t, sel, cw, w1, w3, w2, gfin, final_norm, tb, tf):
    n = h.shape[0]
    tb = min(tb, n)
    n_exp = w1.shape[0]
    triu = jnp.triu(jnp.ones((MOE_SUB, MOE_SUB), F32), k=1).astype(BF16)
    pad = 2 * MOE_WIN
    once = pl.Buffered(1)
    tok = pl.BlockSpec((tb, LANES), lambda i, e, f: (i, 0), pipeline_mode=once)
    tri = pl.BlockSpec((MOE_SUB, MOE_SUB), lambda i, e, f: (0, 0), pipeline_mode=once)
    in_specs = [pl.BlockSpec((tb, D_MODEL), lambda i, e, f: (i, 0), pipeline_mode=once),
                pl.BlockSpec(memory_space=pl.ANY),
                pl.BlockSpec((N_EXPERTS, tb), lambda i, e, f: (0, i), pipeline_mode=once),
                tok, tok,
                pl.BlockSpec((1, D_MODEL, tf), lambda i, e, f: (e, 0, f)),
                pl.BlockSpec((1, D_MODEL, tf), lambda i, e, f: (e, 0, f)),
                pl.BlockSpec((1, tf, D_MODEL), lambda i, e, f: (e, f, 0)),
                tri, tri,
                pl.BlockSpec((1, D_MODEL), lambda i, e, f: (0, 0), pipeline_mode=once)]
    return pl.pallas_call(
        functools.partial(_moe_kernel, final_norm=final_norm),
        grid=(n // tb, n_exp, FFN_DIM // tf), in_specs=in_specs,
        out_specs=pl.BlockSpec((tb, D_MODEL), lambda i, e, f: (i, 0), pipeline_mode=once),
        out_shape=jax.ShapeDtypeStruct((n, D_MODEL), F32),
        scratch_shapes=[pltpu.VMEM((N_EXPERTS, tb), F32),
                        pltpu.VMEM((tb, LANES), F32),
                        pltpu.VMEM((tb + pad, D_MODEL), BF16),
                        pltpu.VMEM((tb + pad, D_MODEL), F32),
                        pltpu.SMEM((tb // MOE_SUB + 1,), jnp.int32),
                        pltpu.SemaphoreType.DMA(())],
        compiler_params=_params(("arbitrary", "arbitrary", "arbitrary"), MOE_VMEM_LIMIT), name="moe",
    )(h, x, sel_t, sel, cw, w1, w3, w2, triu, triu.T, gfin)


def _qk_column_order():
    half = HEAD_DIM // 2
    order = []
    for g in range(len(DILATIONS)):
        base = g * GROUP_WIDTH
        for part in range(2):
            for hd in range(HEADS_PER_GROUP):
                order.extend(range(base + hd * HEAD_DIM + part * half, base + hd * HEAD_DIM + (part + 1) * half))
    return np.asarray(order, np.int32)


def _dft_tables(n):
    r = 32 if n % 32 == 0 else 1
    col = jnp.arange(n, dtype=jnp.int32)[None, :]

    def trig(rows):
        ang = ((rows[:, None] * col) % n).astype(F32) * (2.0 * np.pi / n)
        return jnp.cos(ang), jnp.sin(ang)

    ca, sa = trig(jnp.arange(n // r, dtype=jnp.int32) * r)
    cb, sb = trig(jnp.arange(r, dtype=jnp.int32))
    ca, sa, cb, sb = ca[:, None, :], sa[:, None, :], cb[None, :, :], sb[None, :, :]
    return (ca * cb - sa * sb).reshape(n, n), (sa * cb + ca * sb).reshape(n, n)


def _tables(s):
    half = HEAD_DIM // 2
    inv_freq = ROPE_THETA ** (-jnp.arange(half, dtype=F32) * 2.0 / HEAD_DIM)
    ang = jnp.arange(s, dtype=F32)[:, None] * inv_freq[None, :]
    cos_t = jnp.tile(jnp.cos(ang), (1, LANES // half))
    sin_t = jnp.tile(jnp.sin(ang), (1, LANES // half))
    cg, sg = _dft_tables(FOURIER_GROUP_DIM)
    eye = jnp.eye(FOURIER_GROUPS, dtype=F32)
    scale_c = FOURIER_GROUP_DIM ** -0.5
    cc = (jnp.kron(eye, cg) * scale_c).astype(BF16)
    sc = (jnp.kron(eye, sg) * scale_c).astype(BF16)
    q_len = s // SEQ_RADIX
    cq, sq = _dft_tables(q_len)
    seq_tab = (jnp.concatenate([cq, sq], axis=1) * s ** -0.5).astype(BF16)
    pos = jnp.arange(q_len, dtype=jnp.int32)[None, :]
    ang = ((jnp.arange(1, SEQ_RADIX, dtype=jnp.int32)[:, None] * pos) % s).astype(F32) * (2.0 * np.pi / s)
    seq_tw = jnp.broadcast_to(jnp.stack([jnp.cos(ang), jnp.sin(ang)], axis=1)[..., None],
                              (SEQ_RADIX - 1, 2, q_len, LANES))
    return cos_t, sin_t, cc, sc, seq_tab, seq_tw


def kernel(x, ln_mix, w_in, w_fourier, w_attn, w_out, ln_ffn, dense_w1, dense_w3, dense_w2,
           router_w, router_b, moe_w1, moe_w3, moe_w2, ln_final):
    b, s, _ = x.shape
    depth = w_in.shape[0]
    n = b * s
    cos_t, sin_t, cc, sc, seq_tab, seq_tw = _tables(s)
    order = _qk_column_order()
    gfin = ln_final.reshape(1, D_MODEL)
    expert_bf16 = []

    for layer in range(depth):
        w = w_in[layer]
        wq = w[:, COL_Q:COL_K][:, order]
        wk = w[:, COL_K:COL_V][:, order]
        w_l = jnp.concatenate([w[:, :COL_Q], wq, wk, w[:, COL_V:COL_GF]], axis=1).astype(BF16)
        w_gate = w[:, COL_GF:].astype(BF16)
        g1 = ln_mix[layer].reshape(1, D_MODEL)

        outs = _in_proj(x, g1, w_l, cc, sc, cos_t, sin_t, min(IN_PROJ_ROWS, s))
        ab, qkv = outs[0], outs[1:10]
        yf = _seq_dft(ab, seq_tab, seq_tw, SEQ_DFT_COLS)
        attn = [_attention(qkv[3 * g], qkv[3 * g + 1], qkv[3 * g + 2]) for g in range(len(DILATIONS))]

        last = layer == depth - 1
        i = layer // 2
        if layer % 2 == 0:
            router = None
        else:
            rw = jnp.zeros((D_MODEL, LANES), F32).at[:, :N_EXPERTS].set(router_w[i])
            rw_hi = rw.astype(BF16)
            rw_lo = (rw - rw_hi.astype(F32)).astype(BF16)
            rb = jnp.full((1, LANES), NEG_INF, F32).at[0, :N_EXPERTS].set(router_b[i])
            router = (jnp.concatenate([rw_hi, rw_lo], axis=1), rb)
        res = _mix_out(x, yf, attn, g1, w_gate, w_fourier[layer].astype(BF16), w_attn[layer].astype(BF16),
                       w_out[layer].astype(BF16), ln_ffn[layer].reshape(1, D_MODEL), router, MIX_ROWS)
        xn, h2 = res[0].reshape(n, D_MODEL), res[1].reshape(n, D_MODEL)
        if layer % 2 == 0:
            ride = (not last and
                    (n // FFN_ROWS) * (FFN_DIM // FFN_COLS) == N_EXPERTS * (FFN_DIM // LANES))
            y, expert_bf16 = _ffn(h2, xn, dense_w1[i].astype(BF16), dense_w3[i].astype(BF16),
                                  dense_w2[i].astype(BF16), gfin, last, FFN_ROWS, FFN_COLS,
                                  (moe_w1[i], moe_w3[i]) if ride else (), (moe_w2[i],) if ride else ())
        else:
            if not expert_bf16:
                expert_bf16 = [w[i].astype(BF16) for w in (moe_w1, moe_w3, moe_w2)]
            y = _moe(h2, xn, res[4], res[3], res[2], *expert_bf16, gfin, last, MOE_TOKENS, MOE_COLS)
        x = y.reshape(b, s, D_MODEL)
    return x
```

```python
import functools

import jax
import jax.numpy as jnp
import numpy as np
from jax import lax
from jax.experimental import pallas as pl
from jax.experimental.pallas import tpu as pltpu

D_MODEL = 1024
FOURIER_GROUPS = 4
FOURIER_GROUP_DIM = 128
FOURIER_WIDTH = FOURIER_GROUPS * FOURIER_GROUP_DIM
DILATIONS = (1, 4, 16)
HALF_SPAN = 64
HEADS_PER_GROUP = 4
HEAD_DIM = 64
GROUP_WIDTH = HEADS_PER_GROUP * HEAD_DIM
ATTN_WIDTH = len(DILATIONS) * GROUP_WIDTH
ROPE_THETA = 10000.0
FFN_DIM = 3584
N_EXPERTS = 8
RMS_EPS = 1e-6
NEG_INF = -1e30
LOG2E = float(np.log2(np.e))
LN2 = float(np.log(2.0))
LANES = 128

COL_UF = 0
COL_Q = FOURIER_WIDTH
COL_K = COL_Q + ATTN_WIDTH
COL_V = COL_K + ATTN_WIDTH
COL_GF = COL_V + ATTN_WIDTH
COL_GA = COL_GF + D_MODEL
IN_COLS = COL_GA + D_MODEL

VMEM_LIMIT = 56 * 1024 * 1024
MOE_VMEM_LIMIT = 60 * 1024 * 1024

IN_PROJ_ROWS = 1024
SEQ_DFT_COLS = 256
MIX_ROWS = 512
FFN_ROWS = 1024
FFN_COLS = 512
MOE_TOKENS = 2048
MOE_COLS = 1792

BF16 = jnp.bfloat16
F32 = jnp.float32


def _params(semantics, vmem_limit=VMEM_LIMIT):
    return pltpu.CompilerParams(dimension_semantics=semantics, vmem_limit_bytes=vmem_limit)


def _rms(x, gain):
    ms = jnp.mean(x * x, axis=-1, keepdims=True)
    return x * lax.rsqrt(ms + RMS_EPS) * gain


def _in_proj_kernel(x_ref, g_ref, w_ref, cc_ref, sc_ref, cos_ref, sin_ref,
                    ab_ref, q0_ref, k0_ref, v0_ref, q1_ref, k1_ref, v1_ref, q2_ref, k2_ref, v2_ref,
                    scr_ref):
    tm = x_ref.shape[1]
    h = _rms(x_ref[0], g_ref[...]).astype(BF16)

    def proj(col, width):
        return jnp.dot(h, w_ref[:, col:col + width], preferred_element_type=F32)

    uf = proj(COL_UF, FOURIER_WIDTH).astype(BF16)
    ab_ref[0] = jnp.dot(uf, cc_ref[...], preferred_element_type=F32).astype(BF16)
    ab_ref[1] = jnp.dot(uf, sc_ref[...], preferred_element_type=F32).astype(BF16)

    cos = cos_ref[...]
    sin = sin_ref[...]

    def rope(t):
        t1, t2 = t[:, :LANES], t[:, LANES:]
        return jnp.concatenate([t1 * cos - t2 * sin, t2 * cos + t1 * sin], axis=-1)

    def put(ref, val, d):
        if d == 1:
            ref[0, 0] = val.astype(BF16)
            return
        for c in range(GROUP_WIDTH // LANES):
            scr_ref[c] = val[:, c * LANES:(c + 1) * LANES]
        for r in range(d):
            for c in range(GROUP_WIDTH // LANES):
                ref[0, r, :, c * LANES:(c + 1) * LANES] = scr_ref[c, pl.ds(r, tm // d, stride=d), :].astype(BF16)

    outs = ((q0_ref, k0_ref, v0_ref), (q1_ref, k1_ref, v1_ref), (q2_ref, k2_ref, v2_ref))
    for g, d in enumerate(DILATIONS):
        qr, kr, vr = outs[g]
        put(qr, rope(proj(COL_Q + g * GROUP_WIDTH, GROUP_WIDTH)) * (HEAD_DIM ** -0.5 * LOG2E), d)
        put(kr, rope(proj(COL_K + g * GROUP_WIDTH, GROUP_WIDTH)), d)
        put(vr, proj(COL_V + g * GROUP_WIDTH, GROUP_WIDTH), d)


def _in_proj(x, gain, w, cc, sc, cos_t, sin_t, tm):
    b, s, _ = x.shape
    grid = (b, s // tm)
    qkv_shapes, qkv_specs = [], []
    for d in DILATIONS:
        for _ in range(3):
            qkv_shapes.append(jax.ShapeDtypeStruct((b, d, s // d, GROUP_WIDTH), BF16))
            qkv_specs.append(pl.BlockSpec((1, d, tm // d, GROUP_WIDTH), lambda bi, si: (bi, 0, si, 0)))
    out_shape = [jax.ShapeDtypeStruct((2, s, b * FOURIER_WIDTH), BF16)] + qkv_shapes
    out_specs = [pl.BlockSpec((2, tm, FOURIER_WIDTH), lambda bi, si: (0, si, bi))] + qkv_specs
    in_specs = [
        pl.BlockSpec((1, tm, D_MODEL), lambda bi, si: (bi, si, 0)),
        pl.BlockSpec((1, D_MODEL), lambda bi, si: (0, 0)),
        pl.BlockSpec((D_MODEL, COL_GF), lambda bi, si: (0, 0)),
        pl.BlockSpec((FOURIER_WIDTH, FOURIER_WIDTH), lambda bi, si: (0, 0)),
        pl.BlockSpec((FOURIER_WIDTH, FOURIER_WIDTH), lambda bi, si: (0, 0)),
        pl.BlockSpec((tm, LANES), lambda bi, si: (si, 0)),
        pl.BlockSpec((tm, LANES), lambda bi, si: (si, 0)),
    ]
    return pl.pallas_call(
        _in_proj_kernel, grid=grid, in_specs=in_specs, out_specs=out_specs, out_shape=out_shape,
        scratch_shapes=[pltpu.VMEM((GROUP_WIDTH // LANES, tm, LANES), F32)],
        compiler_params=_params(("parallel", "parallel")), name="in_proj",
    )(x, gain, w, cc, sc, cos_t, sin_t)


SEQ_RADIX = 4


def _seq_dft_kernel(ab_ref, tab_ref, tw_ref, o_ref, u_ref):
    q_len = tab_ref.shape[0]
    for c in range(o_ref.shape[2] // LANES):
        cols = slice(c * LANES, (c + 1) * LANES)
        re = [ab_ref[0, j, :, cols].astype(F32) for j in range(SEQ_RADIX)]
        im = [-ab_ref[1, j, :, cols].astype(F32) for j in range(SEQ_RADIX)]
        p_re, p_im, q_re, q_im = re[0] + re[2], im[0] + im[2], re[0] - re[2], im[0] - im[2]
        s_re, s_im, d_re, d_im = re[1] + re[3], im[1] + im[3], re[1] - re[3], im[1] - im[3]
        terms = ((p_re + s_re, p_im + s_im), (q_re + d_im, q_im - d_re),
                 (p_re - s_re, p_im - s_im), (q_re - d_im, q_im + d_re))
        for r, (t_re, t_im) in enumerate(terms):
            if r > 0:
                cos, sin = tw_ref[r - 1, 0], tw_ref[r - 1, 1]
                t_re, t_im = t_re * cos + t_im * sin, t_im * cos - t_re * sin
            u_ref[r, :q_len, cols] = t_re.astype(BF16)
            u_ref[r, q_len:, cols] = t_im.astype(BF16)
    for r in range(SEQ_RADIX):
        o_ref[r] = jnp.dot(tab_ref[...], u_ref[r], preferred_element_type=F32).astype(o_ref.dtype)


def _seq_dft(ab, tab, tw, bn):
    _, s, n = ab.shape
    q_len = s // SEQ_RADIX
    bn = min(bn, n)
    return pl.pallas_call(
        _seq_dft_kernel, grid=(n // bn,),
        in_specs=[pl.BlockSpec((2, SEQ_RADIX, q_len, bn), lambda j: (0, 0, 0, j)),
                  pl.BlockSpec((q_len, 2 * q_len), lambda j: (0, 0), pipeline_mode=pl.Buffered(1)),
                  pl.BlockSpec((SEQ_RADIX - 1, 2, q_len, LANES), lambda j: (0, 0, 0, 0),
                               pipeline_mode=pl.Buffered(1))],
        out_specs=pl.BlockSpec((SEQ_RADIX, q_len, bn), lambda j: (0, 0, j)),
        out_shape=jax.ShapeDtypeStruct((SEQ_RADIX, q_len, n), BF16),
        scratch_shapes=[pltpu.VMEM((SEQ_RADIX, 2 * q_len, bn), BF16)],
        compiler_params=_params(("parallel",)), name="seq_dft",
    )(ab.reshape(2, SEQ_RADIX, q_len, n), tab, tw)


ATTN_UNROLL = 32


def _attn_kernel(q_ref, k_ref, v_ref, o_ref, lse_ref, *, qb, tk, seq_len, unroll):
    lane = lax.broadcasted_iota(jnp.int32, (1, GROUP_WIDTH), 1)
    qk_head = (lane % LANES) // (HEAD_DIM // 2)
    v_head = lane // HEAD_DIM
    col_minus_row = (lax.broadcasted_iota(jnp.int32, (qb, tk), 1)
                     - lax.broadcasted_iota(jnp.int32, (qb, tk), 0))
    n_res = q_ref.shape[1]
    n_sub = seq_len // qb

    def query_block(res, sub):
        m0 = pl.multiple_of(sub * qb, qb)
        start = pl.multiple_of(jnp.clip(m0 - HALF_SPAN, 0, seq_len - tk), 16)
        q = q_ref[0, res, pl.ds(m0, qb), :]
        k = k_ref[0, res, pl.ds(start, tk), :]
        v = v_ref[0, res, pl.ds(start, tk), :]
        valid = jnp.abs(col_minus_row + (start - m0)) <= HALF_SPAN
        q_heads = jnp.concatenate(
            [jnp.where(qk_head == hd, q, jnp.zeros_like(q)) for hd in range(HEADS_PER_GROUP)], axis=0)
        s = lax.dot_general(q_heads, k, (((1,), (1,)), ((), ())), preferred_element_type=F32)
        s = jnp.where(valid[None], s.reshape(HEADS_PER_GROUP, qb, tk), NEG_INF)
        m = jnp.max(s, axis=-1, keepdims=True)
        p = jnp.exp2(s - m)
        l = jnp.sum(p, axis=-1, keepdims=True)
        pv = jnp.dot(p.astype(BF16).reshape(HEADS_PER_GROUP * qb, tk), v, preferred_element_type=F32)
        pv = pv.reshape(HEADS_PER_GROUP, qb, GROUP_WIDTH) / l
        lse = (m + jnp.log2(l)) * LN2
        o_acc = pv[0]
        lse_acc = jnp.broadcast_to(lse[0], (qb, GROUP_WIDTH))
        for hd in range(1, HEADS_PER_GROUP):
            o_acc = jnp.where(v_head == hd, pv[hd], o_acc)
            lse_acc = jnp.where(v_head == hd, lse[hd], lse_acc)
        o_ref[0, res, pl.ds(m0, qb), :] = o_acc.astype(o_ref.dtype)
        lse_ref[0, res, pl.ds(m0, qb), :] = lse_acc

    def body(it, carry):
        for j in range(unroll):
            if n_sub % unroll == 0:
                per_res = n_sub // unroll
                query_block(it // per_res, (it % per_res) * unroll + j)
            else:
                query_block(it * (unroll // n_sub) + j // n_sub, j % n_sub)
        return carry

    lax.fori_loop(0, n_res * n_sub // unroll, body, 0)


def _attention(q, k, v):
    b, d, seq_len, _ = q.shape
    qb = min(128, seq_len)
    tk = min(qb + 2 * HALF_SPAN, seq_len)
    n_sub = seq_len // qb
    n_res = min(d, max(1, ATTN_UNROLL // n_sub))
    unroll = min(ATTN_UNROLL, n_res * n_sub)
    assert (n_res * n_sub) % unroll == 0 and (n_sub % unroll == 0 or unroll % n_sub == 0)
    kern = functools.partial(_attn_kernel, qb=qb, tk=tk, seq_len=seq_len, unroll=unroll)
    blk = pl.BlockSpec((1, n_res, seq_len, GROUP_WIDTH), lambda bi, ri: (bi, ri, 0, 0))
    shape = (b, d, seq_len, GROUP_WIDTH)
    return pl.pallas_call(
        kern, grid=(b, d // n_res), in_specs=[blk, blk, blk], out_specs=[blk, blk],
        out_shape=[jax.ShapeDtypeStruct(shape, BF16), jax.ShapeDtypeStruct(shape, F32)],
        compiler_params=_params(("parallel", "parallel")), name=f"attn_d{d}",
    )(q, k, v)


def _mix_out_kernel(*refs, routed):
    (x_ref, yf_ref, o0_ref, l0_ref, o1_ref, l1_ref, o2_ref, l2_ref, g1_ref, wg_ref,
     wf_ref, wa_ref, wo_ref, g2_ref) = refs[:14]
    if routed:
        rw_ref, rb_ref, xn_ref, h2_ref, cw_ref, sel_ref, sel_t_ref, o_scr, l_scr, f_scr = refs[14:]
    else:
        xn_ref, h2_ref, o_scr, l_scr, f_scr = refs[14:]
    tm = x_ref.shape[1]

    def gathered(ref, scr, d):
        if d == 1:
            return ref[0]
        n_chunk = ref.shape[-1] // LANES
        for r in range(d):
            for c in range(n_chunk):
                scr[c, pl.ds(r, tm // d, stride=d), :] = ref[r, :, c * LANES:(c + 1) * LANES].astype(F32)
        return jnp.concatenate([scr[c] for c in range(n_chunk)], axis=-1)

    h1 =_rms(x_ref[0], g1_ref[...]).astype(BF16)
    gate_f = jax.nn.sigmoid(jnp.dot(h1, wg_ref[:, :D_MODEL], preferred_element_type=F32))
    gate_a = jax.nn.sigmoid(jnp.dot(h1, wg_ref[:, D_MODEL:], preferred_element_type=F32))
    y_mix = gathered(yf_ref, f_scr, SEQ_RADIX).astype(BF16)
    y_f = jnp.dot(y_mix, wf_ref[...], preferred_element_type=F32)

    o_refs, l_refs = (o0_ref, o1_ref, o2_ref), (l0_ref, l1_ref, l2_ref)
    os_, ls_ = [], []
    for g, d in enumerate(DILATIONS):
        os_.append(gathered(o_refs[g].at[0], o_scr.at[g], d))
        ls_.append(gathered(l_refs[g].at[0], l_scr.at[g], d))
    mx = jnp.maximum(jnp.maximum(ls_[0], ls_[1]), ls_[2])
    es = [jnp.exp(l - mx) for l in ls_]
    den = es[0] + es[1] + es[2]
    o_att = (es[0] * os_[0] + es[1] * os_[1] + es[2] * os_[2]) / den
    y_a = jnp.dot(o_att.astype(BF16), wa_ref[...], preferred_element_type=F32)
    z = gate_f * y_f + gate_a * y_a
    xn = x_ref[0] +jnp.dot(z.astype(BF16), wo_ref[...], preferred_element_type=F32)
    xn_ref[0] = xn
    h2 = _rms(xn, g2_ref[...])
    h2_hi = h2.astype(BF16)
    h2_ref[0] = h2_hi

    if routed:
        h2_lo = (h2 - h2_hi.astype(F32)).astype(BF16)
        hi = jnp.dot(h2_hi, rw_ref[...], preferred_element_type=F32)
        logits = (hi[:, :LANES] + hi[:, LANES:]
                  + jnp.dot(h2_lo, rw_ref[:, :LANES], preferred_element_type=F32)) + rb_ref[...]
        lane = lax.broadcasted_iota(jnp.int32, logits.shape, 1)
        big = jnp.int32(LANES)
        m1 = jnp.max(logits, axis=-1, keepdims=True)
        i1 = jnp.min(jnp.where(logits == m1, lane, big), axis=-1, keepdims=True)
        rest = jnp.where(lane == i1, NEG_INF * 2, logits)
        m2 = jnp.max(rest, axis=-1, keepdims=True)
        i2 = jnp.min(jnp.where(rest == m2, lane, big), axis=-1, keepdims=True)
        e2 = jnp.exp(m2 - m1)
        w1 = 1.0 / (1.0 + e2)
        w2 = e2 / (1.0 + e2)
        cw_ref[...] = jnp.where(lane == i1, w1, jnp.where(lane == i2, w2, 0.0))
        sel = jnp.where((lane == i1) | (lane == i2), 1.0, 0.0)
        sel_ref[...] = sel
        sel_t_ref[...] = sel.T[:N_EXPERTS, :]


def _mix_out(x, yf, attn, g1, wg, wf, wa, wo, g2, router, tm):
    b, s, _ = x.shape
    routed = router is not None
    row = pl.BlockSpec((1, tm, D_MODEL), lambda bi, si: (bi, si, 0))
    full = lambda shape: pl.BlockSpec(shape, lambda bi, si: (0,) * len(shape), pipeline_mode=pl.Buffered(1))
    in_specs = [row, pl.BlockSpec((SEQ_RADIX, tm // SEQ_RADIX, FOURIER_WIDTH), lambda bi, si: (0, si, bi))]
    args = [x, yf]
    for d, (o, l) in zip(DILATIONS, attn):
        spec = pl.BlockSpec((1, d, tm // d, GROUP_WIDTH), lambda bi, si: (bi, 0, si, 0))
        in_specs += [spec, spec]
        args += [o, l]
    in_specs += [full((1, D_MODEL)), full((D_MODEL, 2 * D_MODEL)), full((FOURIER_WIDTH, D_MODEL)),
                 full((GROUP_WIDTH, D_MODEL)), full((D_MODEL, D_MODEL)), full((1, D_MODEL))]
    args += [g1, wg, wf, wa, wo, g2]
    out_shape = [jax.ShapeDtypeStruct((b, s, D_MODEL), F32), jax.ShapeDtypeStruct((b, s, D_MODEL), BF16)]
    out_specs = [row, row]
    if routed:
        in_specs += [full((D_MODEL, 2 * LANES)), full((1, LANES))]
        args += list(router)
        tok = pl.BlockSpec((tm, LANES), lambda bi, si: (bi * (s // tm) + si, 0))
        out_shape += [jax.ShapeDtypeStruct((b * s, LANES), F32)] * 2 + [jax.ShapeDtypeStruct((N_EXPERTS, b * s), F32)]
        out_specs += [tok, tok, pl.BlockSpec((N_EXPERTS, tm), lambda bi, si: (0, bi * (s // tm) + si))]
    return pl.pallas_call(
        functools.partial(_mix_out_kernel, routed=routed), grid=(b, s // tm),
        in_specs=in_specs, out_specs=out_specs, out_shape=out_shape,
        scratch_shapes=[pltpu.VMEM((len(DILATIONS), GROUP_WIDTH // LANES, tm, LANES), F32)] * 2
        + [pltpu.VMEM((FOURIER_WIDTH // LANES, tm, LANES), F32)],
        compiler_params=_params(("parallel", "parallel")), name="mix_out",
    )(*args)


def _swiglu(h, w1, w3, w2):
    a = jnp.dot(h, w1, preferred_element_type=F32)
    g = jnp.dot(h, w3, preferred_element_type=F32)
    act = (a * jax.nn.sigmoid(a) * g).astype(BF16)
    return jnp.dot(act, w2, preferred_element_type=F32)


def _ffn_kernel(h_ref, x_ref, w1_ref, w3_ref, w2_ref, gfin_ref, *rest, final_norm):
    o_ref = rest[len(rest) // 2]
    f = pl.program_id(1)

    @pl.when(f == 0)
    def _():
        o_ref[...] = x_ref[...]

    o_ref[...] += _swiglu(h_ref[...], w1_ref[...], w3_ref[...], w2_ref[...])

    if final_norm:
        @pl.when(f == pl.num_programs(1) - 1)
        def _():
            o_ref[...] = _rms(o_ref[...], gfin_ref[...])

    n_cast = len(rest) // 2
    for src, dst in zip(rest[:n_cast], rest[n_cast + 1:]):
        dst[...] = src[...].astype(dst.dtype)


def _ffn(h, x, w1, w3, w2, gfin, final_norm, tm, tf, cast_cols=(), cast_rows=()):
    n = h.shape[0]
    grid = (n // tm, FFN_DIM // tf)
    row = pl.BlockSpec((tm, D_MODEL), lambda i, f: (i, 0))
    in_specs = [row, row,
                pl.BlockSpec((D_MODEL, tf), lambda i, f: (0, f)),
                pl.BlockSpec((D_MODEL, tf), lambda i, f: (0, f)),
                pl.BlockSpec((tf, D_MODEL), lambda i, f: (f, 0)),
                pl.BlockSpec((1, D_MODEL), lambda i, f: (0, 0))]
    cast_specs, cast_shapes = [], []
    for arr, axis in [(a, 2) for a in cast_cols] + [(a, 1) for a in cast_rows]:
        per_expert = arr.shape[axis] // LANES
        assert arr.shape[0] * per_expert == grid[0] * grid[1], "one cast block per grid step"
        shape = (1, arr.shape[1], LANES) if axis == 2 else (1, LANES, arr.shape[2])

        def index(i, f, per_expert=per_expert, axis=axis):
            t = i * grid[1] + f
            return (t // per_expert, 0, t % per_expert) if axis == 2 else (t // per_expert, t % per_expert, 0)

        cast_specs.append(pl.BlockSpec(shape, index))
        cast_shapes.append(jax.ShapeDtypeStruct(arr.shape, BF16))
    outs = pl.pallas_call(
        functools.partial(_ffn_kernel, final_norm=final_norm),
        grid=grid, in_specs=in_specs + cast_specs, out_specs=[row] + cast_specs,
        out_shape=[jax.ShapeDtypeStruct((n, D_MODEL), F32)] + cast_shapes,
        compiler_params=_params(("arbitrary", "arbitrary")), name="ffn",
    )(h, x, w1, w3, w2, gfin, *cast_cols, *cast_rows)
    return outs[0], outs[1:]


MOE_SUB = 256
MOE_WIN = 128
MOE_ROWS = 128
MOE_TAIL = 64
ROW_ALIGN = 16


def _moe_kernel(h_ref, x_hbm, sel_t_ref, sel_ref, cw_ref, w1_ref, w3_ref, w2_ref, triu_ref, tril_ref, gfin_ref,
                o_ref, pos_t_ref, pos_ref, hs_ref, y_ref, pre_ref, x_sem, *, final_norm):
    tb = h_ref.shape[0]
    n_sub = tb // MOE_SUB
    blk, e, f = pl.program_id(0), pl.program_id(1), pl.program_id(2)
    n_e, n_f = pl.num_programs(1), pl.num_programs(2)

    @pl.when((blk == 0) & (e == 0) & (f == 0))
    def _():
        hs_ref[...] = jnp.zeros_like(hs_ref)
        y_ref[...] = jnp.zeros_like(y_ref)

    @pl.when((e == 0) & (f == 0))
    def _():
        x_copy = pltpu.make_async_copy(x_hbm.at[pl.ds(pl.multiple_of(blk * tb, tb), tb), :], o_ref, x_sem)
        x_copy.start()
        off_t = jnp.zeros((N_EXPERTS, 1), F32)
        off = jnp.zeros((1, LANES), F32)
        for s in range(n_sub):
            sl = slice(s * MOE_SUB, (s + 1) * MOE_SUB)
            sel_t = sel_t_ref[:, sl]
            within_t = jnp.dot(sel_t.astype(BF16), triu_ref[...], preferred_element_type=F32)
            pos_t_ref[:, sl] = jnp.where(sel_t > 0.0, within_t + off_t, -1.0)
            off_t = off_t + jnp.sum(sel_t, axis=1, keepdims=True)
            sel = sel_ref[sl, :]
            within = jnp.dot(tril_ref[...], sel.astype(BF16), preferred_element_type=F32)
            pos_ref[sl, :] = jnp.where(sel > 0.0, within + off, -1.0)
            off = off + jnp.sum(sel, axis=0, keepdims=True)
        x_copy.wait()

    def window(s):
        p0, p1 = pre_ref[s], pre_ref[s + 1]
        base = (p0 // ROW_ALIGN) * ROW_ALIGN
        return p0, base, (p1 - base + MOE_WIN - 1) // MOE_WIN

    @pl.when(f == 0)
    def _():
        run = jnp.int32(0)
        pre_ref[0] = run
        for s in range(n_sub):
            cnt = jnp.sum(sel_t_ref[pl.ds(e, 1), s * MOE_SUB:(s + 1) * MOE_SUB])
            run = run + cnt.astype(jnp.int32)
            pre_ref[s + 1] = run
        row = lax.broadcasted_iota(jnp.int32, (MOE_WIN, MOE_SUB), 0)
        row1 = lax.broadcasted_iota(jnp.int32, (MOE_WIN, 1), 0)

        def gather(s, j):
            p0, p1 = pre_ref[s], pre_ref[s + 1]
            _, base, _ = window(s)
            sl = slice(s * MOE_SUB, (s + 1) * MOE_SUB)
            r0 = pl.multiple_of(base + j * MOE_WIN, ROW_ALIGN)
            pos_i = pos_t_ref[pl.ds(e, 1), sl].astype(jnp.int32)
            one_hot = jnp.where(row == pos_i - r0, 1.0, 0.0).astype(BF16)
            rows = jnp.dot(one_hot, h_ref[sl, :], preferred_element_type=F32).astype(BF16)
            mine = (row1 >= p0 - r0) & (row1 < p1 - r0)
            hs_ref[pl.ds(r0, MOE_WIN), :] = jnp.where(mine, rows, hs_ref[pl.ds(r0, MOE_WIN), :])

        for s in range(n_sub):
            gather(s, 0)
        for s in range(n_sub):
            lax.fori_loop(1, window(s)[2], lambda j, c, s=s: (gather(s, j), c)[1], 0)

    full = 4 * MOE_ROWS
    n_total = pre_ref[n_sub]
    n_whole = n_total // full
    tail = n_total - n_whole * full
    widen = (tail > 0) & (tail <= MOE_TAIL) & (n_whole > 0)
    n_full = n_whole - widen.astype(jnp.int32)
    rest_start = n_whole * full
    n_units = jnp.where(widen, 0, (tail + MOE_ROWS - 1) // MOE_ROWS)
    rem = n_units % 4

    def expert_rows(r0, n_rows):
        y = _swiglu(hs_ref[pl.ds(r0, n_rows), :], w1_ref[0], w3_ref[0], w2_ref[0])
        y_ref[pl.ds(r0, n_rows), :] = jnp.where(f == 0, y, y_ref[pl.ds(r0, n_rows), :] + y)

    def full_chunk(c, carry):
        expert_rows(pl.multiple_of(c * full, full), full)
        return carry

    lax.fori_loop(0, n_full, full_chunk, 0)

    @pl.when(widen)
    def _():
        expert_rows(pl.multiple_of(n_full * full, full), full + MOE_TAIL)

    @pl.when(n_units == 4)
    def _():
        expert_rows(pl.multiple_of(rest_start, full), full)

    @pl.when((rem == 2) | (rem == 3))
    def _():
        expert_rows(pl.multiple_of(rest_start, 2 * MOE_ROWS), 2 * MOE_ROWS)

    @pl.when(rem % 2 == 1)
    def _():
        expert_rows(pl.multiple_of(rest_start + (n_units - 1) * MOE_ROWS, MOE_ROWS), MOE_ROWS)

    @pl.when(f == n_f - 1)
    def _():
        lane = lax.broadcasted_iota(jnp.int32, (MOE_SUB, LANES), 1)

        def scatter(s, j):
            _, base, _ = window(s)
            sl = slice(s * MOE_SUB, (s + 1) * MOE_SUB)
            r0 = pl.multiple_of(base + j * MOE_WIN, ROW_ALIGN)
            mine = lane == e
            pos_i = jnp.sum(jnp.where(mine, pos_ref[sl, :], 0.0), axis=1, keepdims=True).astype(jnp.int32)
            weight = jnp.sum(jnp.where(mine, cw_ref[sl, :], 0.0), axis=1, keepdims=True)
            one_hot = jnp.where(lane == pos_i - r0, 1.0, 0.0).astype(BF16)
            rows = y_ref[pl.ds(r0, MOE_WIN), :].astype(BF16)
            back = jnp.dot(one_hot, rows, preferred_element_type=F32)
            o_ref[sl, :] += back * weight

        for s in range(n_sub):
            scatter(s, 0)
        for s in range(n_sub):
            lax.fori_loop(1, window(s)[2], lambda j, c, s=s: (scatter(s, j), c)[1], 0)

        if final_norm:
            @pl.when(e == n_e - 1)
            def _():
                o_ref[...] = _rms(o_ref[...], gfin_ref[...])


def _moe(h, x, sel_t, sel, cw, w1, w3, w2, gfin, final_norm, tb, tf):
    n = h.shape[0]
    tb = min(tb, n)
    n_exp = w1.shape[0]
    triu = jnp.triu(jnp.ones((MOE_SUB, MOE_SUB), F32), k=1).astype(BF16)
    pad = 2 * MOE_WIN
    once = pl.Buffered(1)
    tok = pl.BlockSpec((tb, LANES), lambda i, e, f: (i, 0), pipeline_mode=once)
    tri = pl.BlockSpec((MOE_SUB, MOE_SUB), lambda i, e, f: (0, 0), pipeline_mode=once)
    in_specs = [pl.BlockSpec((tb, D_MODEL), lambda i, e, f: (i, 0), pipeline_mode=once),
                pl.BlockSpec(memory_space=pl.ANY),
                pl.BlockSpec((N_EXPERTS, tb), lambda i, e, f: (0, i), pipeline_mode=once),
                tok, tok,
                pl.BlockSpec((1, D_MODEL, tf), lambda i, e, f: (e, 0, f)),
                pl.BlockSpec((1, D_MODEL, tf), lambda i, e, f: (e, 0, f)),
                pl.BlockSpec((1, tf, D_MODEL), lambda i, e, f: (e, f, 0)),
                tri, tri,
                pl.BlockSpec((1, D_MODEL), lambda i, e, f: (0, 0), pipeline_mode=once)]
    return pl.pallas_call(
        functools.partial(_moe_kernel, final_norm=final_norm),
        grid=(n // tb, n_exp, FFN_DIM // tf), in_specs=in_specs,
        out_specs=pl.BlockSpec((tb, D_MODEL), lambda i, e, f: (i, 0), pipeline_mode=once),
        out_shape=jax.ShapeDtypeStruct((n, D_MODEL), F32),
        scratch_shapes=[pltpu.VMEM((N_EXPERTS, tb), F32),
                        pltpu.VMEM((tb, LANES), F32),
                        pltpu.VMEM((tb + pad, D_MODEL), BF16),
                        pltpu.VMEM((tb + pad, D_MODEL), F32),
                        pltpu.SMEM((tb // MOE_SUB + 1,), jnp.int32),
                        pltpu.SemaphoreType.DMA(())],
        compiler_params=_params(("arbitrary", "arbitrary", "arbitrary"), MOE_VMEM_LIMIT), name="moe",
    )(h, x, sel_t, sel, cw, w1, w3, w2, triu, triu.T, gfin)


def _qk_column_order():
    half = HEAD_DIM // 2
    order = []
    for g in range(len(DILATIONS)):
        base = g * GROUP_WIDTH
        for part in range(2):
            for hd in range(HEADS_PER_GROUP):
                order.extend(range(base + hd * HEAD_DIM + part * half, base + hd * HEAD_DIM + (part + 1) * half))
    return np.asarray(order, np.int32)


def _dft_tables(n):
    r = 32 if n % 32 == 0 else 1
    col = jnp.arange(n, dtype=jnp.int32)[None, :]

    def trig(rows):
        ang = ((rows[:, None] * col) % n).astype(F32) * (2.0 * np.pi / n)
        return jnp.cos(ang), jnp.sin(ang)

    ca, sa = trig(jnp.arange(n // r, dtype=jnp.int32) * r)
    cb, sb = trig(jnp.arange(r, dtype=jnp.int32))
    ca, sa, cb, sb = ca[:, None, :], sa[:, None, :], cb[None, :, :], sb[None, :, :]
    return (ca * cb - sa * sb).reshape(n, n), (sa * cb + ca * sb).reshape(n, n)


def _tables(s):
    half = HEAD_DIM // 2
    inv_freq = ROPE_THETA ** (-jnp.arange(half, dtype=F32) * 2.0 / HEAD_DIM)
    ang = jnp.arange(s, dtype=F32)[:, None] * inv_freq[None, :]
    cos_t = jnp.tile(jnp.cos(ang), (1, LANES // half))
    sin_t = jnp.tile(jnp.sin(ang), (1, LANES // half))
    cg, sg = _dft_tables(FOURIER_GROUP_DIM)
    eye = jnp.eye(FOURIER_GROUPS, dtype=F32)
    scale_c = FOURIER_GROUP_DIM ** -0.5
    cc = (jnp.kron(eye, cg) * scale_c).astype(BF16)
    sc = (jnp.kron(eye, sg) * scale_c).astype(BF16)
    q_len = s // SEQ_RADIX
    cq, sq = _dft_tables(q_len)
    seq_tab = (jnp.concatenate([cq, sq], axis=1) * s ** -0.5).astype(BF16)
    pos = jnp.arange(q_len, dtype=jnp.int32)[None, :]
    ang = ((jnp.arange(1, SEQ_RADIX, dtype=jnp.int32)[:, None] * pos) % s).astype(F32) * (2.0 * np.pi / s)
    seq_tw = jnp.broadcast_to(jnp.stack([jnp.cos(ang), jnp.sin(ang)], axis=1)[..., None],
                              (SEQ_RADIX - 1, 2, q_len, LANES))
    return cos_t, sin_t, cc, sc, seq_tab, seq_tw


def kernel(x, ln_mix, w_in, w_fourier, w_attn, w_out, ln_ffn, dense_w1, dense_w3, dense_w2,
           router_w, router_b, moe_w1, moe_w3, moe_w2, ln_final):
    b, s, _ = x.shape
    depth = w_in.shape[0]
    n = b * s
    cos_t, sin_t, cc, sc, seq_tab, seq_tw = _tables(s)
    order = _qk_column_order()
    gfin = ln_final.reshape(1, D_MODEL)
    expert_bf16 = []

    for layer in range(depth):
        w = w_in[layer]
        wq = w[:, COL_Q:COL_K][:, order]
        wk = w[:, COL_K:COL_V][:, order]
        w_l = jnp.concatenate([w[:, :COL_Q], wq, wk, w[:, COL_V:COL_GF]], axis=1).astype(BF16)
        w_gate = w[:, COL_GF:].astype(BF16)
        g1 = ln_mix[layer].reshape(1, D_MODEL)

        outs = _in_proj(x, g1, w_l, cc, sc, cos_t, sin_t, min(IN_PROJ_ROWS, s))
        ab, qkv = outs[0], outs[1:10]
        yf = _seq_dft(ab, seq_tab, seq_tw, SEQ_DFT_COLS)
        attn = [_attention(qkv[3 * g], qkv[3 * g + 1], qkv[3 * g + 2]) for g in range(len(DILATIONS))]

        last = layer == depth - 1
        i = layer // 2
        if layer % 2 == 0:
            router = None
        else:
            rw = jnp.zeros((D_MODEL, LANES), F32).at[:, :N_EXPERTS].set(router_w[i])
            rw_hi = rw.astype(BF16)
            rw_lo = (rw - rw_hi.astype(F32)).astype(BF16)
            rb = jnp.full((1, LANES), NEG_INF, F32).at[0, :N_EXPERTS].set(router_b[i])
            router = (jnp.concatenate([rw_hi, rw_lo], axis=1), rb)
        res = _mix_out(x, yf, attn, g1, w_gate, w_fourier[layer].astype(BF16), w_attn[layer].astype(BF16),
                       w_out[layer].astype(BF16), ln_ffn[layer].reshape(1, D_MODEL), router, MIX_ROWS)
        xn, h2 = res[0].reshape(n, D_MODEL), res[1].reshape(n, D_MODEL)
        if layer % 2 == 0:
            ride = (not last and
                    (n // FFN_ROWS) * (FFN_DIM // FFN_COLS) == N_EXPERTS * (FFN_DIM // LANES))
            y, expert_bf16 = _ffn(h2, xn, dense_w1[i].astype(BF16), dense_w3[i].astype(BF16),
                                  dense_w2[i].astype(BF16), gfin, last, FFN_ROWS, FFN_COLS,
                                  (moe_w1[i], moe_w3[i]) if ride else (), (moe_w2[i],) if ride else ())
        else:
            if not expert_bf16:
                expert_bf16 = [w[i].astype(BF16) for w in (moe_w1, moe_w3, moe_w2)]
            y = _moe(h2, xn, res[4], res[3], res[2], *expert_bf16, gfin, last, MOE_TOKENS, MOE_COLS)
        x = y.reshape(b, s, D_MODEL)
    return x
```

```python
import functools

import jax
import jax.numpy as jnp
import numpy as np
from jax import lax
from jax.experimental import pallas as pl
from jax.experimental.pallas import tpu as pltpu

D_MODEL = 1024
FOURIER_GROUPS = 4
FOURIER_GROUP_DIM = 128
FOURIER_WIDTH = FOURIER_GROUPS * FOURIER_GROUP_DIM
DILATIONS = (1, 4, 16)
HALF_SPAN = 64
HEADS_PER_GROUP = 4
HEAD_DIM = 64
GROUP_WIDTH = HEADS_PER_GROUP * HEAD_DIM
ATTN_WIDTH = len(DILATIONS) * GROUP_WIDTH
ROPE_THETA = 10000.0
FFN_DIM = 3584
N_EXPERTS = 8
RMS_EPS = 1e-6
NEG_INF = -1e30
LOG2E = float(np.log2(np.e))
LN2 = float(np.log(2.0))
LANES = 128

COL_UF = 0
COL_Q = FOURIER_WIDTH
COL_K = COL_Q + ATTN_WIDTH
COL_V = COL_K + ATTN_WIDTH
COL_GF = COL_V + ATTN_WIDTH
COL_GA = COL_GF + D_MODEL
IN_COLS = COL_GA + D_MODEL

VMEM_LIMIT = 56 * 1024 * 1024
MOE_VMEM_LIMIT = 60 * 1024 * 1024

IN_PROJ_ROWS = 1024
SEQ_DFT_COLS = 256
MIX_ROWS = 512
FFN_ROWS = 1024
FFN_COLS = 512
MOE_TOKENS = 2048
MOE_COLS = 1792

BF16 = jnp.bfloat16
F32 = jnp.float32


def _params(semantics, vmem_limit=VMEM_LIMIT):
    return pltpu.CompilerParams(dimension_semantics=semantics, vmem_limit_bytes=vmem_limit)


def _rms(x, gain):
    ms = jnp.mean(x * x, axis=-1, keepdims=True)
    return x * lax.rsqrt(ms + RMS_EPS) * gain


def _in_proj_kernel(x_ref, g_ref, w_ref, cc_ref, sc_ref, cos_ref, sin_ref,
                    ab_ref, q0_ref, k0_ref, v0_ref, q1_ref, k1_ref, v1_ref, q2_ref, k2_ref, v2_ref,
                    scr_ref):
    tm = x_ref.shape[1]
    h = _rms(x_ref[0], g_ref[...]).astype(BF16)

    def proj(col, width):
        return jnp.dot(h, w_ref[:, col:col + width], preferred_element_type=F32)

    uf = proj(COL_UF, FOURIER_WIDTH).astype(BF16)
    ab_ref[0] = jnp.dot(uf, cc_ref[...], preferred_element_type=F32).astype(BF16)
    ab_ref[1] = jnp.dot(uf, sc_ref[...], preferred_element_type=F32).astype(BF16)

    cos = cos_ref[...]
    sin = sin_ref[...]

    def rope(t):
        t1, t2 = t[:, :LANES], t[:, LANES:]
        return jnp.concatenate([t1 * cos - t2 * sin, t2 * cos + t1 * sin], axis=-1)

    def put(ref, val, d):
        if d == 1:
            ref[0, 0] = val.astype(BF16)
            return
        for c in range(GROUP_WIDTH // LANES):
            scr_ref[c] = val[:, c * LANES:(c + 1) * LANES]
        for r in range(d):
            for c in range(GROUP_WIDTH // LANES):
                ref[0, r, :, c * LANES:(c + 1) * LANES] = scr_ref[c, pl.ds(r, tm // d, stride=d), :].astype(BF16)

    outs = ((q0_ref, k0_ref, v0_ref), (q1_ref, k1_ref, v1_ref), (q2_ref, k2_ref, v2_ref))
    for g, d in enumerate(DILATIONS):
        qr, kr, vr = outs[g]
        put(qr, rope(proj(COL_Q + g * GROUP_WIDTH, GROUP_WIDTH)) * (HEAD_DIM ** -0.5 * LOG2E), d)
        put(kr, rope(proj(COL_K + g * GROUP_WIDTH, GROUP_WIDTH)), d)
        put(vr, proj(COL_V + g * GROUP_WIDTH, GROUP_WIDTH), d)


def _in_proj(x, gain, w, cc, sc, cos_t, sin_t, tm):
    b, s, _ = x.shape
    grid = (b, s // tm)
    qkv_shapes, qkv_specs = [], []
    for d in DILATIONS:
        for _ in range(3):
            qkv_shapes.append(jax.ShapeDtypeStruct((b, d, s // d, GROUP_WIDTH), BF16))
            qkv_specs.append(pl.BlockSpec((1, d, tm // d, GROUP_WIDTH), lambda bi, si: (bi, 0, si, 0)))
    out_shape = [jax.ShapeDtypeStruct((2, s, b * FOURIER_WIDTH), BF16)] + qkv_shapes
    out_specs = [pl.BlockSpec((2, tm, FOURIER_WIDTH), lambda bi, si: (0, si, bi))] + qkv_specs
    in_specs = [
        pl.BlockSpec((1, tm, D_MODEL), lambda bi, si: (bi, si, 0)),
        pl.BlockSpec((1, D_MODEL), lambda bi, si: (0, 0)),
        pl.BlockSpec((D_MODEL, COL_GF), lambda bi, si: (0, 0)),
        pl.BlockSpec((FOURIER_WIDTH, FOURIER_WIDTH), lambda bi, si: (0, 0)),
        pl.BlockSpec((FOURIER_WIDTH, FOURIER_WIDTH), lambda bi, si: (0, 0)),
        pl.BlockSpec((tm, LANES), lambda bi, si: (si, 0)),
        pl.BlockSpec((tm, LANES), lambda bi, si: (si, 0)),
    ]
    return pl.pallas_call(
        _in_proj_kernel, grid=grid, in_specs=in_specs, out_specs=out_specs, out_shape=out_shape,
        scratch_shapes=[pltpu.VMEM((GROUP_WIDTH // LANES, tm, LANES), F32)],
        compiler_params=_params(("parallel", "parallel")), name="in_proj",
    )(x, gain, w, cc, sc, cos_t, sin_t)


SEQ_RADIX = 4


def _seq_dft_kernel(ab_ref, tab_ref, tw_ref, o_ref, u_ref):
    q_len = tab_ref.shape[0]
    for c in range(o_ref.shape[2] // LANES):
        cols = slice(c * LANES, (c + 1) * LANES)
        re = [ab_ref[0, j, :, cols].astype(F32) for j in range(SEQ_RADIX)]
        im = [-ab_ref[1, j, :, cols].astype(F32) for j in range(SEQ_RADIX)]
        p_re, p_im, q_re, q_im = re[0] + re[2], im[0] + im[2], re[0] - re[2], im[0] - im[2]
        s_re, s_im, d_re, d_im = re[1] + re[3], im[1] + im[3], re[1] - re[3], im[1] - im[3]
        terms = ((p_re + s_re, p_im + s_im), (q_re + d_im, q_im - d_re),
                 (p_re - s_re, p_im - s_im), (q_re - d_im, q_im + d_re))
        for r, (t_re, t_im) in enumerate(terms):
            if r > 0:
                cos, sin = tw_ref[r - 1, 0], tw_ref[r - 1, 1]
                t_re, t_im = t_re * cos + t_im * sin, t_im * cos - t_re * sin
            u_ref[r, :q_len, cols] = t_re.astype(BF16)
            u_ref[r, q_len:, cols] = t_im.astype(BF16)
    for r in range(SEQ_RADIX):
        o_ref[r] = jnp.dot(tab_ref[...], u_ref[r], preferred_element_type=F32).astype(o_ref.dtype)


def _seq_dft(ab, tab, tw, bn):
    _, s, n = ab.shape
    q_len = s // SEQ_RADIX
    bn = min(bn, n)
    return pl.pallas_call(
        _seq_dft_kernel, grid=(n // bn,),
        in_specs=[pl.BlockSpec((2, SEQ_RADIX, q_len, bn), lambda j: (0, 0, 0, j)),
                  pl.BlockSpec((q_len, 2 * q_len), lambda j: (0, 0), pipeline_mode=pl.Buffered(1)),
                  pl.BlockSpec((SEQ_RADIX - 1, 2, q_len, LANES), lambda j: (0, 0, 0, 0),
                               pipeline_mode=pl.Buffered(1))],
        out_specs=pl.BlockSpec((SEQ_RADIX, q_len, bn), lambda j: (0, 0, j)),
        out_shape=jax.ShapeDtypeStruct((SEQ_RADIX, q_len, n), BF16),
        scratch_shapes=[pltpu.VMEM((SEQ_RADIX, 2 * q_len, bn), BF16)],
        compiler_params=_params(("parallel",)), name="seq_dft",
    )(ab.reshape(2, SEQ_RADIX, q_len, n), tab, tw)


ATTN_UNROLL = 32


def _attn_kernel(q_ref, k_ref, v_ref, o_ref, lse_ref, *, qb, tk, seq_len, unroll):
    lane = lax.broadcasted_iota(jnp.int32, (1, GROUP_WIDTH), 1)
    qk_head = (lane % LANES) // (HEAD_DIM // 2)
    v_head = lane // HEAD_DIM
    col_minus_row = (lax.broadcasted_iota(jnp.int32, (qb, tk), 1)
                     - lax.broadcasted_iota(jnp.int32, (qb, tk), 0))
    n_res = q_ref.shape[1]
    n_sub = seq_len // qb

    def query_block(res, sub):
        m0 = pl.multiple_of(sub * qb, qb)
        start = pl.multiple_of(jnp.clip(m0 - HALF_SPAN, 0, seq_len - tk), 16)
        q = q_ref[0, res, pl.ds(m0, qb), :]
        k = k_ref[0, res, pl.ds(start, tk), :]
        v = v_ref[0, res, pl.ds(start, tk), :]
        valid = jnp.abs(col_minus_row + (start - m0)) <= HALF_SPAN
        q_heads = jnp.concatenate(
            [jnp.where(qk_head == hd, q, jnp.zeros_like(q)) for hd in range(HEADS_PER_GROUP)], axis=0)
        s = lax.dot_general(q_heads, k, (((1,), (1,)), ((), ())), preferred_element_type=F32)
        s = jnp.where(valid[None], s.reshape(HEADS_PER_GROUP, qb, tk), NEG_INF)
        m = jnp.max(s, axis=-1, keepdims=True)
        p = jnp.exp2(s - m)
        l = jnp.sum(p, axis=-1, keepdims=True)
        pv = jnp.dot(p.astype(BF16).reshape(HEADS_PER_GROUP * qb, tk), v, preferred_element_type=F32)
        pv = pv.reshape(HEADS_PER_GROUP, qb, GROUP_WIDTH) / l
        lse = (m + jnp.log2(l)) * LN2
        o_acc = pv[0]
        lse_acc = jnp.broadcast_to(lse[0], (qb, GROUP_WIDTH))
        for hd in range(1, HEADS_PER_GROUP):
            o_acc = jnp.where(v_head == hd, pv[hd], o_acc)
            lse_acc = jnp.where(v_head == hd, lse[hd], lse_acc)
        o_ref[0, res, pl.ds(m0, qb), :] = o_acc.astype(o_ref.dtype)
        lse_ref[0, res, pl.ds(m0, qb), :] = lse_acc

    def body(it, carry):
        for j in range(unroll):
            if n_sub % unroll == 0:
                per_res = n_sub // unroll
                query_block(it // per_res, (it % per_res) * unroll + j)
            else:
                query_block(it * (unroll // n_sub) + j // n_sub, j % n_sub)
        return carry

    lax.fori_loop(0, n_res * n_sub // unroll, body, 0)


def _attention(q, k, v):
    b, d, seq_len, _ = q.shape
    qb = min(128, seq_len)
    tk = min(qb + 2 * HALF_SPAN, seq_len)
    n_sub = seq_len // qb
    n_res = min(d, max(1, ATTN_UNROLL // n_sub))
    unroll = min(ATTN_UNROLL, n_res * n_sub)
    assert (n_res * n_sub) % unroll == 0 and (n_sub % unroll == 0 or unroll % n_sub == 0)
    kern = functools.partial(_attn_kernel, qb=qb, tk=tk, seq_len=seq_len, unroll=unroll)
    blk = pl.BlockSpec((1, n_res, seq_len, GROUP_WIDTH), lambda bi, ri: (bi, ri, 0, 0))
    shape = (b, d, seq_len, GROUP_WIDTH)
    return pl.pallas_call(
        kern, grid=(b, d // n_res), in_specs=[blk, blk, blk], out_specs=[blk, blk],
        out_shape=[jax.ShapeDtypeStruct(shape, BF16), jax.ShapeDtypeStruct(shape, F32)],
        compiler_params=_params(("parallel", "parallel")), name=f"attn_d{d}",
    )(q, k, v)


def _mix_out_kernel(*refs, routed):
    (x_ref, yf_ref, o0_ref, l0_ref, o1_ref, l1_ref, o2_ref, l2_ref, g1_ref, wg_ref,
     wf_ref, wa_ref, wo_ref, g2_ref) = refs[:14]
    if routed:
        rw_ref, rb_ref, xn_ref, h2_ref, cw_ref, sel_ref, sel_t_ref, o_scr, l_scr, f_scr = refs[14:]
    else:
        xn_ref, h2_ref, o_scr, l_scr, f_scr = refs[14:]
    tm = x_ref.shape[1]

    def gathered(ref, scr, d):
        if d == 1:
            return ref[0]
        n_chunk = ref.shape[-1] // LANES
        for r in range(d):
            for c in range(n_chunk):
                scr[c, pl.ds(r, tm // d, stride=d), :] = ref[r, :, c * LANES:(c + 1) * LANES].astype(F32)
        return jnp.concatenate([scr[c] for c in range(n_chunk)], axis=-1)

    h1 =_rms(x_ref[0], g1_ref[...]).astype(BF16)
    gate_f = jax.nn.sigmoid(jnp.dot(h1, wg_ref[:, :D_MODEL], preferred_element_type=F32))
    gate_a = jax.nn.sigmoid(jnp.dot(h1, wg_ref[:, D_MODEL:], preferred_element_type=F32))
    y_mix = gathered(yf_ref, f_scr, SEQ_RADIX).astype(BF16)
    y_f = jnp.dot(y_mix, wf_ref[...], preferred_element_type=F32)

    o_refs, l_refs = (o0_ref, o1_ref, o2_ref), (l0_ref, l1_ref, l2_ref)
    os_, ls_ = [], []
    for g, d in enumerate(DILATIONS):
        os_.append(gathered(o_refs[g].at[0], o_scr.at[g], d))
        ls_.append(gathered(l_refs[g].at[0], l_scr.at[g], d))
    mx = jnp.maximum(jnp.maximum(ls_[0], ls_[1]), ls_[2])
    es = [jnp.exp(l - mx) for l in ls_]
    den = es[0] + es[1] + es[2]
    o_att = (es[0] * os_[0] + es[1] * os_[1] + es[2] * os_[2]) / den
    y_a = jnp.dot(o_att.astype(BF16), wa_ref[...], preferred_element_type=F32)
    z = gate_f * y_f + gate_a * y_a
    xn = x_ref[0] +jnp.dot(z.astype(BF16), wo_ref[...], preferred_element_type=F32)
    xn_ref[0] = xn
    h2 = _rms(xn, g2_ref[...])
    h2_hi = h2.astype(BF16)
    h2_ref[0] = h2_hi

    if routed:
        h2_lo = (h2 - h2_hi.astype(F32)).astype(BF16)
        hi = jnp.dot(h2_hi, rw_ref[...], preferred_element_type=F32)
        logits = (hi[:, :LANES] + hi[:, LANES:]
                  + jnp.dot(h2_lo, rw_ref[:, :LANES], preferred_element_type=F32)) + rb_ref[...]
        lane = lax.broadcasted_iota(jnp.int32, logits.shape, 1)
        big = jnp.int32(LANES)
        m1 = jnp.max(logits, axis=-1, keepdims=True)
        i1 = jnp.min(jnp.where(logits == m1, lane, big), axis=-1, keepdims=True)
        rest = jnp.where(lane == i1, NEG_INF * 2, logits)
        m2 = jnp.max(rest, axis=-1, keepdims=True)
        i2 = jnp.min(jnp.where(rest == m2, lane, big), axis=-1, keepdims=True)
        e2 = jnp.exp(m2 - m1)
        w1 = 1.0 / (1.0 + e2)
        w2 = e2 / (1.0 + e2)
        cw_ref[...] = jnp.where(lane == i1, w1, jnp.where(lane == i2, w2, 0.0))
        sel = jnp.where((lane == i1) | (lane == i2), 1.0, 0.0)
        sel_ref[...] = sel
        sel_t_ref[...] = sel.T[:N_EXPERTS, :]


def _mix_out(x, yf, attn, g1, wg, wf, wa, wo, g2, router, tm):
    b, s, _ = x.shape
    routed = router is not None
    row = pl.BlockSpec((1, tm, D_MODEL), lambda bi, si: (bi, si, 0))
    full = lambda shape: pl.BlockSpec(shape, lambda bi, si: (0,) * len(shape), pipeline_mode=pl.Buffered(1))
    in_specs = [row, pl.BlockSpec((SEQ_RADIX, tm // SEQ_RADIX, FOURIER_WIDTH), lambda bi, si: (0, si, bi))]
    args = [x, yf]
    for d, (o, l) in zip(DILATIONS, attn):
        spec = pl.BlockSpec((1, d, tm // d, GROUP_WIDTH), lambda bi, si: (bi, 0, si, 0))
        in_specs += [spec, spec]
        args += [o, l]
    in_specs += [full((1, D_MODEL)), full((D_MODEL, 2 * D_MODEL)), full((FOURIER_WIDTH, D_MODEL)),
                 full((GROUP_WIDTH, D_MODEL)), full((D_MODEL, D_MODEL)), full((1, D_MODEL))]
    args += [g1, wg, wf, wa, wo, g2]
    out_shape = [jax.ShapeDtypeStruct((b, s, D_MODEL), F32), jax.ShapeDtypeStruct((b, s, D_MODEL), BF16)]
    out_specs = [row, row]
    if routed:
        in_specs += [full((D_MODEL, 2 * LANES)), full((1, LANES))]
        args += list(router)
        tok = pl.BlockSpec((tm, LANES), lambda bi, si: (bi * (s // tm) + si, 0))
        out_shape += [jax.ShapeDtypeStruct((b * s, LANES), F32)] * 2 + [jax.ShapeDtypeStruct((N_EXPERTS, b * s), F32)]
        out_specs += [tok, tok, pl.BlockSpec((N_EXPERTS, tm), lambda bi, si: (0, bi * (s // tm) + si))]
    return pl.pallas_call(
        functools.partial(_mix_out_kernel, routed=routed), grid=(b, s // tm),
        in_specs=in_specs, out_specs=out_specs, out_shape=out_shape,
        scratch_shapes=[pltpu.VMEM((len(DILATIONS), GROUP_WIDTH // LANES, tm, LANES), F32)] * 2
        + [pltpu.VMEM((FOURIER_WIDTH // LANES, tm, LANES), F32)],
        compiler_params=_params(("parallel", "parallel")), name="mix_out",
    )(*args)


def _swiglu(h, w1, w3, w2):
    a = jnp.dot(h, w1, preferred_element_type=F32)
    g = jnp.dot(h, w3, preferred_element_type=F32)
    act = (a * jax.nn.sigmoid(a) * g).astype(BF16)
    return jnp.dot(act, w2, preferred_element_type=F32)


def _ffn_kernel(h_ref, x_ref, w1_ref, w3_ref, w2_ref, gfin_ref, *rest, final_norm):
    o_ref = rest[len(rest) // 2]
    f = pl.program_id(1)

    @pl.when(f == 0)
    def _():
        o_ref[...] = x_ref[...]

    o_ref[...] += _swiglu(h_ref[...], w1_ref[...], w3_ref[...], w2_ref[...])

    if final_norm:
        @pl.when(f == pl.num_programs(1) - 1)
        def _():
            o_ref[...] = _rms(o_ref[...], gfin_ref[...])

    n_cast = len(rest) // 2
    for src, dst in zip(rest[:n_cast], rest[n_cast + 1:]):
        dst[...] = src[...].astype(dst.dtype)


def _ffn(h, x, w1, w3, w2, gfin, final_norm, tm, tf, cast_cols=(), cast_rows=()):
    n = h.shape[0]
    grid = (n // tm, FFN_DIM // tf)
    row = pl.BlockSpec((tm, D_MODEL), lambda i, f: (i, 0))
    in_specs = [row, row,
                pl.BlockSpec((D_MODEL, tf), lambda i, f: (0, f)),
                pl.BlockSpec((D_MODEL, tf), lambda i, f: (0, f)),
                pl.BlockSpec((tf, D_MODEL), lambda i, f: (f, 0)),
                pl.BlockSpec((1, D_MODEL), lambda i, f: (0, 0))]
    cast_specs, cast_shapes = [], []
    for arr, axis in [(a, 2) for a in cast_cols] + [(a, 1) for a in cast_rows]:
        per_expert = arr.shape[axis] // LANES
        assert arr.shape[0] * per_expert == grid[0] * grid[1], "one cast block per grid step"
        shape = (1, arr.shape[1], LANES) if axis == 2 else (1, LANES, arr.shape[2])

        def index(i, f, per_expert=per_expert, axis=axis):
            t = i * grid[1] + f
            return (t // per_expert, 0, t % per_expert) if axis == 2 else (t // per_expert, t % per_expert, 0)

        cast_specs.append(pl.BlockSpec(shape, index))
        cast_shapes.append(jax.ShapeDtypeStruct(arr.shape, BF16))
    outs = pl.pallas_call(
        functools.partial(_ffn_kernel, final_norm=final_norm),
        grid=grid, in_specs=in_specs + cast_specs, out_specs=[row] + cast_specs,
        out_shape=[jax.ShapeDtypeStruct((n, D_MODEL), F32)] + cast_shapes,
        compiler_params=_params(("arbitrary", "arbitrary")), name="ffn",
    )(h, x, w1, w3, w2, gfin, *cast_cols, *cast_rows)
    return outs[0], outs[1:]


MOE_SUB = 256
MOE_WIN = 128
MOE_ROWS = 128
MOE_TAIL = 64
ROW_ALIGN = 16


def _moe_kernel(h_ref, x_hbm, sel_t_ref, sel_ref, cw_ref, w1_ref, w3_ref, w2_ref, triu_ref, tril_ref, gfin_ref,
                o_ref, pos_t_ref, pos_ref, hs_ref, y_ref, pre_ref, x_sem, *, final_norm):
    tb = h_ref.shape[0]
    n_sub = tb // MOE_SUB
    blk, e, f = pl.program_id(0), pl.program_id(1), pl.program_id(2)
    n_e, n_f = pl.num_programs(1), pl.num_programs(2)

    @pl.when((blk == 0) & (e == 0) & (f == 0))
    def _():
        hs_ref[...] = jnp.zeros_like(hs_ref)
        y_ref[...] = jnp.zeros_like(y_ref)

    @pl.when((e == 0) & (f == 0))
    def _():
        x_copy = pltpu.make_async_copy(x_hbm.at[pl.ds(pl.multiple_of(blk * tb, tb), tb), :], o_ref, x_sem)
        x_copy.start()
        off_t = jnp.zeros((N_EXPERTS, 1), F32)
        off = jnp.zeros((1, LANES), F32)
        for s in range(n_sub):
            sl = slice(s * MOE_SUB, (s + 1) * MOE_SUB)
            sel_t = sel_t_ref[:, sl]
            within_t = jnp.dot(sel_t.astype(BF16), triu_ref[...], preferred_element_type=F32)
            pos_t_ref[:, sl] = jnp.where(sel_t > 0.0, within_t + off_t, -1.0)
            off_t = off_t + jnp.sum(sel_t, axis=1, keepdims=True)
            sel = sel_ref[sl, :]
            within = jnp.dot(tril_ref[...], sel.astype(BF16), preferred_element_type=F32)
            pos_ref[sl, :] = jnp.where(sel > 0.0, within + off, -1.0)
            off = off + jnp.sum(sel, axis=0, keepdims=True)
        x_copy.wait()

    def window(s):
        p0, p1 = pre_ref[s], pre_ref[s + 1]
        base = (p0 // ROW_ALIGN) * ROW_ALIGN
        return p0, base, (p1 - base + MOE_WIN - 1) // MOE_WIN

    @pl.when(f == 0)
    def _():
        run = jnp.int32(0)
        pre_ref[0] = run
        for s in range(n_sub):
            cnt = jnp.sum(sel_t_ref[pl.ds(e, 1), s * MOE_SUB:(s + 1) * MOE_SUB])
            run = run + cnt.astype(jnp.int32)
            pre_ref[s + 1] = run
        row = lax.broadcasted_iota(jnp.int32, (MOE_WIN, MOE_SUB), 0)
        row1 = lax.broadcasted_iota(jnp.int32, (MOE_WIN, 1), 0)

        def gather(s, j):
            p0, p1 = pre_ref[s], pre_ref[s + 1]
            _, base, _ = window(s)
            sl = slice(s * MOE_SUB, (s + 1) * MOE_SUB)
            r0 = pl.multiple_of(base + j * MOE_WIN, ROW_ALIGN)
            pos_i = pos_t_ref[pl.ds(e, 1), sl].astype(jnp.int32)
            one_hot = jnp.where(row == pos_i - r0, 1.0, 0.0).astype(BF16)
            rows = jnp.dot(one_hot, h_ref[sl, :], preferred_element_type=F32).astype(BF16)
            mine = (row1 >= p0 - r0) & (row1 < p1 - r0)
            hs_ref[pl.ds(r0, MOE_WIN), :] = jnp.where(mine, rows, hs_ref[pl.ds(r0, MOE_WIN), :])

        for s in range(n_sub):
            gather(s, 0)
        for s in range(n_sub):
            lax.fori_loop(1, window(s)[2], lambda j, c, s=s: (gather(s, j), c)[1], 0)

    full = 4 * MOE_ROWS
    n_total = pre_ref[n_sub]
    n_whole = n_total // full
    tail = n_total - n_whole * full
    widen = (tail > 0) & (tail <= MOE_TAIL) & (n_whole > 0)
    n_full = n_whole - widen.astype(jnp.int32)
    rest_start = n_whole * full
    n_units = jnp.where(widen, 0, (tail + MOE_ROWS - 1) // MOE_ROWS)
    rem = n_units % 4

    def expert_rows(r0, n_rows):
        y = _swiglu(hs_ref[pl.ds(r0, n_rows), :], w1_ref[0], w3_ref[0], w2_ref[0])
        y_ref[pl.ds(r0, n_rows), :] = jnp.where(f == 0, y, y_ref[pl.ds(r0, n_rows), :] + y)

    def full_chunk(c, carry):
        expert_rows(pl.multiple_of(c * full, full), full)
        return carry

    lax.fori_loop(0, n_full, full_chunk, 0)

    @pl.when(widen)
    def _():
        expert_rows(pl.multiple_of(n_full * full, full), full + MOE_TAIL)

    @pl.when(n_units == 4)
    def _():
        expert_rows(pl.multiple_of(rest_start, full), full)

    @pl.when((rem == 2) | (rem == 3))
    def _():
        expert_rows(pl.multiple_of(rest_start, 2 * MOE_ROWS), 2 * MOE_ROWS)

    @pl.when(rem % 2 == 1)
    def _():
        expert_rows(pl.multiple_of(rest_start + (n_units - 1) * MOE_ROWS, MOE_ROWS), MOE_ROWS)

    @pl.when(f == n_f - 1)
    def _():
        lane = lax.broadcasted_iota(jnp.int32, (MOE_SUB, LANES), 1)

        def scatter(s, j):
            _, base, _ = window(s)
            sl = slice(s * MOE_SUB, (s + 1) * MOE_SUB)
            r0 = pl.multiple_of(base + j * MOE_WIN, ROW_ALIGN)
            mine = lane == e
            pos_i = jnp.sum(jnp.where(mine, pos_ref[sl, :], 0.0), axis=1, keepdims=True).astype(jnp.int32)
            weight = jnp.sum(jnp.where(mine, cw_ref[sl, :], 0.0), axis=1, keepdims=True)
            one_hot = jnp.where(lane == pos_i - r0, 1.0, 0.0).astype(BF16)
            rows = y_ref[pl.ds(r0, MOE_WIN), :].astype(BF16)
            back = jnp.dot(one_hot, rows, preferred_element_type=F32)
            o_ref[sl, :] += back * weight

        for s in range(n_sub):
            scatter(s, 0)
        for s in range(n_sub):
            lax.fori_loop(1, window(s)[2], lambda j, c, s=s: (scatter(s, j), c)[1], 0)

        if final_norm:
            @pl.when(e == n_e - 1)
            def _():
                o_ref[...] = _rms(o_ref[...], gfin_ref[...])


def _moe(h, x, sel_t, sel, cw, w1, w3, w2, gfin, final_norm, tb, tf):
    n = h.shape[0]
    tb = min(tb, n)
    n_exp = w1.shape[0]
    triu = jnp.triu(jnp.ones((MOE_SUB, MOE_SUB), F32), k=1).astype(BF16)
    pad = 2 * MOE_WIN
    once = pl.Buffered(1)
    tok = pl.BlockSpec((tb, LANES), lambda i, e, f: (i, 0), pipeline_mode=once)
    tri = pl.BlockSpec((MOE_SUB, MOE_SUB), lambda i, e, f: (0, 0), pipeline_mode=once)
    in_specs = [pl.BlockSpec((tb, D_MODEL), lambda i, e, f: (i, 0)),
                pl.BlockSpec(memory_space=pl.ANY),
                pl.BlockSpec((N_EXPERTS, tb), lambda i, e, f: (0, i), pipeline_mode=once),
                tok, tok,
                pl.BlockSpec((1, D_MODEL, tf), lambda i, e, f: (e, 0, f)),
                pl.BlockSpec((1, D_MODEL, tf), lambda i, e, f: (e, 0, f)),
                pl.BlockSpec((1, tf, D_MODEL), lambda i, e, f: (e, f, 0)),
                tri, tri,
                pl.BlockSpec((1, D_MODEL), lambda i, e, f: (0, 0), pipeline_mode=once)]
    return pl.pallas_call(
        functools.partial(_moe_kernel, final_norm=final_norm),
        grid=(n // tb, n_exp, FFN_DIM // tf), in_specs=in_specs,
        out_specs=pl.BlockSpec((tb, D_MODEL), lambda i, e, f: (i, 0), pipeline_mode=once),
        out_shape=jax.ShapeDtypeStruct((n, D_MODEL), F32),
        scratch_shapes=[pltpu.VMEM((N_EXPERTS, tb), F32),
                        pltpu.VMEM((tb, LANES), F32),
                        pltpu.VMEM((tb + pad, D_MODEL), BF16),
                        pltpu.VMEM((tb + pad, D_MODEL), F32),
                        pltpu.SMEM((tb // MOE_SUB + 1,), jnp.int32),
                        pltpu.SemaphoreType.DMA(())],
        compiler_params=_params(("arbitrary", "arbitrary", "arbitrary"), MOE_VMEM_LIMIT), name="moe",
    )(h, x, sel_t, sel, cw, w1, w3, w2, triu, triu.T, gfin)


def _qk_column_order():
    half = HEAD_DIM // 2
    order = []
    for g in range(len(DILATIONS)):
        base = g * GROUP_WIDTH
        for part in range(2):
            for hd in range(HEADS_PER_GROUP):
                order.extend(range(base + hd * HEAD_DIM + part * half, base + hd * HEAD_DIM + (part + 1) * half))
    return np.asarray(order, np.int32)


def _dft_tables(n):
    r = 32 if n % 32 == 0 else 1
    col = jnp.arange(n, dtype=jnp.int32)[None, :]

    def trig(rows):
        ang = ((rows[:, None] * col) % n).astype(F32) * (2.0 * np.pi / n)
        return jnp.cos(ang), jnp.sin(ang)

    ca, sa = trig(jnp.arange(n // r, dtype=jnp.int32) * r)
    cb, sb = trig(jnp.arange(r, dtype=jnp.int32))
    ca, sa, cb, sb = ca[:, None, :], sa[:, None, :], cb[None, :, :], sb[None, :, :]
    return (ca * cb - sa * sb).reshape(n, n), (sa * cb + ca * sb).reshape(n, n)


def _tables(s):
    half = HEAD_DIM // 2
    inv_freq = ROPE_THETA ** (-jnp.arange(half, dtype=F32) * 2.0 / HEAD_DIM)
    ang = jnp.arange(s, dtype=F32)[:, None] * inv_freq[None, :]
    cos_t = jnp.tile(jnp.cos(ang), (1, LANES // half))
    sin_t = jnp.tile(jnp.sin(ang), (1, LANES // half))
    cg, sg = _dft_tables(FOURIER_GROUP_DIM)
    eye = jnp.eye(FOURIER_GROUPS, dtype=F32)
    scale_c = FOURIER_GROUP_DIM ** -0.5
    cc = (jnp.kron(eye, cg) * scale_c).astype(BF16)
    sc = (jnp.kron(eye, sg) * scale_c).astype(BF16)
    q_len = s // SEQ_RADIX
    cq, sq = _dft_tables(q_len)
    seq_tab = (jnp.concatenate([cq, sq], axis=1) * s ** -0.5).astype(BF16)
    pos = jnp.arange(q_len, dtype=jnp.int32)[None, :]
    ang = ((jnp.arange(1, SEQ_RADIX, dtype=jnp.int32)[:, None] * pos) % s).astype(F32) * (2.0 * np.pi / s)
    seq_tw = jnp.broadcast_to(jnp.stack([jnp.cos(ang), jnp.sin(ang)], axis=1)[..., None],
                              (SEQ_RADIX - 1, 2, q_len, LANES))
    return cos_t, sin_t, cc, sc, seq_tab, seq_tw


def kernel(x, ln_mix, w_in, w_fourier, w_attn, w_out, ln_ffn, dense_w1, dense_w3, dense_w2,
           router_w, router_b, moe_w1, moe_w3, moe_w2, ln_final):
    b, s, _ = x.shape
    depth = w_in.shape[0]
    n = b * s
    cos_t, sin_t, cc, sc, seq_tab, seq_tw = _tables(s)
    order = _qk_column_order()
    gfin = ln_final.reshape(1, D_MODEL)
    expert_bf16 = []

    for layer in range(depth):
        w = w_in[layer]
        wq = w[:, COL_Q:COL_K][:, order]
        wk = w[:, COL_K:COL_V][:, order]
        w_l = jnp.concatenate([w[:, :COL_Q], wq, wk, w[:, COL_V:COL_GF]], axis=1).astype(BF16)
        w_gate = w[:, COL_GF:].astype(BF16)
        g1 = ln_mix[layer].reshape(1, D_MODEL)

        outs = _in_proj(x, g1, w_l, cc, sc, cos_t, sin_t, min(IN_PROJ_ROWS, s))
        ab, qkv = outs[0], outs[1:10]
        yf = _seq_dft(ab, seq_tab, seq_tw, SEQ_DFT_COLS)
        attn = [_attention(qkv[3 * g], qkv[3 * g + 1], qkv[3 * g + 2]) for g in range(len(DILATIONS))]

        last = layer == depth - 1
        i = layer // 2
        if layer % 2 == 0:
            router = None
        else:
            rw = jnp.zeros((D_MODEL, LANES), F32).at[:, :N_EXPERTS].set(router_w[i])
            rw_hi = rw.astype(BF16)
            rw_lo = (rw - rw_hi.astype(F32)).astype(BF16)
            rb = jnp.full((1, LANES), NEG_INF, F32).at[0, :N_EXPERTS].set(router_b[i])
            router = (jnp.concatenate([rw_hi, rw_lo], axis=1), rb)
        res = _mix_out(x, yf, attn, g1, w_gate, w_fourier[layer].astype(BF16), w_attn[layer].astype(BF16),
                       w_out[layer].astype(BF16), ln_ffn[layer].reshape(1, D_MODEL), router, MIX_ROWS)
        xn, h2 = res[0].reshape(n, D_MODEL), res[1].reshape(n, D_MODEL)
        if layer % 2 == 0:
            ride = (not last and
                    (n // FFN_ROWS) * (FFN_DIM // FFN_COLS) == N_EXPERTS * (FFN_DIM // LANES))
            y, expert_bf16 = _ffn(h2, xn, dense_w1[i].astype(BF16), dense_w3[i].astype(BF16),
                                  dense_w2[i].astype(BF16), gfin, last, FFN_ROWS, FFN_COLS,
                                  (moe_w1[i], moe_w3[i]) if ride else (), (moe_w2[i],) if ride else ())
        else:
            if not expert_bf16:
                expert_bf16 = [w[i].astype(BF16) for w in (moe_w1, moe_w3, moe_w2)]
            y = _moe(h2, xn, res[4], res[3], res[2], *expert_bf16, gfin, last, MOE_TOKENS, MOE_COLS)
        x = y.reshape(b, s, D_MODEL)
    return x
```
